```python
import jax, jax.numpy as jnp
from jax import lax
import numpy as np

D_MODEL = 2048
BATCH = 2
SEQ = 16384
DEPTH = 2

GRID_W = 64
CTX_LEN = 256
HEAD_DIM = 128
A_HEADS = 8
A_KV_HEADS = 2
A_WINDOW = 128
A_BLOCK = 128
B_HEADS = 8
NA_ROWS = 8
NA_COLS = 16
ROPE_THETA = 10000.0
A_Q_COLS = A_HEADS * HEAD_DIM
A_KV_COLS = A_KV_HEADS * HEAD_DIM
B_COLS = B_HEADS * HEAD_DIM
QKV_COLS = A_Q_COLS + 2 * A_KV_COLS + 3 * B_COLS
QKV_SPLITS = (A_Q_COLS, A_Q_COLS + A_KV_COLS, A_Q_COLS + 2 * A_KV_COLS, A_Q_COLS + 2 * A_KV_COLS + B_COLS, A_Q_COLS + 2 * A_KV_COLS + 2 * B_COLS)
MIX_WIDTH = A_Q_COLS + B_COLS
RW_HEAD = 64
RW_HEADS = D_MODEL // RW_HEAD
RW_DECAY_LORA = 96
RW_AAA_LORA = 96
RW_GATE_LORA = 256
RW_LNX_EPS = 64e-5
N_GROUPS = 4
EXPERTS_PER_GROUP = 8
N_EXPERTS = N_GROUPS * EXPERTS_PER_GROUP
TOP_K = 2
EXPERT_FF = 512
MOE_BLOCK = 128
NORM_EPS = 1e-6
NEG_INF = -1e30

kernel_name = 'hybrid_diffusion_backbone'


def rmsnorm(x, g):
    xf = x.astype(jnp.float32)
    y = xf * lax.rsqrt(jnp.mean(xf * xf, axis=-1, keepdims=True) + NORM_EPS)
    return (y * g.astype(jnp.float32)).astype(x.dtype)


def ada_split(silu_cond, w, b):
    mod = (silu_cond @ w + b)[..., None, :]
    return jnp.split(mod, 6, axis=-1)


def axial_rope(x, row_pos, col_pos):
    half = x.shape[-1] // 2
    quarter = half // 2
    inv_freq = ROPE_THETA ** (-jnp.arange(quarter, dtype=jnp.float32) / quarter)

    def rotate(xp, pos):
        ang = pos.astype(jnp.float32)[:, None] * inv_freq[None, :]
        cos, sin = jnp.cos(ang)[None, :, None, :], jnp.sin(ang)[None, :, None, :]
        x1, x2 = xp[..., :quarter], xp[..., quarter:]
        return jnp.concatenate([x1 * cos - x2 * sin, x1 * sin + x2 * cos], axis=-1)

    xf = x.astype(jnp.float32)
    return jnp.concatenate([rotate(xf[..., :half], row_pos), rotate(xf[..., half:], col_pos)], axis=-1).astype(x.dtype)


def softmax_with_sink(logits, sink):
    m = jnp.maximum(jnp.max(logits, axis=-1, keepdims=True), sink)
    p = jnp.exp(logits - m)
    return p / (jnp.sum(p, axis=-1, keepdims=True) + jnp.exp(sink - m))


def split_heads(p):
    B, L, _ = p.shape
    parts = jnp.split(p, QKV_SPLITS, axis=-1)
    return tuple(t.reshape(B, L, -1, HEAD_DIM) for t in parts)


def window_attention(q, k, v, kc, vc, sink):
    B, L, _, dh = q.shape
    G = A_HEADS // A_KV_HEADS
    nb = L // A_BLOCK
    scale = dh ** -0.5
    qb = q.reshape(B, nb, A_BLOCK, A_KV_HEADS, G, dh)
    pad = ((0, 0), (A_BLOCK, A_BLOCK), (0, 0), (0, 0))
    kp = jnp.pad(k, pad).reshape(B, nb + 2, A_BLOCK, A_KV_HEADS, dh)
    vp = jnp.pad(v, pad).reshape(B, nb + 2, A_BLOCK, A_KV_HEADS, dh)
    kw = jnp.concatenate([kp[:, :-2], kp[:, 1:-1], kp[:, 2:]], axis=2)
    vw = jnp.concatenate([vp[:, :-2], vp[:, 1:-1], vp[:, 2:]], axis=2)
    blk = jnp.arange(nb)[:, None]
    qpos = blk * A_BLOCK + jnp.arange(A_BLOCK)[None, :]
    kpos = (blk - 1) * A_BLOCK + jnp.arange(3 * A_BLOCK)[None, :]
    valid = (jnp.abs(qpos[:, :, None] - kpos[:, None, :]) <= A_WINDOW) & (kpos[:, None, :] >= 0) & (kpos[:, None, :] < L)
    s_win = jnp.einsum('bnqkgd,bnskd->bnkgqs', qb, kw).astype(jnp.float32) * scale
    s_win = jnp.where(valid[None, :, None, None], s_win, NEG_INF)
    s_ctx = jnp.einsum('bnqkgd,bckd->bnkgqc', qb, kc).astype(jnp.float32) * scale
    sink_b = sink.astype(jnp.float32).reshape(A_KV_HEADS, G)[None, None, :, :, None, None]
    p = softmax_with_sink(jnp.concatenate([s_win, s_ctx], axis=-1), sink_b).astype(v.dtype)
    nw = 3 * A_BLOCK
    o = jnp.einsum('bnkgqs,bnskd->bnqkgd', p[..., :nw], vw) + jnp.einsum('bnkgqc,bckd->bnqkgd', p[..., nw:], vc)
    return o.reshape(B, L, A_HEADS * dh)


def neighbourhood_attention(q, k, v, kc, vc, rpb):
    B, L, H, dh = q.shape
    rows = L // GRID_W
    kr = min(NA_ROWS, rows)
    scale = dh ** -0.5
    r = jnp.arange(rows)
    col = jnp.arange(GRID_W)
    r0 = jnp.clip(r - kr // 2, 0, rows - kr)
    c0 = jnp.clip(col - NA_COLS // 2, 0, GRID_W - NA_COLS)
    band = r0[:, None] + jnp.arange(kr)[None, :]
    qg = q.reshape(B, rows, GRID_W, H, dh)
    kg = k.reshape(B, rows, GRID_W, H, dh)[:, band]
    vg = v.reshape(B, rows, GRID_W, H, dh)[:, band]
    col_ok = (col[None, :] >= c0[:, None]) & (col[None, :] < c0[:, None] + NA_COLS)
    dr = band - r[:, None] + (NA_ROWS - 1)
    dc = jnp.clip(col[None, :] - col[:, None] + (NA_COLS - 1), 0, 2 * NA_COLS - 2)
    bias = rpb.astype(jnp.float32)[:, dr[:, None, :, None], dc[None, :, None, :]]
    s_nb = jnp.einsum('brqhd,brjwhd->bhrqjw', qg, kg).astype(jnp.float32) * scale + bias
    s_nb = jnp.where(col_ok[:, None, :], s_nb, NEG_INF).reshape(B, H, rows, GRID_W, kr * GRID_W)
    s_ctx = jnp.einsum('brqhd,bchd->bhrqc', qg, kc).astype(jnp.float32) * scale
    p = jax.nn.softmax(jnp.concatenate([s_nb, s_ctx], axis=-1), axis=-1).astype(v.dtype)
    nk = kr * GRID_W
    p_nb = p[..., :nk].reshape(B, H, rows, GRID_W, kr, GRID_W)
    o = jnp.einsum('bhrqjw,brjwhd->brqhd', p_nb, vg) + jnp.einsum('bhrqc,bchd->brqhd', p[..., nk:], vc)
    return o.reshape(B, L, H * dh)


def context_attention(qc, kc, vc, sink):
    B, C, Hq, dh = qc.shape
    Hk = kc.shape[2]
    G = Hq // Hk
    qg = qc.reshape(B, C, Hk, G, dh)
    s = jnp.einsum('bqkgd,bckd->bkgqc', qg, kc).astype(jnp.float32) * dh ** -0.5
    if sink is None:
        p = jax.nn.softmax(s, axis=-1)
    else:
        p = softmax_with_sink(s, sink.astype(jnp.float32).reshape(Hk, G)[None, :, :, None, None])
    o = jnp.einsum('bkgqc,bckd->bqkgd', p.astype(vc.dtype), vc)
    return o.reshape(B, C, Hq * dh)


def attention_layer(h, hc, w_in, w_out, a_q_gain, a_k_gain, a_sink, b_q_gain, b_k_gain, b_rpb, row_pos, col_pos, need_ctx_out):
    qa, ka, va, qb, kb, vb = split_heads(h @ w_in)
    qa_c, ka_c, va_c, qb_c, kb_c, vb_c = split_heads(hc @ w_in)
    qa = axial_rope(rmsnorm(qa, a_q_gain), row_pos, col_pos)
    ka = axial_rope(rmsnorm(ka, a_k_gain), row_pos, col_pos)
    qb, kb = rmsnorm(qb, b_q_gain), rmsnorm(kb, b_k_gain)
    ka_c, kb_c = rmsnorm(ka_c, a_k_gain), rmsnorm(kb_c, b_k_gain)
    ya = window_attention(qa, ka, va, ka_c, va_c, a_sink)
    yb = neighbourhood_attention(qb, kb, vb, kb_c, vb_c, b_rpb)
    y = jnp.concatenate([ya, yb], axis=-1) @ w_out
    if not need_ctx_out:
        return y, None
    yac = context_attention(rmsnorm(qa_c, a_q_gain), ka_c, va_c, a_sink)
    ybc = context_attention(rmsnorm(qb_c, b_q_gain), kb_c, vb_c, None)
    return y, jnp.concatenate([yac, ybc], axis=-1) @ w_out


def centred_shift(x):
    xp = jnp.pad(x, ((0, 0), (1, 1), (0, 0)))
    return 0.5 * (xp[:, :-2] + xp[:, 2:]) - x


def l2_normalize(x):
    xf = x.astype(jnp.float32)
    return (xf * lax.rsqrt(jnp.sum(xf * xf, axis=-1, keepdims=True) + 1e-12)).astype(x.dtype)


def rwkv_prepare(x, mu, wr, wk, wv, w0, w1, w2, a0, a1, a2, g1, g2, k_k, k_a):
    B, L, D = x.shape
    heads = lambda t: t.reshape(B, L, RW_HEADS, RW_HEAD)
    xx = centred_shift(x)
    xr, xw, xk, xv, xa, xg = (x + xx * mu[m] for m in range(6))
    r = heads(xr @ wr)
    k = xk @ wk
    v = heads(xv @ wv)
    g = jax.nn.sigmoid(xg @ g1) @ g2
    kk = l2_normalize(heads(k * k_k))
    dirs = []
    for d in range(2):
        w_log = -jax.nn.softplus(-(w0[d] + jnp.tanh(xw @ w1[d]) @ w2[d]).astype(jnp.float32)) - 0.5
        decay = heads(jnp.exp(-jnp.exp(w_log)))
        a = jax.nn.sigmoid(a0[d] + (xa @ a1[d]) @ a2[d])
        kd = heads(k * (1 + (a - 1) * k_a))
        dirs.append((decay, kd, kk * heads(a)))
    return r, v, g, kk, dirs


def wkv_scan(state0, r, decay, k, v, a, b, reverse):
    xs = tuple(jnp.moveaxis(t.astype(jnp.float32), 1, 0) for t in (r, decay, k, v, a, b))

    def step(S, inp):
        rt, wt, kt, vt, at, bt = inp
        sa = jnp.einsum('bhvk,bhk->bhv', S, at)
        S = S * wt[:, :, None, :] + sa[..., None] * bt[:, :, None, :] + vt[..., None] * kt[:, :, None, :]
        return S, jnp.einsum('bhvk,bhk->bhv', S, rt)

    state, ys = lax.scan(step, state0, xs, reverse=reverse)
    return jnp.moveaxis(ys, 0, 1), state


def rwkv_output(y, r, v, g, k_sum, r_k, lnx_w, lnx_b, wo):
    B, L, H, N = y.shape
    mu = jnp.mean(y, axis=-1, keepdims=True)
    var = jnp.mean(jnp.square(y - mu), axis=-1, keepdims=True)
    yn = ((y - mu) * lax.rsqrt(var + RW_LNX_EPS)).reshape(B, L, H * N) * lnx_w + lnx_b
    bonus = jnp.sum((r * k_sum * r_k).astype(jnp.float32), axis=-1, keepdims=True) * v.astype(jnp.float32)
    return ((yn + bonus.reshape(B, L, H * N)).astype(g.dtype) * g) @ wo


def rwkv_layer(h, hc, mu, wr, wk, wv, wo, w0, w1, w2, a0, a1, a2, g1, g2, k_k, k_a, r_k, lnx_w, lnx_b, need_ctx_out):
    prep = lambda t: rwkv_prepare(t, mu, wr, wk, wv, w0, w1, w2, a0, a1, a2, g1, g2, k_k, k_a)
    r, v, g, kk, dirs = prep(h)
    rc, vc, gc, kkc, dirs_c = prep(hc)
    zero = jnp.zeros((h.shape[0], RW_HEADS, RW_HEAD, RW_HEAD), jnp.float32)
    y_parts, yc_parts = [], []
    for d, reverse in enumerate((False, True)):
        decay_c, k_c, b_c = dirs_c[d]
        yc_d, state_c = wkv_scan(zero, rc, decay_c, k_c, vc, -kkc, b_c, reverse)
        decay, kd, bd = dirs[d]
        y_d, _ = wkv_scan(state_c, r, decay, kd, v, -kk, bd, reverse)
        y_parts.append(y_d)
        yc_parts.append(yc_d)
    out = rwkv_output(y_parts[0] + y_parts[1], r, v, g, dirs[0][1] + dirs[1][1], r_k, lnx_w, lnx_b, wo)
    if not need_ctx_out:
        return out, None
    out_c = rwkv_output(yc_parts[0] + yc_parts[1], rc, vc, gc, dirs_c[0][1] + dirs_c[1][1], r_k, lnx_w, lnx_b, wo)
    return out, out_c


def hierarchical_moe(xt, w_grp, w_exp, w1, w3, w2):
    n, D = xt.shape
    xf = xt.astype(jnp.float32)
    p_grp = jax.nn.softmax(xf @ w_grp.astype(jnp.float32), axis=-1)
    g_sel = jnp.argmax(p_grp, axis=-1)
    p_sel = jnp.take_along_axis(p_grp, g_sel[:, None], axis=-1)
    logits = (xf @ w_exp.astype(jnp.float32)).reshape(n, N_GROUPS, EXPERTS_PER_GROUP)
    logits_g = jnp.take_along_axis(logits, g_sel[:, None, None], axis=1)[:, 0]
    top_val, top_idx = lax.top_k(logits_g, TOP_K)
    gate = p_sel * jax.nn.softmax(top_val, axis=-1)
    expert = g_sel[:, None] * EXPERTS_PER_GROUP + top_idx
    flat_e = expert.reshape(-1)
    flat_w = gate.reshape(-1)
    flat_tok = jnp.repeat(jnp.arange(n, dtype=jnp.int32), TOP_K)
    order = jnp.argsort(flat_e)
    se, stok, sw = flat_e[order], flat_tok[order], flat_w[order]
    counts = jnp.bincount(flat_e, length=N_EXPERTS)
    padded = (counts + MOE_BLOCK - 1) // MOE_BLOCK * MOE_BLOCK
    pad_end = jnp.cumsum(padded)
    start = jnp.cumsum(counts) - counts
    dest = pad_end[se] - padded[se] + jnp.arange(n * TOP_K) - start[se]
    n_blk = (n * TOP_K + N_EXPERTS * (MOE_BLOCK - 1) + MOE_BLOCK - 1) // MOE_BLOCK
    slots = n_blk * MOE_BLOCK
    slot_tok = jnp.full((slots,), n, jnp.int32).at[dest].set(stok)
    slot_w = jnp.zeros((slots,), jnp.float32).at[dest].set(sw)
    blk_expert = jnp.minimum(jnp.searchsorted(pad_end, jnp.arange(n_blk) * MOE_BLOCK, side='right'), N_EXPERTS - 1)
    x_rows = jnp.concatenate([xt, jnp.zeros((1, D), xt.dtype)], axis=0)[slot_tok].reshape(n_blk, MOE_BLOCK, D)

    def expert_block(args):
        xb, e = args
        return (jax.nn.silu(xb @ w1[e]) * (xb @ w3[e])) @ w2[e]

    y = lax.map(expert_block, (x_rows, blk_expert)).reshape(slots, D)
    out = jnp.zeros((n + 1, D), xt.dtype).at[slot_tok].add(y * slot_w[:, None].astype(xt.dtype))
    return out[:n]


def setup_inputs(seed: int = 0) -> dict:
    key = jax.random.key(seed)
    keys = iter(jax.random.split(key, 64))

    def normal(shape, scale):
        return jax.random.normal(next(keys), shape, jnp.float32) * scale

    def uniform(shape, lo, hi):
        return jax.random.uniform(next(keys), shape, jnp.float32, lo, hi)

    D = D_MODEL
    n_att = (DEPTH + 1) // 2
    n_rw = DEPTH // 2
    return {
        'x': normal((BATCH, SEQ, D), 1.0),
        'c': normal((BATCH, D), 1.0),
        'ctx': normal((BATCH, CTX_LEN, D), 1.0),
        'c_ctx': normal((D,), 1.0),
        'ada_w': normal((DEPTH, D, 6 * D), 0.5 * D ** -0.5),
        'ada_b': normal((DEPTH, 6 * D), 0.02),
        'norm_mix_g': 1.0 + normal((DEPTH, D), 0.02),
        'norm_ffn_g': 1.0 + normal((DEPTH, D), 0.02),
        'attn_w_in': normal((n_att, D, QKV_COLS), D ** -0.5),
        'attn_w_out': normal((n_att, MIX_WIDTH, D), MIX_WIDTH ** -0.5),
        'a_q_gain': 1.0 + normal((n_att, HEAD_DIM), 0.02),
        'a_k_gain': 1.0 + normal((n_att, HEAD_DIM), 0.02),
        'a_sink': normal((n_att, A_HEADS), 1.0),
        'b_q_gain': 1.0 + normal((n_att, HEAD_DIM), 0.02),
        'b_k_gain': 1.0 + normal((n_att, HEAD_DIM), 0.02),
        'b_rpb': normal((n_att, B_HEADS, 2 * NA_ROWS - 1, 2 * NA_COLS - 1), 0.5),
        'rw_mu': uniform((n_rw, 6, D), 0.0, 1.0),
        'rw_wr': normal((n_rw, D, D), D ** -0.5),
        'rw_wk': normal((n_rw, D, D), D ** -0.5),
        'rw_wv': normal((n_rw, D, D), D ** -0.5),
        'rw_wo': normal((n_rw, D, D), D ** -0.5),
        'rw_w0': uniform((n_rw, 2, D), -6.0, -1.0),
        'rw_w1': normal((n_rw, 2, D, RW_DECAY_LORA), D ** -0.5),
        'rw_w2': normal((n_rw, 2, RW_DECAY_LORA, D), 0.3 * RW_DECAY_LORA ** -0.5),
        'rw_a0': normal((n_rw, 2, D), 0.5),
        'rw_a1': normal((n_rw, 2, D, RW_AAA_LORA), D ** -0.5),
        'rw_a2': normal((n_rw, 2, RW_AAA_LORA, D), 0.5 * RW_AAA_LORA ** -0.5),
        'rw_g1': normal((n_rw, D, RW_GATE_LORA), D ** -0.5),
        'rw_g2': normal((n_rw, RW_GATE_LORA, D), RW_GATE_LORA ** -0.5),
        'rw_k_k': 1.0 + normal((n_rw, D), 0.1),
        'rw_k_a': 1.0 + normal((n_rw, D), 0.1),
        'rw_r_k': normal((n_rw, RW_HEADS, RW_HEAD), 0.1),
        'rw_lnx_w': 1.0 + normal((n_rw, D), 0.02),
        'rw_lnx_b': normal((n_rw, D), 0.02),
        'moe_w_grp': normal((DEPTH, D, N_GROUPS), D ** -0.5),
        'moe_w_exp': normal((DEPTH, D, N_EXPERTS), D ** -0.5),
        'moe_w1': normal((DEPTH, N_EXPERTS, D, EXPERT_FF), D ** -0.5),
        'moe_w3': normal((DEPTH, N_EXPERTS, D, EXPERT_FF), D ** -0.5),
        'moe_w2': normal((DEPTH, N_EXPERTS, EXPERT_FF, D), EXPERT_FF ** -0.5),
    }


def reference(x, c, ctx, c_ctx, ada_w, ada_b, norm_mix_g, norm_ffn_g, attn_w_in, attn_w_out, a_q_gain, a_k_gain, a_sink, b_q_gain, b_k_gain, b_rpb, rw_mu, rw_wr, rw_wk, rw_wv, rw_wo, rw_w0, rw_w1, rw_w2, rw_a0, rw_a1, rw_a2, rw_g1, rw_g2, rw_k_k, rw_k_a, rw_r_k, rw_lnx_w, rw_lnx_b, moe_w_grp, moe_w_exp, moe_w1, moe_w3, moe_w2):
    B, L, D = x.shape
    t = jnp.arange(L)
    row_pos, col_pos = t // GRID_W, t % GRID_W
    cond = jax.nn.silu(c)
    cond_ctx = jax.nn.silu(c_ctx)
    xc = ctx
    for i in range(DEPTH):
        last = i == DEPTH - 1
        j = i // 2
        sh1, sc1, gt1, sh2, sc2, gt2 = ada_split(cond, ada_w[i], ada_b[i])
        csh1, csc1, cgt1, csh2, csc2, cgt2 = ada_split(cond_ctx, ada_w[i], ada_b[i])
        h = rmsnorm(x, norm_mix_g[i]) * (1 + sc1) + sh1
        hc = rmsnorm(xc, norm_mix_g[i]) * (1 + csc1) + csh1
        if i % 2 == 0:
            y, yc = attention_layer(h, hc, attn_w_in[j], attn_w_out[j], a_q_gain[j], a_k_gain[j], a_sink[j], b_q_gain[j], b_k_gain[j], b_rpb[j], row_pos, col_pos, not last)
        else:
            y, yc = rwkv_layer(h, hc, rw_mu[j], rw_wr[j], rw_wk[j], rw_wv[j], rw_wo[j], rw_w0[j], rw_w1[j], rw_w2[j], rw_a0[j], rw_a1[j], rw_a2[j], rw_g1[j], rw_g2[j], rw_k_k[j], rw_k_a[j], rw_r_k[j], rw_lnx_w[j], rw_lnx_b[j], not last)
        x = x + gt1 * y
        h = rmsnorm(x, norm_ffn_g[i]) * (1 + sc2) + sh2
        experts = (moe_w_grp[i], moe_w_exp[i], moe_w1[i], moe_w3[i], moe_w2[i])
        if last:
            x = x + gt2 * hierarchical_moe(h.reshape(B * L, D), *experts).reshape(B, L, D)
        else:
            xc = xc + cgt1 * yc
            hc = rmsnorm(xc, norm_ffn_g[i]) * (1 + csc2) + csh2
            f = hierarchical_moe(jnp.concatenate([h.reshape(B * L, D), hc.reshape(-1, D)], axis=0), *experts)
            x = x + gt2 * f[:B * L].reshape(B, L, D)
            xc = xc + cgt2 * f[B * L:].reshape(hc.shape)
    return x
```

```python
import functools
import math

import jax
import jax.numpy as jnp
from jax import lax
from jax.experimental import pallas as pl
from jax.experimental.pallas import tpu as pltpu

GRID_W = 64
HEAD_DIM = 128
A_HEADS = 8
A_KV_HEADS = 2
A_BLOCK = 128
B_HEADS = 8
NA_ROWS = 8
NA_COLS = 16
ROPE_THETA = 10000.0
A_Q_COLS = A_HEADS * HEAD_DIM
A_KV_COLS = A_KV_HEADS * HEAD_DIM
B_COLS = B_HEADS * HEAD_DIM
RW_HEAD = 64
RW_LNX_EPS = 64e-5
N_GROUPS = 4
EXPERTS_PER_GROUP = 8
N_EXPERTS = N_GROUPS * EXPERTS_PER_GROUP
TOP_K = 2
NORM_EPS = 1e-6
NEG_INF = -1e30

LANES = 128
VMEM_LIMIT_BYTES = 48 * 1024 * 1024
MOE_ROWS = 256
NB_ROWS = 8
WKV_CHUNK = 64
WKV_PAIRS = 8

BF16 = jnp.bfloat16
F32 = jnp.float32


def _params(*sem):
    return pltpu.CompilerParams(dimension_semantics=sem, vmem_limit_bytes=VMEM_LIMIT_BYTES)


def _nt(a, b):
    return lax.dot_general(a, b, (((1,), (1,)), ((), ())), preferred_element_type=F32)


def _mm(a, b):
    return jnp.dot(a, b, preferred_element_type=F32)


def _linear_kernel(*refs, mode, gated, precise):
    it = iter(refs)
    x_ref = next(it)
    if mode == "norm":
        g_ref, sc_ref, sh_ref = next(it), next(it), next(it)
    elif mode == "mix":
        xx_ref, mu_ref = next(it), next(it)
    w_ref = next(it)
    if gated:
        res_ref, gate_ref = next(it), next(it)
    o_ref = next(it)
    h_ref = next(it)

    @pl.when(pl.program_id(1) == 0)
    def _():
        x = x_ref[...].astype(F32)
        if mode == "norm":
            y = x * lax.rsqrt(jnp.mean(x * x, axis=-1, keepdims=True) + NORM_EPS)
            y = y * g_ref[...]
            x = y * (1.0 + sc_ref[0]) + sh_ref[0]
        elif mode == "mix":
            x = x + xx_ref[...].astype(F32) * mu_ref[...]
        h_ref[...] = x.astype(h_ref.dtype)

    if precise:
        acc = jnp.dot(h_ref[...], w_ref[...], preferred_element_type=F32,
                      precision=lax.Precision.HIGHEST)
    else:
        acc = _mm(h_ref[...], w_ref[...])
    if gated:
        acc = res_ref[...] + gate_ref[0] * acc
    o_ref[...] = acc.astype(o_ref.dtype)


def linear(x, w, *, norm=None, mix=None, gated=None, out_dtype=F32, precise=False, tm=512, tn=512):
    M, K = x.shape
    N = w.shape[1]
    tm = min(tm, M)
    tn = tn if N % tn == 0 else N
    assert M % tm == 0 and N % tn == 0
    mode = "norm" if norm is not None else ("mix" if mix is not None else "plain")
    wdt = F32 if precise else BF16
    args = [x]
    in_specs = [pl.BlockSpec((tm, K), lambda i, j: (i, 0))]
    if mode == "norm":
        g, sc, sh = norm
        nb = sc.shape[0]
        rpb = M // nb
        assert rpb % tm == 0
        args += [g.reshape(1, K).astype(F32), sc.reshape(nb, 1, K).astype(F32), sh.reshape(nb, 1, K).astype(F32)]
        in_specs += [pl.BlockSpec((1, K), lambda i, j: (0, 0)),
                     pl.BlockSpec((1, 1, K), lambda i, j: ((i * tm) // rpb, 0, 0)),
                     pl.BlockSpec((1, 1, K), lambda i, j: ((i * tm) // rpb, 0, 0))]
    elif mode == "mix":
        xx, mu = mix
        args += [xx, mu.reshape(1, K).astype(F32)]
        in_specs += [pl.BlockSpec((tm, K), lambda i, j: (i, 0)), pl.BlockSpec((1, K), lambda i, j: (0, 0))]
    args.append(w.astype(wdt))
    in_specs.append(pl.BlockSpec((K, tn), lambda i, j: (0, j)))
    if gated is not None:
        res, gate = gated
        nbg = gate.shape[0]
        rpg = M // nbg
        assert rpg % tm == 0
        args += [res, gate.reshape(nbg, 1, N).astype(F32)]
        in_specs += [pl.BlockSpec((tm, tn), lambda i, j: (i, j)),
                     pl.BlockSpec((1, 1, tn), lambda i, j: ((i * tm) // rpg, 0, j))]
    return pl.pallas_call(
        functools.partial(_linear_kernel, mode=mode, gated=gated is not None, precise=precise),
        out_shape=jax.ShapeDtypeStruct((M, N), out_dtype),
        grid=(M // tm, N // tn),
        in_specs=in_specs,
        out_specs=pl.BlockSpec((tm, tn), lambda i, j: (i, j)),
        scratch_shapes=[pltpu.VMEM((tm, K), wdt)],
        compiler_params=_params("parallel", "arbitrary"),
        name="linear_" + mode,
    )(*args)


def _win_attn_kernel(q_ref, kp_ref, kc_ref, kn_ref, vp_ref, vc_ref, vn_ref, kx_ref, vx_ref, sink_ref,
                     o_ref, *, nblk, groups, scale):
    n = pl.program_id(2)
    blk = A_BLOCK
    q = q_ref[0]
    qs = jnp.concatenate([q[:, g * HEAD_DIM:(g + 1) * HEAD_DIM] for g in range(groups)], axis=0)
    rows = groups * blk
    s_p = _nt(qs, kp_ref[0]) * scale
    s_c = _nt(qs, kc_ref[0]) * scale
    s_n = _nt(qs, kn_ref[0]) * scale
    s_x = _nt(qs, kx_ref[0]) * scale
    qi = lax.broadcasted_iota(jnp.int32, (rows, blk), 0) % blk
    kj = lax.broadcasted_iota(jnp.int32, (rows, blk), 1)
    s_p = jnp.where((kj >= qi) & (n > 0), s_p, NEG_INF)
    s_n = jnp.where((kj <= qi) & (n < nblk - 1), s_n, NEG_INF)
    sink = sink_ref[0][:, :1]
    m = jnp.maximum(jnp.maximum(jnp.max(s_p, axis=-1, keepdims=True), jnp.max(s_c, axis=-1, keepdims=True)),
                    jnp.maximum(jnp.max(s_n, axis=-1, keepdims=True), jnp.max(s_x, axis=-1, keepdims=True)))
    m = jnp.maximum(m, sink)
    p_p, p_c, p_n, p_x = jnp.exp(s_p - m), jnp.exp(s_c - m), jnp.exp(s_n - m), jnp.exp(s_x - m)
    den = (jnp.sum(p_p, axis=-1, keepdims=True) + jnp.sum(p_c, axis=-1, keepdims=True)
           + jnp.sum(p_n, axis=-1, keepdims=True) + jnp.sum(p_x, axis=-1, keepdims=True) + jnp.exp(sink - m))
    o = (_mm(p_p.astype(BF16), vp_ref[0]) + _mm(p_c.astype(BF16), vc_ref[0])
         + _mm(p_n.astype(BF16), vn_ref[0]) + _mm(p_x.astype(BF16), vx_ref[0])) / den
    for g in range(groups):
        o_ref[0, :, g * HEAD_DIM:(g + 1) * HEAD_DIM] = o[g * blk:(g + 1) * blk].astype(o_ref.dtype)


def window_attention(q, k, v, kx, vx, sink):
    B, L, _ = q.shape
    C = kx.shape[1]
    groups = A_HEADS // A_KV_HEADS
    nblk = L // A_BLOCK
    gw = groups * HEAD_DIM
    sink_t = jnp.broadcast_to(
        jnp.repeat(sink.astype(F32).reshape(A_KV_HEADS, groups), A_BLOCK, axis=1)[:, :, None],
        (A_KV_HEADS, groups * A_BLOCK, LANES))
    kv_spec = lambda f: pl.BlockSpec((1, A_BLOCK, HEAD_DIM), f)
    prev = lambda b, h, n: (b, jnp.maximum(n - 1, 0), h)
    cur = lambda b, h, n: (b, n, h)
    nxt = lambda b, h, n: (b, jnp.minimum(n + 1, nblk - 1), h)
    ctx_spec = pl.BlockSpec((1, C, HEAD_DIM), lambda b, h, n: (b, 0, h))
    return pl.pallas_call(
        functools.partial(_win_attn_kernel, nblk=nblk, groups=groups, scale=HEAD_DIM ** -0.5),
        out_shape=jax.ShapeDtypeStruct((B, L, A_Q_COLS), BF16),
        grid=(B, A_KV_HEADS, nblk),
        in_specs=[pl.BlockSpec((1, A_BLOCK, gw), cur),
                  kv_spec(prev), kv_spec(cur), kv_spec(nxt), kv_spec(prev), kv_spec(cur), kv_spec(nxt),
                  ctx_spec, ctx_spec,
                  pl.BlockSpec((1, groups * A_BLOCK, LANES), lambda b, h, n: (h, 0, 0))],
        out_specs=pl.BlockSpec((1, A_BLOCK, gw), cur),
        compiler_params=_params("parallel", "parallel", "arbitrary"),
        name="window_attention",
    )(q, k, k, k, v, v, v, kx, vx, sink_t)


def _nbr_attn_kernel(q_ref, kp_ref, kc_ref, kn_ref, vp_ref, vc_ref, vn_ref, kx_ref, vx_ref, bias_ref,
                     o_ref, kbuf, vbuf, *, grid_rows, scale):
    rb = pl.program_id(2)
    tok = NB_ROWS * GRID_W
    nk = NA_ROWS * GRID_W
    kbuf[0:tok] = kp_ref[0]
    kbuf[tok:2 * tok] = kc_ref[0]
    kbuf[2 * tok:3 * tok] = kn_ref[0]
    vbuf[0:tok] = vp_ref[0]
    vbuf[tok:2 * tok] = vc_ref[0]
    vbuf[2 * tok:3 * tok] = vn_ref[0]
    kx = kx_ref[0]
    vx = vx_ref[0]
    for rr in range(NB_ROWS):
        r = rb * NB_ROWS + rr
        r0 = jnp.clip(r - NA_ROWS // 2, 0, grid_rows - NA_ROWS)
        off = pl.multiple_of((r0 - (rb - 1) * NB_ROWS) * GRID_W, GRID_W)
        ks = kbuf[pl.ds(off, nk), :]
        vs = vbuf[pl.ds(off, nk), :]
        qr = q_ref[0, rr * GRID_W:(rr + 1) * GRID_W, :]
        s = _nt(qr, ks) * scale + bias_ref[0, r0 - r + NA_ROWS - 1]
        sx = _nt(qr, kx) * scale
        m = jnp.maximum(jnp.max(s, axis=-1, keepdims=True), jnp.max(sx, axis=-1, keepdims=True))
        p = jnp.exp(s - m)
        px = jnp.exp(sx - m)
        den = jnp.sum(p, axis=-1, keepdims=True) + jnp.sum(px, axis=-1, keepdims=True)
        o = (_mm(p.astype(BF16), vs) + _mm(px.astype(BF16), vx)) / den
        o_ref[0, rr * GRID_W:(rr + 1) * GRID_W, :] = o.astype(o_ref.dtype)


def _nbr_bias_table(rpb):
    col = jnp.arange(GRID_W)
    c0 = jnp.clip(col - NA_COLS // 2, 0, GRID_W - NA_COLS)
    col_ok = (col[None, :] >= c0[:, None]) & (col[None, :] < c0[:, None] + NA_COLS)
    dc = jnp.clip(col[None, :] - col[:, None] + (NA_COLS - 1), 0, 2 * NA_COLS - 2)
    dr = jnp.arange(NA_ROWS)[:, None] + jnp.arange(NA_ROWS)[None, :]
    t = rpb.astype(F32)[:, dr[:, None, :, None], dc[None, :, None, :]]
    t = jnp.where(col_ok[None, None, :, None, :], t, NEG_INF)
    return t.reshape(rpb.shape[0], NA_ROWS, GRID_W, NA_ROWS * GRID_W)


def neighbourhood_attention(q, k, v, kx, vx, rpb):
    B, L, _ = q.shape
    C = kx.shape[1]
    grid_rows = L // GRID_W
    tok = NB_ROWS * GRID_W
    assert grid_rows >= NA_ROWS and L % tok == 0 and NB_ROWS == NA_ROWS
    nrb = L // tok
    bias = _nbr_bias_table(rpb)
    spec = lambda f: pl.BlockSpec((1, tok, HEAD_DIM), f)
    prev = lambda b, h, n: (b, jnp.maximum(n - 1, 0), h)
    cur = lambda b, h, n: (b, n, h)
    nxt = lambda b, h, n: (b, jnp.minimum(n + 1, nrb - 1), h)
    ctx_spec = pl.BlockSpec((1, C, HEAD_DIM), lambda b, h, n: (b, 0, h))
    return pl.pallas_call(
        functools.partial(_nbr_attn_kernel, grid_rows=grid_rows, scale=HEAD_DIM ** -0.5),
        out_shape=jax.ShapeDtypeStruct((B, L, B_COLS), BF16),
        grid=(B, B_HEADS, nrb),
        in_specs=[spec(cur), spec(prev), spec(cur), spec(nxt), spec(prev), spec(cur), spec(nxt),
                  ctx_spec, ctx_spec,
                  pl.BlockSpec((1, NA_ROWS, GRID_W, NA_ROWS * GRID_W), lambda b, h, n: (h, 0, 0, 0))],
        out_specs=spec(cur),
        scratch_shapes=[pltpu.VMEM((3 * tok, HEAD_DIM), BF16), pltpu.VMEM((3 * tok, HEAD_DIM), BF16)],
        compiler_params=_params("parallel", "parallel", "arbitrary"),
        name="neighbourhood_attention",
    )(q, k, k, k, v, v, v, kx, vx, bias)


def _ctx_attn_kernel(q_ref, k_ref, v_ref, sink_ref, o_ref, *, scale):
    s = _nt(q_ref[0], k_ref[0]) * scale
    sink = sink_ref[0][:1, :1]
    m = jnp.maximum(jnp.max(s, axis=-1, keepdims=True), sink)
    p = jnp.exp(s - m)
    den = jnp.sum(p, axis=-1, keepdims=True) + jnp.exp(sink - m)
    o_ref[0] = (_mm(p.astype(BF16), v_ref[0]) / den).astype(o_ref.dtype)


def context_attention(q, k, v, sink):
    B, C, qc = q.shape
    hq = qc // HEAD_DIM
    grp = hq // (k.shape[2] // HEAD_DIM)
    sink_t = jnp.broadcast_to(sink.astype(F32)[:, None, None], (hq, 8, LANES))
    return pl.pallas_call(
        functools.partial(_ctx_attn_kernel, scale=HEAD_DIM ** -0.5),
        out_shape=jax.ShapeDtypeStruct((B, C, qc), BF16),
        grid=(B, hq),
        in_specs=[pl.BlockSpec((1, C, HEAD_DIM), lambda b, h: (b, 0, h)),
                  pl.BlockSpec((1, C, HEAD_DIM), lambda b, h: (b, 0, h // grp)),
                  pl.BlockSpec((1, C, HEAD_DIM), lambda b, h: (b, 0, h // grp)),
                  pl.BlockSpec((1, 8, LANES), lambda b, h: (h, 0, 0))],
        out_specs=pl.BlockSpec((1, C, HEAD_DIM), lambda b, h: (b, 0, h)),
        compiler_params=_params("parallel", "parallel"),
        name="context_attention",
    )(q, k, v, sink_t)


def _wkv_pair(r, lw, k, v, a, b, st, tri_cum, strict_bd, incl_bd, lane_lo, st_bd):
    C = r.shape[0]
    cum = jnp.dot(tri_cum, lw, preferred_element_type=F32, precision=lax.Precision.HIGHEST)
    tot = jnp.sum(lw, axis=0, keepdims=True)
    g_inv = jnp.exp(-cum)
    g_tail = jnp.exp(tot - cum)
    a_t = a * jnp.exp(cum - lw)
    r_t = r * jnp.exp(cum)
    b_t = b * g_inv
    k_t = k * g_inv
    lhs = jnp.concatenate([a_t, r_t], axis=0).astype(BF16)
    zero = jnp.zeros_like(b_t)
    rhs = jnp.concatenate([jnp.where(lane_lo, b_t, zero), jnp.where(lane_lo, zero, b_t),
                           jnp.where(lane_lo, k_t, zero), jnp.where(lane_lo, zero, k_t)], axis=0).astype(BF16)
    prod = _nt(lhs, rhs)

    def block_diag(x, mask):
        return jnp.where(mask, jnp.concatenate([x, x], axis=0), 0.0)

    n_ab = block_diag(prod[0:C, 0:2 * C], strict_bd)
    n_ak = block_diag(prod[0:C, 2 * C:4 * C], strict_bd)
    n_rb = block_diag(prod[C:2 * C, 0:2 * C], incl_bd)
    n_rk = block_diag(prod[C:2 * C, 2 * C:4 * C], incl_bd)

    sh = _nt(lhs, st.astype(BF16))
    ah = sh[0:C]
    rh = sh[C:2 * C]
    v2 = jnp.concatenate([v, v], axis=0).astype(BF16)
    x = jnp.concatenate([ah, ah], axis=0) + _mm(n_ak.astype(BF16), v2)
    npow = n_ab.astype(BF16)
    steps = int(math.log2(C))
    for i in range(steps):
        x = x + _mm(npow, x.astype(BF16))
        if i + 1 < steps:
            npow = _mm(npow, npow).astype(BF16)
    u2 = x.astype(BF16)
    ys = jnp.concatenate([rh, rh], axis=0) + _mm(jnp.concatenate([n_rb, n_rk], axis=1).astype(BF16),
                                                  jnp.concatenate([u2, v2], axis=0))
    y = jnp.where(lane_lo, ys[0:C], ys[C:2 * C])
    u = jnp.where(lane_lo, x[0:C], x[C:2 * C])
    uv_t = jnp.concatenate([u, v], axis=0).T.astype(BF16)
    bk = jnp.concatenate([b * g_tail, k * g_tail], axis=0).astype(BF16)
    st_new = st * jnp.exp(tot) + jnp.where(st_bd, _mm(uv_t, bk), 0.0)
    return y, st_new


def _wkv_kernel(r_ref, lw_ref, k_ref, v_ref, a_ref, b_ref, y_ref, st_ref, *, reverse, pairs):
    C = WKV_CHUNK

    @pl.when(pl.program_id(2) == 0)
    def _():
        st_ref[...] = jnp.zeros_like(st_ref)

    ti = lax.broadcasted_iota(jnp.int32, (C, C), 0)
    si = lax.broadcasted_iota(jnp.int32, (C, C), 1)
    tri_cum = ((si >= ti) if reverse else (si <= ti)).astype(F32)
    rho = lax.broadcasted_iota(jnp.int32, (2 * C, 2 * C), 0)
    sig = lax.broadcasted_iota(jnp.int32, (2 * C, 2 * C), 1)
    same = (rho // C) == (sig // C)
    tt, ss = rho % C, sig % C
    strict_bd = same & ((ss > tt) if reverse else (ss < tt))
    incl_bd = same & ((ss >= tt) if reverse else (ss <= tt))
    lane_lo = lax.broadcasted_iota(jnp.int32, (C, LANES), 1) < RW_HEAD
    st_bd = (lax.broadcasted_iota(jnp.int32, (LANES, LANES), 0) // RW_HEAD) == (
        lax.broadcasted_iota(jnp.int32, (LANES, LANES), 1) // RW_HEAD)
    for p in range(pairs):
        sl = slice(p * LANES, (p + 1) * LANES)
        y, st_new = _wkv_pair(r_ref[0, :, sl], lw_ref[0, :, sl], k_ref[0, :, sl], v_ref[0, :, sl],
                              a_ref[0, :, sl], b_ref[0, :, sl], st_ref[p],
                              tri_cum, strict_bd, incl_bd, lane_lo, st_bd)
        y_ref[0, :, sl] = y
        st_ref[p] = st_new


def wkv_scan(r, lw, k, v, a, b, *, start, length, reverse):
    B, T, D = r.shape
    C = WKV_CHUNK
    assert start % C == 0 and length % C == 0 and 2 * C == LANES
    nchunk = length // C
    c0 = start // C
    pairs = min(WKV_PAIRS, D // LANES)
    width = pairs * LANES
    assert D % width == 0
    if reverse:
        imap = lambda bi, di, ci: (bi, c0 + nchunk - 1 - ci, di)
    else:
        imap = lambda bi, di, ci: (bi, c0 + ci, di)
    spec = pl.BlockSpec((1, C, width), imap)
    return pl.pallas_call(
        functools.partial(_wkv_kernel, reverse=reverse, pairs=pairs),
        out_shape=jax.ShapeDtypeStruct((B, T, D), F32),
        grid=(B, D // width, nchunk),
        in_specs=[spec] * 6,
        out_specs=spec,
        scratch_shapes=[pltpu.VMEM((pairs, LANES, LANES), F32)],
        compiler_params=_params("parallel", "parallel", "arbitrary"),
        name="wkv_scan_bwd" if reverse else "wkv_scan_fwd",
    )(r, lw, k, v, a, b)


def _moe_ffn_kernel(be_ref, nused_ref, x_ref, w1_ref, w3_ref, w2_ref, o_ref):
    i = pl.program_id(0)

    @pl.when(i < nused_ref[0])
    def _():
        x = x_ref[...]
        h1 = _mm(x, w1_ref[0])
        h3 = _mm(x, w3_ref[0])
        hh = (h1 * jax.nn.sigmoid(h1)) * h3
        o_ref[...] = _mm(hh.astype(BF16), w2_ref[0]).astype(o_ref.dtype)

    @pl.when(i >= nused_ref[0])
    def _():
        o_ref[...] = jnp.zeros_like(o_ref)


def moe_ffn(x_rows, blk_expert, n_used, w1, w3, w2):
    slots, D = x_rows.shape
    F = w1.shape[2]
    nblk = slots // MOE_ROWS
    grid_spec = pltpu.PrefetchScalarGridSpec(
        num_scalar_prefetch=2,
        grid=(nblk,),
        in_specs=[pl.BlockSpec((MOE_ROWS, D), lambda i, be, nu: (i, 0)),
                  pl.BlockSpec((1, D, F), lambda i, be, nu: (be[i], 0, 0)),
                  pl.BlockSpec((1, D, F), lambda i, be, nu: (be[i], 0, 0)),
                  pl.BlockSpec((1, F, D), lambda i, be, nu: (be[i], 0, 0))],
        out_specs=pl.BlockSpec((MOE_ROWS, D), lambda i, be, nu: (i, 0)),
    )
    return pl.pallas_call(
        _moe_ffn_kernel,
        out_shape=jax.ShapeDtypeStruct((slots, D), F32),
        grid_spec=grid_spec,
        compiler_params=_params("arbitrary"),
        name="moe_ffn",
    )(blk_expert, n_used, x_rows, w1, w3, w2)


def hierarchical_moe(h, w_grp, w_exp, w1, w3, w2):
    n, D = h.shape
    w_route = jnp.concatenate([w_grp, w_exp], axis=1).astype(F32)
    w_route = jnp.pad(w_route, ((0, 0), (0, LANES - w_route.shape[1])))
    route = linear(h, w_route, precise=True)
    p_grp = jax.nn.softmax(route[:, :N_GROUPS], axis=-1)
    g_sel = jnp.argmax(p_grp, axis=-1)
    p_sel = jnp.take_along_axis(p_grp, g_sel[:, None], axis=-1)
    logits = route[:, N_GROUPS:N_GROUPS + N_EXPERTS].reshape(n, N_GROUPS, EXPERTS_PER_GROUP)
    logits_g = jnp.take_along_axis(logits, g_sel[:, None, None], axis=1)[:, 0]
    top_val, top_idx = lax.top_k(logits_g, TOP_K)
    gate = p_sel * jax.nn.softmax(top_val, axis=-1)
    expert = (g_sel[:, None] * EXPERTS_PER_GROUP + top_idx).astype(jnp.int32)
    flat_e = expert.reshape(-1)
    onehot = (flat_e[:, None] == jnp.arange(N_EXPERTS, dtype=jnp.int32)[None, :]).astype(jnp.int32)
    csum = jnp.cumsum(onehot, axis=0)
    rank = jnp.take_along_axis(csum, flat_e[:, None], axis=1)[:, 0] - 1
    counts = csum[-1]
    padded = (counts + MOE_ROWS - 1) // MOE_ROWS * MOE_ROWS
    pad_end = jnp.cumsum(padded)
    dest = (pad_end - padded)[flat_e] + rank
    nblk = (n * TOP_K + N_EXPERTS * (MOE_ROWS - 1) + MOE_ROWS - 1) // MOE_ROWS
    slots = nblk * MOE_ROWS
    flat_tok = jnp.repeat(jnp.arange(n, dtype=jnp.int32), TOP_K)
    slot_tok = jnp.full((slots,), n, jnp.int32).at[dest].set(flat_tok)
    blk_expert = jnp.minimum(jnp.searchsorted(pad_end, jnp.arange(nblk, dtype=jnp.int32) * MOE_ROWS, side='right'),
                             N_EXPERTS - 1).astype(jnp.int32)
    n_used = (pad_end[-1] // MOE_ROWS).astype(jnp.int32).reshape(1)
    x_rows = jnp.concatenate([h.astype(BF16), jnp.zeros((1, D), BF16)], axis=0)[slot_tok]
    y = moe_ffn(x_rows, blk_expert, n_used, w1.astype(BF16), w3.astype(BF16), w2.astype(BF16))
    d2 = dest.reshape(n, TOP_K)
    return y[d2[:, 0]] * gate[:, 0:1] + y[d2[:, 1]] * gate[:, 1:2]


def _head_rmsnorm(x, g):
    y = x * lax.rsqrt(jnp.mean(x * x, axis=-1, keepdims=True) + NORM_EPS)
    return y * g.astype(F32)


def _rope_tables(L):
    t = jnp.arange(L)
    row_pos, col_pos = t // GRID_W, t % GRID_W
    quarter = HEAD_DIM // 4
    inv_freq = ROPE_THETA ** (-jnp.arange(quarter, dtype=F32) / quarter)
    ar = row_pos.astype(F32)[:, None] * inv_freq[None, :]
    ac = col_pos.astype(F32)[:, None] * inv_freq[None, :]
    return jnp.cos(ar), jnp.sin(ar), jnp.cos(ac), jnp.sin(ac)


def _axial_rope(x, tables):
    cr, sr, cc, sc = (t[None, :, None, :] for t in tables)
    q = HEAD_DIM // 4
    x1, x2, x3, x4 = x[..., :q], x[..., q:2 * q], x[..., 2 * q:3 * q], x[..., 3 * q:]
    return jnp.concatenate([x1 * cr - x2 * sr, x1 * sr + x2 * cr, x3 * cc - x4 * sc, x3 * sc + x4 * cc], axis=-1)


def _split_qkv(p):
    B, L, _ = p.shape
    cuts = (A_Q_COLS, A_Q_COLS + A_KV_COLS, A_Q_COLS + 2 * A_KV_COLS, A_Q_COLS + 2 * A_KV_COLS + B_COLS,
            A_Q_COLS + 2 * A_KV_COLS + 2 * B_COLS)
    return tuple(t.reshape(B, L, -1, HEAD_DIM) for t in jnp.split(p, cuts, axis=-1))


def attention_layer(x, xc, norm_g, sc, sh, csc, csh, w_in, w_out, a_q_gain, a_k_gain, a_sink, b_q_gain, b_k_gain,
                    b_rpb, gt, cgt):
    B, L, D = x.shape
    C = xc.shape[1]
    flat = lambda t: t.reshape(t.shape[0], t.shape[1], -1)
    qkv = linear(x.reshape(B * L, D), w_in, norm=(norm_g, sc, sh)).reshape(B, L, -1)
    qkv_c = linear(xc.reshape(B * C, D), w_in, norm=(norm_g, csc, csh)).reshape(B, C, -1)
    qa, ka, va, qb, kb, vb = _split_qkv(qkv)
    qa_c, ka_c, va_c, qb_c, kb_c, vb_c = _split_qkv(qkv_c)
    tables = _rope_tables(L)
    qa = flat(_axial_rope(_head_rmsnorm(qa, a_q_gain), tables).astype(BF16))
    ka = flat(_axial_rope(_head_rmsnorm(ka, a_k_gain), tables).astype(BF16))
    qb = flat(_head_rmsnorm(qb, b_q_gain).astype(BF16))
    kb = flat(_head_rmsnorm(kb, b_k_gain).astype(BF16))
    va, vb = flat(va.astype(BF16)), flat(vb.astype(BF16))
    ka_c = flat(_head_rmsnorm(ka_c, a_k_gain).astype(BF16))
    kb_c = flat(_head_rmsnorm(kb_c, b_k_gain).astype(BF16))
    va_c, vb_c = flat(va_c.astype(BF16)), flat(vb_c.astype(BF16))
    ya = window_attention(qa, ka, va, ka_c, va_c, a_sink)
    yb = neighbourhood_attention(qb, kb, vb, kb_c, vb_c, b_rpb)
    y = jnp.concatenate([ya, yb], axis=-1).reshape(B * L, -1)
    x_new = linear(y, w_out, gated=(x.reshape(B * L, D), gt)).reshape(B, L, D)
    qa_c = flat(_head_rmsnorm(qa_c, a_q_gain).astype(BF16))
    qb_c = flat(_head_rmsnorm(qb_c, b_q_gain).astype(BF16))
    yac = context_attention(qa_c, ka_c, va_c, a_sink)
    ybc = context_attention(qb_c, kb_c, vb_c, jnp.full((B_HEADS,), NEG_INF, F32))
    yc = jnp.concatenate([yac, ybc], axis=-1).reshape(B * C, -1)
    xc_new = linear(yc, w_out, gated=(xc.reshape(B * C, D), cgt)).reshape(B, C, D)
    return x_new, xc_new


def rwkv_layer(x, xc, norm_g, sc, sh, csc, csh, gt, mu, wr, wk, wv, wo, w0, w1, w2, a0, a1, a2, g1, g2, k_k, k_a,
               r_k, lnx_w, lnx_b):
    B, L, D = x.shape
    C = xc.shape[1]
    H = D // RW_HEAD
    T = L + 2 * C

    def modulated(t, s, b):
        tf = t.astype(F32)
        y = tf * lax.rsqrt(jnp.mean(tf * tf, axis=-1, keepdims=True) + NORM_EPS) * norm_g.astype(F32)
        return y * (1.0 + s[:, None, :]) + b[:, None, :]

    def shift(t):
        tp = jnp.pad(t, ((0, 0), (1, 1), (0, 0)))
        return 0.5 * (tp[:, :-2] + tp[:, 2:]) - t

    h = modulated(x, sc, sh)
    hc = modulated(xc, jnp.broadcast_to(csc, (B, D)), jnp.broadcast_to(csh, (B, D)))
    hcat = jnp.concatenate([hc, h, hc], axis=1).reshape(B * T, D)
    xx = jnp.concatenate([shift(hc), shift(h), shift(hc)], axis=1).reshape(B * T, D)
    proj = lambda m, w: linear(hcat, w, mix=(xx, mu[m]))
    r = proj(0, wr)
    k = proj(2, wk)
    v = proj(3, wv)
    g = linear(jax.nn.sigmoid(proj(5, g1)), g2)
    w_lora = jnp.tanh(proj(1, jnp.concatenate([w1[0], w1[1]], axis=1)))
    a_lora = proj(4, jnp.concatenate([a1[0], a1[1]], axis=1))
    heads = lambda t: t.reshape(B * T, H, RW_HEAD)
    kkf = heads(k * k_k)
    kk = (kkf * lax.rsqrt(jnp.sum(kkf * kkf, axis=-1, keepdims=True) + 1e-12)).reshape(B * T, D)
    nl = w1.shape[2]
    na = a1.shape[2]
    ys = []
    k_sum = 0.0
    for d, reverse in enumerate((False, True)):
        w_raw = w0[d] + linear(w_lora[:, d * nl:(d + 1) * nl], w2[d])
        lw = -jnp.exp(-jax.nn.softplus(-w_raw) - 0.5)
        ag = jax.nn.sigmoid(a0[d] + linear(a_lora[:, d * na:(d + 1) * na], a2[d]))
        kd = k * (1.0 + (ag - 1.0) * k_a)
        k_sum = k_sum + kd
        to3 = lambda t: t.reshape(B, T, D)
        y_d = wkv_scan(to3(r), to3(lw), to3(kd), to3(v), to3(-kk), to3(kk * ag),
                       start=C if reverse else 0, length=C + L, reverse=reverse)
        ys.append(y_d[:, C:C + L])
    lat = lambda t: t.reshape(B, T, D)[:, C:C + L]
    y = (ys[0] + ys[1]).reshape(B, L, H, RW_HEAD)
    mean = jnp.mean(y, axis=-1, keepdims=True)
    var = jnp.mean(jnp.square(y - mean), axis=-1, keepdims=True)
    yn = ((y - mean) * lax.rsqrt(var + RW_LNX_EPS)).reshape(B, L, D) * lnx_w + lnx_b
    rl = lat(r).reshape(B, L, H, RW_HEAD)
    bonus = jnp.sum(rl * lat(k_sum).reshape(B, L, H, RW_HEAD) * r_k, axis=-1, keepdims=True) * lat(v).reshape(
        B, L, H, RW_HEAD)
    z = (yn + bonus.reshape(B, L, D)) * lat(g)
    return linear(z.reshape(B * L, D), wo, gated=(x.reshape(B * L, D), gt)).reshape(B, L, D)


def _ffn_input(t, g, s, b):
    tf = t.astype(F32)
    y = tf * lax.rsqrt(jnp.mean(tf * tf, axis=-1, keepdims=True) + NORM_EPS) * g.astype(F32)
    return y * (1.0 + s[:, None, :]) + b[:, None, :]


def kernel(x, c, ctx, c_ctx, ada_w, ada_b, norm_mix_g, norm_ffn_g, attn_w_in, attn_w_out, a_q_gain, a_k_gain, a_sink, b_q_gain, b_k_gain, b_rpb, rw_mu, rw_wr, rw_wk, rw_wv, rw_wo, rw_w0, rw_w1, rw_w2, rw_a0, rw_a1, rw_a2, rw_g1, rw_g2, rw_k_k, rw_k_a, rw_r_k, rw_lnx_w, rw_lnx_b, moe_w_grp, moe_w_exp, moe_w1, moe_w3, moe_w2):
    B, L, D = x.shape
    C = ctx.shape[1]
    depth = ada_w.shape[0]
    assert depth == 2, "layer 0 is the attention mixer, layer 1 (last) the RWKV-7 mixer"
    pad = (-(B + 1)) % 8
    cond = jnp.concatenate([jax.nn.silu(c), jax.nn.silu(c_ctx)[None, :], jnp.zeros((pad, D), F32)], axis=0)
    xc = ctx
    for i in range(depth):
        last = i == depth - 1
        j = i // 2
        mod = linear(cond, ada_w[i], precise=True) + ada_b[i]
        sh1, sc1, gt1, sh2, sc2, gt2 = jnp.split(mod[:B], 6, axis=-1)
        csh1, csc1, cgt1, csh2, csc2, cgt2 = jnp.split(mod[B:B + 1], 6, axis=-1)
        if i % 2 == 0:
            x, xc_mix = attention_layer(x, xc, norm_mix_g[i], sc1, sh1, csc1, csh1, attn_w_in[j], attn_w_out[j],
                                        a_q_gain[j], a_k_gain[j], a_sink[j], b_q_gain[j], b_k_gain[j], b_rpb[j],
                                        gt1, cgt1)
        else:
            x = rwkv_layer(x, xc, norm_mix_g[i], sc1, sh1, csc1, csh1, gt1, rw_mu[j], rw_wr[j], rw_wk[j], rw_wv[j],
                           rw_wo[j], rw_w0[j], rw_w1[j], rw_w2[j], rw_a0[j], rw_a1[j], rw_a2[j], rw_g1[j], rw_g2[j],
                           rw_k_k[j], rw_k_a[j], rw_r_k[j], rw_lnx_w[j], rw_lnx_b[j])
            xc_mix = None
        h = _ffn_input(x, norm_ffn_g[i], sc2, sh2).reshape(B * L, D)
        experts = (moe_w_grp[i], moe_w_exp[i], moe_w1[i], moe_w3[i], moe_w2[i])
        if last:
            f = hierarchical_moe(h, *experts)
            x = x + gt2[:, None, :] * f.reshape(B, L, D)
        else:
            xc = xc_mix
            hc = _ffn_input(xc, norm_ffn_g[i], jnp.broadcast_to(csc2, (B, D)), jnp.broadcast_to(csh2, (B, D)))
            f = hierarchical_moe(jnp.concatenate([h, hc.reshape(B * C, D)], axis=0), *experts)
            x = x + gt2[:, None, :] * f[:B * L].reshape(B, L, D)
            xc = xc + cgt2[:, None, :] * f[B * L:].reshape(B, C, D)
    return x
```

```python
import functools
import math

import jax
import jax.numpy as jnp
from jax import lax
from jax.experimental import pallas as pl
from jax.experimental.pallas import tpu as pltpu

GRID_W = 64
HEAD_DIM = 128
A_HEADS = 8
A_KV_HEADS = 2
A_BLOCK = 128
B_HEADS = 8
NA_ROWS = 8
NA_COLS = 16
ROPE_THETA = 10000.0
A_Q_COLS = A_HEADS * HEAD_DIM
A_KV_COLS = A_KV_HEADS * HEAD_DIM
B_COLS = B_HEADS * HEAD_DIM
RW_HEAD = 64
RW_LNX_EPS = 64e-5
N_GROUPS = 4
EXPERTS_PER_GROUP = 8
N_EXPERTS = N_GROUPS * EXPERTS_PER_GROUP
TOP_K = 2
NORM_EPS = 1e-6
NEG_INF = -1e30

LANES = 128
VMEM_LIMIT_BYTES = 48 * 1024 * 1024
MOE_ROWS = 256
NB_ROWS = 8
WKV_CHUNK = 64
WKV_PAIRS = 8

BF16 = jnp.bfloat16
F32 = jnp.float32


def _params(*sem):
    return pltpu.CompilerParams(dimension_semantics=sem, vmem_limit_bytes=VMEM_LIMIT_BYTES)


def _nt(a, b):
    return lax.dot_general(a, b, (((1,), (1,)), ((), ())), preferred_element_type=F32)


def _mm(a, b):
    return jnp.dot(a, b, preferred_element_type=F32)


def _linear_kernel(*refs, mode, gated, precise):
    it = iter(refs)
    x_ref = next(it)
    if mode == "norm":
        g_ref, sc_ref, sh_ref = next(it), next(it), next(it)
    elif mode == "mix":
        xx_ref, mu_ref = next(it), next(it)
    w_ref = next(it)
    if gated:
        res_ref, gate_ref = next(it), next(it)
    o_ref = next(it)
    h_ref = next(it)

    @pl.when(pl.program_id(1) == 0)
    def _():
        x = x_ref[...].astype(F32)
        if mode == "norm":
            y = x * lax.rsqrt(jnp.mean(x * x, axis=-1, keepdims=True) + NORM_EPS)
            y = y * g_ref[...]
            x = y * (1.0 + sc_ref[0]) + sh_ref[0]
        elif mode == "mix":
            x = x + xx_ref[...].astype(F32) * mu_ref[...]
        h_ref[...] = x.astype(h_ref.dtype)

    if precise:
        acc = jnp.dot(h_ref[...], w_ref[...], preferred_element_type=F32,
                      precision=lax.Precision.HIGHEST)
    else:
        acc = _mm(h_ref[...], w_ref[...])
    if gated:
        acc = res_ref[...] + gate_ref[0] * acc
    o_ref[...] = acc.astype(o_ref.dtype)


def linear(x, w, *, norm=None, mix=None, gated=None, out_dtype=F32, precise=False, tm=512, tn=512):
    M, K = x.shape
    N = w.shape[1]
    tm = min(tm, M)
    tn = tn if N % tn == 0 else N
    assert M % tm == 0 and N % tn == 0
    mode = "norm" if norm is not None else ("mix" if mix is not None else "plain")
    wdt = F32 if precise else BF16
    args = [x]
    in_specs = [pl.BlockSpec((tm, K), lambda i, j: (i, 0))]
    if mode == "norm":
        g, sc, sh = norm
        nb = sc.shape[0]
        rpb = M // nb
        assert rpb % tm == 0
        args += [g.reshape(1, K).astype(F32), sc.reshape(nb, 1, K).astype(F32), sh.reshape(nb, 1, K).astype(F32)]
        in_specs += [pl.BlockSpec((1, K), lambda i, j: (0, 0)),
                     pl.BlockSpec((1, 1, K), lambda i, j: ((i * tm) // rpb, 0, 0)),
                     pl.BlockSpec((1, 1, K), lambda i, j: ((i * tm) // rpb, 0, 0))]
    elif mode == "mix":
        xx, mu = mix
        args += [xx, mu.reshape(1, K).astype(F32)]
        in_specs += [pl.BlockSpec((tm, K), lambda i, j: (i, 0)), pl.BlockSpec((1, K), lambda i, j: (0, 0))]
    args.append(w.astype(wdt))
    in_specs.append(pl.BlockSpec((K, tn), lambda i, j: (0, j)))
    if gated is not None:
        res, gate = gated
        nbg = gate.shape[0]
        rpg = M // nbg
        assert rpg % tm == 0
        args += [res, gate.reshape(nbg, 1, N).astype(F32)]
        in_specs += [pl.BlockSpec((tm, tn), lambda i, j: (i, j)),
                     pl.BlockSpec((1, 1, tn), lambda i, j: ((i * tm) // rpg, 0, j))]
    return pl.pallas_call(
        functools.partial(_linear_kernel, mode=mode, gated=gated is not None, precise=precise),
        out_shape=jax.ShapeDtypeStruct((M, N), out_dtype),
        grid=(M // tm, N // tn),
        in_specs=in_specs,
        out_specs=pl.BlockSpec((tm, tn), lambda i, j: (i, j)),
        scratch_shapes=[pltpu.VMEM((tm, K), wdt)],
        compiler_params=_params("parallel", "arbitrary"),
        name="linear_" + mode,
    )(*args)


def _win_attn_kernel(q_ref, kp_ref, kc_ref, kn_ref, vp_ref, vc_ref, vn_ref, kx_ref, vx_ref, sink_ref,
                     o_ref, *, nblk, groups, scale):
    n = pl.program_id(2)
    blk = A_BLOCK
    q = q_ref[0]
    qs = jnp.concatenate([q[:, g * HEAD_DIM:(g + 1) * HEAD_DIM] for g in range(groups)], axis=0)
    rows = groups * blk
    s_p = _nt(qs, kp_ref[0]) * scale
    s_c = _nt(qs, kc_ref[0]) * scale
    s_n = _nt(qs, kn_ref[0]) * scale
    s_x = _nt(qs, kx_ref[0]) * scale
    qi = lax.broadcasted_iota(jnp.int32, (rows, blk), 0) % blk
    kj = lax.broadcasted_iota(jnp.int32, (rows, blk), 1)
    s_p = jnp.where((kj >= qi) & (n > 0), s_p, NEG_INF)
    s_n = jnp.where((kj <= qi) & (n < nblk - 1), s_n, NEG_INF)
    sink = sink_ref[0][:, :1]
    m = jnp.maximum(jnp.maximum(jnp.max(s_p, axis=-1, keepdims=True), jnp.max(s_c, axis=-1, keepdims=True)),
                    jnp.maximum(jnp.max(s_n, axis=-1, keepdims=True), jnp.max(s_x, axis=-1, keepdims=True)))
    m = jnp.maximum(m, sink)
    p_p, p_c, p_n, p_x = jnp.exp(s_p - m), jnp.exp(s_c - m), jnp.exp(s_n - m), jnp.exp(s_x - m)
    den = (jnp.sum(p_p, axis=-1, keepdims=True) + jnp.sum(p_c, axis=-1, keepdims=True)
           + jnp.sum(p_n, axis=-1, keepdims=True) + jnp.sum(p_x, axis=-1, keepdims=True) + jnp.exp(sink - m))
    o = (_mm(p_p.astype(BF16), vp_ref[0]) + _mm(p_c.astype(BF16), vc_ref[0])
         + _mm(p_n.astype(BF16), vn_ref[0]) + _mm(p_x.astype(BF16), vx_ref[0])) / den
    for g in range(groups):
        o_ref[0, :, g * HEAD_DIM:(g + 1) * HEAD_DIM] = o[g * blk:(g + 1) * blk].astype(o_ref.dtype)


def window_attention(q, k, v, kx, vx, sink):
    B, L, _ = q.shape
    C = kx.shape[1]
    groups = A_HEADS // A_KV_HEADS
    nblk = L // A_BLOCK
    gw = groups * HEAD_DIM
    sink_t = jnp.broadcast_to(
        jnp.repeat(sink.astype(F32).reshape(A_KV_HEADS, groups), A_BLOCK, axis=1)[:, :, None],
        (A_KV_HEADS, groups * A_BLOCK, LANES))
    kv_spec = lambda f: pl.BlockSpec((1, A_BLOCK, HEAD_DIM), f)
    prev = lambda b, h, n: (b, jnp.maximum(n - 1, 0), h)
    cur = lambda b, h, n: (b, n, h)
    nxt = lambda b, h, n: (b, jnp.minimum(n + 1, nblk - 1), h)
    ctx_spec = pl.BlockSpec((1, C, HEAD_DIM), lambda b, h, n: (b, 0, h))
    return pl.pallas_call(
        functools.partial(_win_attn_kernel, nblk=nblk, groups=groups, scale=HEAD_DIM ** -0.5),
        out_shape=jax.ShapeDtypeStruct((B, L, A_Q_COLS), BF16),
        grid=(B, A_KV_HEADS, nblk),
        in_specs=[pl.BlockSpec((1, A_BLOCK, gw), cur),
                  kv_spec(prev), kv_spec(cur), kv_spec(nxt), kv_spec(prev), kv_spec(cur), kv_spec(nxt),
                  ctx_spec, ctx_spec,
                  pl.BlockSpec((1, groups * A_BLOCK, LANES), lambda b, h, n: (h, 0, 0))],
        out_specs=pl.BlockSpec((1, A_BLOCK, gw), cur),
        compiler_params=_params("parallel", "parallel", "arbitrary"),
        name="window_attention",
    )(q, k, k, k, v, v, v, kx, vx, sink_t)


def _nbr_attn_kernel(q_ref, kp_ref, kc_ref, kn_ref, vp_ref, vc_ref, vn_ref, kx_ref, vx_ref, bias_ref,
                     o_ref, kbuf, vbuf, *, grid_rows, scale):
    rb = pl.program_id(2)
    tok = NB_ROWS * GRID_W
    nk = NA_ROWS * GRID_W
    kbuf[0:tok] = kp_ref[0]
    kbuf[tok:2 * tok] = kc_ref[0]
    kbuf[2 * tok:3 * tok] = kn_ref[0]
    vbuf[0:tok] = vp_ref[0]
    vbuf[tok:2 * tok] = vc_ref[0]
    vbuf[2 * tok:3 * tok] = vn_ref[0]
    kx = kx_ref[0]
    vx = vx_ref[0]
    for rr in range(NB_ROWS):
        r = rb * NB_ROWS + rr
        r0 = jnp.clip(r - NA_ROWS // 2, 0, grid_rows - NA_ROWS)
        off = pl.multiple_of((r0 - (rb - 1) * NB_ROWS) * GRID_W, GRID_W)
        ks = kbuf[pl.ds(off, nk), :]
        vs = vbuf[pl.ds(off, nk), :]
        qr = q_ref[0, rr * GRID_W:(rr + 1) * GRID_W, :]
        s = _nt(qr, ks) * scale + bias_ref[0, r0 - r + NA_ROWS - 1]
        sx = _nt(qr, kx) * scale
        m = jnp.maximum(jnp.max(s, axis=-1, keepdims=True), jnp.max(sx, axis=-1, keepdims=True))
        p = jnp.exp(s - m)
        px = jnp.exp(sx - m)
        den = jnp.sum(p, axis=-1, keepdims=True) + jnp.sum(px, axis=-1, keepdims=True)
        o = (_mm(p.astype(BF16), vs) + _mm(px.astype(BF16), vx)) / den
        o_ref[0, rr * GRID_W:(rr + 1) * GRID_W, :] = o.astype(o_ref.dtype)


def _nbr_bias_table(rpb):
    col = jnp.arange(GRID_W)
    c0 = jnp.clip(col - NA_COLS // 2, 0, GRID_W - NA_COLS)
    col_ok = (col[None, :] >= c0[:, None]) & (col[None, :] < c0[:, None] + NA_COLS)
    dc = jnp.clip(col[None, :] - col[:, None] + (NA_COLS - 1), 0, 2 * NA_COLS - 2)
    dr = jnp.arange(NA_ROWS)[:, None] + jnp.arange(NA_ROWS)[None, :]
    t = rpb.astype(F32)[:, dr[:, None, :, None], dc[None, :, None, :]]
    t = jnp.where(col_ok[None, None, :, None, :], t, NEG_INF)
    return t.reshape(rpb.shape[0], NA_ROWS, GRID_W, NA_ROWS * GRID_W)


def neighbourhood_attention(q, k, v, kx, vx, rpb):
    B, L, _ = q.shape
    C = kx.shape[1]
    grid_rows = L // GRID_W
    tok = NB_ROWS * GRID_W
    assert grid_rows >= NA_ROWS and L % tok == 0 and NB_ROWS == NA_ROWS
    nrb = L // tok
    bias = _nbr_bias_table(rpb)
    spec = lambda f: pl.BlockSpec((1, tok, HEAD_DIM), f)
    prev = lambda b, h, n: (b, jnp.maximum(n - 1, 0), h)
    cur = lambda b, h, n: (b, n, h)
    nxt = lambda b, h, n: (b, jnp.minimum(n + 1, nrb - 1), h)
    ctx_spec = pl.BlockSpec((1, C, HEAD_DIM), lambda b, h, n: (b, 0, h))
    return pl.pallas_call(
        functools.partial(_nbr_attn_kernel, grid_rows=grid_rows, scale=HEAD_DIM ** -0.5),
        out_shape=jax.ShapeDtypeStruct((B, L, B_COLS), BF16),
        grid=(B, B_HEADS, nrb),
        in_specs=[spec(cur), spec(prev), spec(cur), spec(nxt), spec(prev), spec(cur), spec(nxt),
                  ctx_spec, ctx_spec,
                  pl.BlockSpec((1, NA_ROWS, GRID_W, NA_ROWS * GRID_W), lambda b, h, n: (h, 0, 0, 0))],
        out_specs=spec(cur),
        scratch_shapes=[pltpu.VMEM((3 * tok, HEAD_DIM), BF16), pltpu.VMEM((3 * tok, HEAD_DIM), BF16)],
        compiler_params=_params("parallel", "parallel", "arbitrary"),
        name="neighbourhood_attention",
    )(q, k, k, k, v, v, v, kx, vx, bias)


def _ctx_attn_kernel(q_ref, k_ref, v_ref, sink_ref, o_ref, *, scale):
    s = _nt(q_ref[0], k_ref[0]) * scale
    sink = sink_ref[0][:1, :1]
    m = jnp.maximum(jnp.max(s, axis=-1, keepdims=True), sink)
    p = jnp.exp(s - m)
    den = jnp.sum(p, axis=-1, keepdims=True) + jnp.exp(sink - m)
    o_ref[0] = (_mm(p.astype(BF16), v_ref[0]) / den).astype(o_ref.dtype)


def context_attention(q, k, v, sink):
    B, C, qc = q.shape
    hq = qc // HEAD_DIM
    grp = hq // (k.shape[2] // HEAD_DIM)
    sink_t = jnp.broadcast_to(sink.astype(F32)[:, None, None], (hq, 8, LANES))
    return pl.pallas_call(
        functools.partial(_ctx_attn_kernel, scale=HEAD_DIM ** -0.5),
        out_shape=jax.ShapeDtypeStruct((B, C, qc), BF16),
        grid=(B, hq),
        in_specs=[pl.BlockSpec((1, C, HEAD_DIM), lambda b, h: (b, 0, h)),
                  pl.BlockSpec((1, C, HEAD_DIM), lambda b, h: (b, 0, h // grp)),
                  pl.BlockSpec((1, C, HEAD_DIM), lambda b, h: (b, 0, h // grp)),
                  pl.BlockSpec((1, 8, LANES), lambda b, h: (h, 0, 0))],
        out_specs=pl.BlockSpec((1, C, HEAD_DIM), lambda b, h: (b, 0, h)),
        compiler_params=_params("parallel", "parallel"),
        name="context_attention",
    )(q, k, v, sink_t)


def _wkv_pairs(ins, sts, tri_cum, strict_bd, incl_bd, lane_lo, st_bd):
    C = ins[0][0].shape[0]
    P = range(len(ins))
    steps = int(math.log2(C))
    cat = jnp.concatenate
    cum = [jnp.dot(tri_cum, ins[p][1], preferred_element_type=F32, precision=lax.Precision.HIGHEST) for p in P]
    tot = [jnp.sum(ins[p][1], axis=0, keepdims=True) for p in P]
    g_inv = [jnp.exp(-cum[p]) for p in P]
    lhs = [cat([ins[p][4] * jnp.exp(cum[p] - ins[p][1]), ins[p][0] * jnp.exp(cum[p])], axis=0).astype(BF16)
           for p in P]
    b_t = [ins[p][5] * g_inv[p] for p in P]
    k_t = [ins[p][2] * g_inv[p] for p in P]
    rhs = [cat([jnp.where(lane_lo, b_t[p], 0.0), jnp.where(lane_lo, 0.0, b_t[p]),
                jnp.where(lane_lo, k_t[p], 0.0), jnp.where(lane_lo, 0.0, k_t[p])], axis=0).astype(BF16) for p in P]
    prod = [_nt(lhs[p], rhs[p]) for p in P]
    sh = [_nt(lhs[p], sts[p].astype(BF16)) for p in P]

    def block_diag(x, mask):
        return jnp.where(mask, cat([x, x], axis=0), 0.0).astype(BF16)

    npow = [block_diag(prod[p][0:C, 0:2 * C], strict_bd) for p in P]
    n_ak = [block_diag(prod[p][0:C, 2 * C:4 * C], strict_bd) for p in P]
    n_r = [cat([block_diag(prod[p][C:2 * C, 0:2 * C], incl_bd),
                block_diag(prod[p][C:2 * C, 2 * C:4 * C], incl_bd)], axis=1) for p in P]
    v2 = [cat([ins[p][3], ins[p][3]], axis=0).astype(BF16) for p in P]
    x = [cat([sh[p][0:C], sh[p][0:C]], axis=0) + _mm(n_ak[p], v2[p]) for p in P]
    for i in range(steps):
        x = [x[p] + _mm(npow[p], x[p].astype(BF16)) for p in P]
        if i + 1 < steps:
            npow = [_mm(npow[p], npow[p]).astype(BF16) for p in P]
    ys = [cat([sh[p][C:2 * C], sh[p][C:2 * C]], axis=0)
          + _mm(n_r[p], cat([x[p].astype(BF16), v2[p]], axis=0)) for p in P]
    y = [jnp.where(lane_lo, ys[p][0:C], ys[p][C:2 * C]) for p in P]
    u = [jnp.where(lane_lo, x[p][0:C], x[p][C:2 * C]) for p in P]
    uv_t = [cat([u[p], ins[p][3]], axis=0).T.astype(BF16) for p in P]
    g_tail = [jnp.exp(tot[p] - cum[p]) for p in P]
    bk = [cat([ins[p][5] * g_tail[p], ins[p][2] * g_tail[p]], axis=0).astype(BF16) for p in P]
    st_new = [sts[p] * jnp.exp(tot[p]) + jnp.where(st_bd, _mm(uv_t[p], bk[p]), 0.0) for p in P]
    return y, st_new


def _wkv_kernel(r_ref, lw_ref, k_ref, v_ref, a_ref, b_ref, y_ref, st_ref, *, reverse, pairs):
    C = WKV_CHUNK

    @pl.when(pl.program_id(2) == 0)
    def _():
        st_ref[...] = jnp.zeros_like(st_ref)

    ti = lax.broadcasted_iota(jnp.int32, (C, C), 0)
    si = lax.broadcasted_iota(jnp.int32, (C, C), 1)
    tri_cum = ((si >= ti) if reverse else (si <= ti)).astype(F32)
    rho = lax.broadcasted_iota(jnp.int32, (2 * C, 2 * C), 0)
    sig = lax.broadcasted_iota(jnp.int32, (2 * C, 2 * C), 1)
    same = (rho // C) == (sig // C)
    tt, ss = rho % C, sig % C
    strict_bd = same & ((ss > tt) if reverse else (ss < tt))
    incl_bd = same & ((ss >= tt) if reverse else (ss <= tt))
    lane_lo = lax.broadcasted_iota(jnp.int32, (C, LANES), 1) < RW_HEAD
    st_bd = (lax.broadcasted_iota(jnp.int32, (LANES, LANES), 0) // RW_HEAD) == (
        lax.broadcasted_iota(jnp.int32, (LANES, LANES), 1) // RW_HEAD)
    sls = [slice(p * LANES, (p + 1) * LANES) for p in range(pairs)]
    ins = [tuple(ref[0, :, sl] for ref in (r_ref, lw_ref, k_ref, v_ref, a_ref, b_ref)) for sl in sls]
    y, st_new = _wkv_pairs(ins, [st_ref[p] for p in range(pairs)], tri_cum, strict_bd, incl_bd, lane_lo, st_bd)
    for p, sl in enumerate(sls):
        y_ref[0, :, sl] = y[p]
        st_ref[p] = st_new[p]


def wkv_scan(r, lw, k, v, a, b, *, start, length, reverse):
    B, T, D = r.shape
    C = WKV_CHUNK
    assert start % C == 0 and length % C == 0 and 2 * C == LANES
    nchunk = length // C
    c0 = start // C
    pairs = min(WKV_PAIRS, D // LANES)
    width = pairs * LANES
    assert D % width == 0
    if reverse:
        imap = lambda bi, di, ci: (bi, c0 + nchunk - 1 - ci, di)
    else:
        imap = lambda bi, di, ci: (bi, c0 + ci, di)
    spec = pl.BlockSpec((1, C, width), imap)
    return pl.pallas_call(
        functools.partial(_wkv_kernel, reverse=reverse, pairs=pairs),
        out_shape=jax.ShapeDtypeStruct((B, T, D), F32),
        grid=(B, D // width, nchunk),
        in_specs=[spec] * 6,
        out_specs=spec,
        scratch_shapes=[pltpu.VMEM((pairs, LANES, LANES), F32)],
        compiler_params=_params("parallel", "parallel", "arbitrary"),
        name="wkv_scan_bwd" if reverse else "wkv_scan_fwd",
    )(r, lw, k, v, a, b)


def _moe_ffn_kernel(be_ref, nused_ref, x_ref, w1_ref, w3_ref, w2_ref, o_ref, w1b, w3b, w2b):
    i = pl.program_id(0)
    used = i < nused_ref[0]

    @pl.when(used & ((i == 0) | (be_ref[i] != be_ref[jnp.maximum(i - 1, 0)])))
    def _():
        w1b[...] = w1_ref[0].astype(BF16)
        w3b[...] = w3_ref[0].astype(BF16)
        w2b[...] = w2_ref[0].astype(BF16)

    @pl.when(used)
    def _():
        x = x_ref[...]
        h1 = _mm(x, w1b[...])
        h3 = _mm(x, w3b[...])
        hh = (h1 * jax.nn.sigmoid(h1)) * h3
        o_ref[...] = _mm(hh.astype(BF16), w2b[...]).astype(o_ref.dtype)

    @pl.when(i >= nused_ref[0])
    def _():
        o_ref[...] = jnp.zeros_like(o_ref)


def moe_ffn(x_rows, blk_expert, n_used, w1, w3, w2):
    slots, D = x_rows.shape
    F = w1.shape[2]
    nblk = slots // MOE_ROWS
    grid_spec = pltpu.PrefetchScalarGridSpec(
        num_scalar_prefetch=2,
        grid=(nblk,),
        in_specs=[pl.BlockSpec((MOE_ROWS, D), lambda i, be, nu: (i, 0)),
                  pl.BlockSpec((1, D, F), lambda i, be, nu: (be[i], 0, 0)),
                  pl.BlockSpec((1, D, F), lambda i, be, nu: (be[i], 0, 0)),
                  pl.BlockSpec((1, F, D), lambda i, be, nu: (be[i], 0, 0))],
        out_specs=pl.BlockSpec((MOE_ROWS, D), lambda i, be, nu: (i, 0)),
        scratch_shapes=[pltpu.VMEM((D, F), BF16), pltpu.VMEM((D, F), BF16), pltpu.VMEM((F, D), BF16)],
    )
    return pl.pallas_call(
        _moe_ffn_kernel,
        out_shape=jax.ShapeDtypeStruct((slots, D), BF16),
        grid_spec=grid_spec,
        compiler_params=_params("arbitrary"),
        name="moe_ffn",
    )(blk_expert, n_used, x_rows, w1.astype(F32), w3.astype(F32), w2.astype(F32))


def hierarchical_moe(h, w_grp, w_exp, w1, w3, w2):
    n, D = h.shape
    w_route = jnp.concatenate([w_grp, w_exp], axis=1).astype(F32)
    w_route = jnp.pad(w_route, ((0, 0), (0, LANES - w_route.shape[1])))
    route = linear(h, w_route, precise=True)
    p_grp = jax.nn.softmax(route[:, :N_GROUPS], axis=-1)
    g_sel = jnp.argmax(p_grp, axis=-1)
    p_sel = jnp.take_along_axis(p_grp, g_sel[:, None], axis=-1)
    logits = route[:, N_GROUPS:N_GROUPS + N_EXPERTS].reshape(n, N_GROUPS, EXPERTS_PER_GROUP)
    logits_g = jnp.take_along_axis(logits, g_sel[:, None, None], axis=1)[:, 0]
    top_val, top_idx = lax.top_k(logits_g, TOP_K)
    gate = p_sel * jax.nn.softmax(top_val, axis=-1)
    expert = (g_sel[:, None] * EXPERTS_PER_GROUP + top_idx).astype(jnp.int32)
    flat_e = expert.reshape(-1)
    onehot = (flat_e[:, None] == jnp.arange(N_EXPERTS, dtype=jnp.int32)[None, :]).astype(jnp.int32)
    csum = jnp.cumsum(onehot, axis=0)
    rank = jnp.take_along_axis(csum, flat_e[:, None], axis=1)[:, 0] - 1
    counts = csum[-1]
    padded = (counts + MOE_ROWS - 1) // MOE_ROWS * MOE_ROWS
    pad_end = jnp.cumsum(padded)
    dest = (pad_end - padded)[flat_e] + rank
    nblk = (n * TOP_K + N_EXPERTS * (MOE_ROWS - 1) + MOE_ROWS - 1) // MOE_ROWS
    slots = nblk * MOE_ROWS
    flat_tok = jnp.repeat(jnp.arange(n, dtype=jnp.int32), TOP_K)
    slot_tok = jnp.full((slots,), n, jnp.int32).at[dest].set(flat_tok)
    blk_expert = jnp.minimum(jnp.searchsorted(pad_end, jnp.arange(nblk, dtype=jnp.int32) * MOE_ROWS, side='right'),
                             N_EXPERTS - 1).astype(jnp.int32)
    n_used = (pad_end[-1] // MOE_ROWS).astype(jnp.int32).reshape(1)
    x_rows = jnp.concatenate([h.astype(BF16), jnp.zeros((1, D), BF16)], axis=0)[slot_tok]
    y = moe_ffn(x_rows, blk_expert, n_used, w1, w3, w2)
    d2 = dest.reshape(n, TOP_K)
    return y[d2[:, 0]].astype(F32) * gate[:, 0:1] + y[d2[:, 1]].astype(F32) * gate[:, 1:2]


def _head_rmsnorm(x, g):
    y = x * lax.rsqrt(jnp.mean(x * x, axis=-1, keepdims=True) + NORM_EPS)
    return y * g.astype(F32)


def _rope_tables(L):
    t = jnp.arange(L)
    row_pos, col_pos = t // GRID_W, t % GRID_W
    quarter = HEAD_DIM // 4
    inv_freq = ROPE_THETA ** (-jnp.arange(quarter, dtype=F32) / quarter)
    ar = row_pos.astype(F32)[:, None] * inv_freq[None, :]
    ac = col_pos.astype(F32)[:, None] * inv_freq[None, :]
    return jnp.cos(ar), jnp.sin(ar), jnp.cos(ac), jnp.sin(ac)


def _axial_rope(x, tables):
    cr, sr, cc, sc = (t[None, :, None, :] for t in tables)
    q = HEAD_DIM // 4
    x1, x2, x3, x4 = x[..., :q], x[..., q:2 * q], x[..., 2 * q:3 * q], x[..., 3 * q:]
    return jnp.concatenate([x1 * cr - x2 * sr, x1 * sr + x2 * cr, x3 * cc - x4 * sc, x3 * sc + x4 * cc], axis=-1)


def _split_qkv(p):
    B, L, _ = p.shape
    cuts = (A_Q_COLS, A_Q_COLS + A_KV_COLS, A_Q_COLS + 2 * A_KV_COLS, A_Q_COLS + 2 * A_KV_COLS + B_COLS,
            A_Q_COLS + 2 * A_KV_COLS + 2 * B_COLS)
    return tuple(t.reshape(B, L, -1, HEAD_DIM) for t in jnp.split(p, cuts, axis=-1))


def attention_layer(x, xc, norm_g, sc, sh, csc, csh, w_in, w_out, a_q_gain, a_k_gain, a_sink, b_q_gain, b_k_gain,
                    b_rpb, gt, cgt):
    B, L, D = x.shape
    C = xc.shape[1]
    flat = lambda t: t.reshape(t.shape[0], t.shape[1], -1)
    qkv = linear(x.reshape(B * L, D), w_in, norm=(norm_g, sc, sh)).reshape(B, L, -1)
    qkv_c = linear(xc.reshape(B * C, D), w_in, norm=(norm_g, csc, csh)).reshape(B, C, -1)
    qa, ka, va, qb, kb, vb = _split_qkv(qkv)
    qa_c, ka_c, va_c, qb_c, kb_c, vb_c = _split_qkv(qkv_c)
    tables = _rope_tables(L)
    qa = flat(_axial_rope(_head_rmsnorm(qa, a_q_gain), tables).astype(BF16))
    ka = flat(_axial_rope(_head_rmsnorm(ka, a_k_gain), tables).astype(BF16))
    qb = flat(_head_rmsnorm(qb, b_q_gain).astype(BF16))
    kb = flat(_head_rmsnorm(kb, b_k_gain).astype(BF16))
    va, vb = flat(va.astype(BF16)), flat(vb.astype(BF16))
    ka_c = flat(_head_rmsnorm(ka_c, a_k_gain).astype(BF16))
    kb_c = flat(_head_rmsnorm(kb_c, b_k_gain).astype(BF16))
    va_c, vb_c = flat(va_c.astype(BF16)), flat(vb_c.astype(BF16))
    ya = window_attention(qa, ka, va, ka_c, va_c, a_sink)
    yb = neighbourhood_attention(qb, kb, vb, kb_c, vb_c, b_rpb)
    y = jnp.concatenate([ya, yb], axis=-1).reshape(B * L, -1)
    x_new = linear(y, w_out, gated=(x.reshape(B * L, D), gt)).reshape(B, L, D)
    qa_c = flat(_head_rmsnorm(qa_c, a_q_gain).astype(BF16))
    qb_c = flat(_head_rmsnorm(qb_c, b_q_gain).astype(BF16))
    yac = context_attention(qa_c, ka_c, va_c, a_sink)
    ybc = context_attention(qb_c, kb_c, vb_c, jnp.full((B_HEADS,), NEG_INF, F32))
    yc = jnp.concatenate([yac, ybc], axis=-1).reshape(B * C, -1)
    xc_new = linear(yc, w_out, gated=(xc.reshape(B * C, D), cgt)).reshape(B, C, D)
    return x_new, xc_new


def rwkv_layer(x, xc, norm_g, sc, sh, csc, csh, gt, mu, wr, wk, wv, wo, w0, w1, w2, a0, a1, a2, g1, g2, k_k, k_a,
               r_k, lnx_w, lnx_b):
    B, L, D = x.shape
    C = xc.shape[1]
    H = D // RW_HEAD
    T = L + 2 * C

    def modulated(t, s, b):
        tf = t.astype(F32)
        y = tf * lax.rsqrt(jnp.mean(tf * tf, axis=-1, keepdims=True) + NORM_EPS) * norm_g.astype(F32)
        return y * (1.0 + s[:, None, :]) + b[:, None, :]

    def shift(t):
        tp = jnp.pad(t, ((0, 0), (1, 1), (0, 0)))
        return 0.5 * (tp[:, :-2] + tp[:, 2:]) - t

    h = modulated(x, sc, sh)
    hc = modulated(xc, jnp.broadcast_to(csc, (B, D)), jnp.broadcast_to(csh, (B, D)))
    hcat = jnp.concatenate([hc, h, hc], axis=1).reshape(B * T, D)
    xx = jnp.concatenate([shift(hc), shift(h), shift(hc)], axis=1).reshape(B * T, D)
    proj = lambda m, w: linear(hcat, w, mix=(xx, mu[m]))
    r = proj(0, wr)
    k = proj(2, wk)
    v = proj(3, wv)
    g = linear(jax.nn.sigmoid(proj(5, g1)), g2)
    w_lora = jnp.tanh(proj(1, jnp.concatenate([w1[0], w1[1]], axis=1)))
    a_lora = proj(4, jnp.concatenate([a1[0], a1[1]], axis=1))
    heads = lambda t: t.reshape(B * T, H, RW_HEAD)
    kkf = heads(k * k_k)
    kk = (kkf * lax.rsqrt(jnp.sum(kkf * kkf, axis=-1, keepdims=True) + 1e-12)).reshape(B * T, D)
    nl = w1.shape[2]
    na = a1.shape[2]
    ys = []
    k_sum = 0.0
    for d, reverse in enumerate((False, True)):
        w_raw = w0[d] + linear(w_lora[:, d * nl:(d + 1) * nl], w2[d])
        lw = -jnp.exp(-jax.nn.softplus(-w_raw) - 0.5)
        ag = jax.nn.sigmoid(a0[d] + linear(a_lora[:, d * na:(d + 1) * na], a2[d]))
        kd = k * (1.0 + (ag - 1.0) * k_a)
        k_sum = k_sum + kd
        to3 = lambda t: t.reshape(B, T, D)
        y_d = wkv_scan(to3(r), to3(lw), to3(kd), to3(v), to3(-kk), to3(kk * ag),
                       start=C if reverse else 0, length=C + L, reverse=reverse)
        ys.append(y_d[:, C:C + L])
    lat = lambda t: t.reshape(B, T, D)[:, C:C + L]
    y = (ys[0] + ys[1]).reshape(B, L, H, RW_HEAD)
    mean = jnp.mean(y, axis=-1, keepdims=True)
    var = jnp.mean(jnp.square(y - mean), axis=-1, keepdims=True)
    yn = ((y - mean) * lax.rsqrt(var + RW_LNX_EPS)).reshape(B, L, D) * lnx_w + lnx_b
    rl = lat(r).reshape(B, L, H, RW_HEAD)
    bonus = jnp.sum(rl * lat(k_sum).reshape(B, L, H, RW_HEAD) * r_k, axis=-1, keepdims=True) * lat(v).reshape(
        B, L, H, RW_HEAD)
    z = (yn + bonus.reshape(B, L, D)) * lat(g)
    return linear(z.reshape(B * L, D), wo, gated=(x.reshape(B * L, D), gt)).reshape(B, L, D)


def _ffn_input(t, g, s, b):
    tf = t.astype(F32)
    y = tf * lax.rsqrt(jnp.mean(tf * tf, axis=-1, keepdims=True) + NORM_EPS) * g.astype(F32)
    return y * (1.0 + s[:, None, :]) + b[:, None, :]


def kernel(x, c, ctx, c_ctx, ada_w, ada_b, norm_mix_g, norm_ffn_g, attn_w_in, attn_w_out, a_q_gain, a_k_gain, a_sink, b_q_gain, b_k_gain, b_rpb, rw_mu, rw_wr, rw_wk, rw_wv, rw_wo, rw_w0, rw_w1, rw_w2, rw_a0, rw_a1, rw_a2, rw_g1, rw_g2, rw_k_k, rw_k_a, rw_r_k, rw_lnx_w, rw_lnx_b, moe_w_grp, moe_w_exp, moe_w1, moe_w3, moe_w2):
    B, L, D = x.shape
    C = ctx.shape[1]
    depth = ada_w.shape[0]
    assert depth == 2, "layer 0 is the attention mixer, layer 1 (last) the RWKV-7 mixer"
    pad = (-(B + 1)) % 8
    cond = jnp.concatenate([jax.nn.silu(c), jax.nn.silu(c_ctx)[None, :], jnp.zeros((pad, D), F32)], axis=0)
    xc = ctx
    for i in range(depth):
        last = i == depth - 1
        j = i // 2
        mod = linear(cond, ada_w[i], precise=True) + ada_b[i]
        sh1, sc1, gt1, sh2, sc2, gt2 = jnp.split(mod[:B], 6, axis=-1)
        csh1, csc1, cgt1, csh2, csc2, cgt2 = jnp.split(mod[B:B + 1], 6, axis=-1)
        if i % 2 == 0:
            x, xc_mix = attention_layer(x, xc, norm_mix_g[i], sc1, sh1, csc1, csh1, attn_w_in[j], attn_w_out[j],
                                        a_q_gain[j], a_k_gain[j], a_sink[j], b_q_gain[j], b_k_gain[j], b_rpb[j],
                                        gt1, cgt1)
        else:
            x = rwkv_layer(x, xc, norm_mix_g[i], sc1, sh1, csc1, csh1, gt1, rw_mu[j], rw_wr[j], rw_wk[j], rw_wv[j],
                           rw_wo[j], rw_w0[j], rw_w1[j], rw_w2[j], rw_a0[j], rw_a1[j], rw_a2[j], rw_g1[j], rw_g2[j],
                           rw_k_k[j], rw_k_a[j], rw_r_k[j], rw_lnx_w[j], rw_lnx_b[j])
            xc_mix = None
        h = _ffn_input(x, norm_ffn_g[i], sc2, sh2).reshape(B * L, D)
        experts = (moe_w_grp[i], moe_w_exp[i], moe_w1[i], moe_w3[i], moe_w2[i])
        if last:
            f = hierarchical_moe(h, *experts)
            x = x + gt2[:, None, :] * f.reshape(B, L, D)
        else:
            xc = xc_mix
            hc = _ffn_input(xc, norm_ffn_g[i], jnp.broadcast_to(csc2, (B, D)), jnp.broadcast_to(csh2, (B, D)))
            f = hierarchical_moe(jnp.concatenate([h, hc.reshape(B * C, D)], axis=0), *experts)
            x = x + gt2[:, None, :] * f[:B * L].reshape(B, L, D)
            xc = xc + cgt2[:, None, :] * f[B * L:].reshape(B, C, D)
    return x
```

```python
import functools
import math

import numpy as np
import jax
import jax.numpy as jnp
from jax import lax
from jax.experimental import pallas as pl
from jax.experimental.pallas import tpu as pltpu

GRID_W = 64
HEAD_DIM = 128
A_HEADS = 8
A_KV_HEADS = 2
A_BLOCK = 128
B_HEADS = 8
NA_ROWS = 8
NA_COLS = 16
ROPE_THETA = 10000.0
A_Q_COLS = A_HEADS * HEAD_DIM
A_KV_COLS = A_KV_HEADS * HEAD_DIM
B_COLS = B_HEADS * HEAD_DIM
QKV_COLS = A_Q_COLS + 2 * A_KV_COLS + 3 * B_COLS
MIX_WIDTH = A_Q_COLS + B_COLS
RW_HEAD = 64
RW_LNX_EPS = 64e-5
N_GROUPS = 4
EXPERTS_PER_GROUP = 8
N_EXPERTS = N_GROUPS * EXPERTS_PER_GROUP
TOP_K = 2
NORM_EPS = 1e-6
NEG_INF = -1e30

LANES = 128
VMEM_LIMIT_BYTES = 48 * 1024 * 1024
MOE_ROWS = 256
NB_ROWS = 8
WKV_CHUNK = 64
WKV_PAIRS = 8
QKV_TN = 4 * HEAD_DIM

QA_BLK = 0
KA_BLK = A_Q_COLS // HEAD_DIM
VA_BLK = KA_BLK + A_KV_HEADS
QB_BLK = VA_BLK + A_KV_HEADS
KB_BLK = QB_BLK + B_HEADS
VB_BLK = KB_BLK + B_HEADS

BF16 = jnp.bfloat16
F32 = jnp.float32


def _params(*sem):
    return pltpu.CompilerParams(dimension_semantics=sem, vmem_limit_bytes=VMEM_LIMIT_BYTES)


def _nt(a, b):
    return lax.dot_general(a, b, (((1,), (1,)), ((), ())), preferred_element_type=F32)


def _mm(a, b):
    return jnp.dot(a, b, preferred_element_type=F32)


def _rms_modulate(x, g, sc, sh):
    y = x * lax.rsqrt(jnp.mean(x * x, axis=-1, keepdims=True) + NORM_EPS)
    return (y * g) * (1.0 + sc) + sh


def _linear_kernel(*refs, mode, gated, precise, emit_h):
    it = iter(refs)
    x_ref = next(it)
    if mode == "norm":
        g_ref, sc_ref, sh_ref = next(it), next(it), next(it)
    elif mode == "mix":
        xx_ref, mu_ref = next(it), next(it)
    w_ref = next(it)
    if gated:
        res_ref, gate_ref = next(it), next(it)
    o_ref = next(it)
    if emit_h:
        hb_ref = next(it)
    h_ref = next(it)

    @pl.when(pl.program_id(1) == 0)
    def _():
        x = x_ref[...].astype(F32)
        if mode == "norm":
            x = _rms_modulate(x, g_ref[...], sc_ref[0], sh_ref[0])
        elif mode == "mix":
            x = x + xx_ref[...].astype(F32) * mu_ref[...]
        elif mode == "sigmoid":
            x = jax.nn.sigmoid(x)
        h_ref[...] = x.astype(h_ref.dtype)
        if emit_h:
            hb_ref[...] = x.astype(hb_ref.dtype)

    if precise:
        acc = jnp.dot(h_ref[...], w_ref[...], preferred_element_type=F32,
                      precision=lax.Precision.HIGHEST)
    else:
        acc = _mm(h_ref[...], w_ref[...])
    if gated:
        acc = res_ref[...] + gate_ref[0] * acc
    o_ref[...] = acc.astype(o_ref.dtype)


def linear(x, w, *, norm=None, mix=None, act=None, gated=None, out_dtype=F32, precise=False, emit_h=False,
           tm=512, tn=512):
    M, K = x.shape
    N = w.shape[1]
    tm = min(tm, M)
    tn = tn if N % tn == 0 else N
    assert M % tm == 0 and N % tn == 0
    mode = "norm" if norm is not None else ("mix" if mix is not None else (act or "plain"))
    wdt = F32 if precise else BF16
    args = [x]
    in_specs = [pl.BlockSpec((tm, K), lambda i, j: (i, 0))]
    if mode == "norm":
        g, sc, sh = norm
        nb = sc.shape[0]
        rpb = M // nb
        assert rpb % tm == 0
        args += [g.reshape(1, K).astype(F32), sc.reshape(nb, 1, K).astype(F32), sh.reshape(nb, 1, K).astype(F32)]
        in_specs += [pl.BlockSpec((1, K), lambda i, j: (0, 0)),
                     pl.BlockSpec((1, 1, K), lambda i, j: ((i * tm) // rpb, 0, 0)),
                     pl.BlockSpec((1, 1, K), lambda i, j: ((i * tm) // rpb, 0, 0))]
    elif mode == "mix":
        xx, mu = mix
        args += [xx, mu.reshape(1, K).astype(F32)]
        in_specs += [pl.BlockSpec((tm, K), lambda i, j: (i, 0)), pl.BlockSpec((1, K), lambda i, j: (0, 0))]
    args.append(w.astype(wdt))
    in_specs.append(pl.BlockSpec((K, tn), lambda i, j: (0, j)))
    if gated is not None:
        res, gate = gated
        nbg = gate.shape[0]
        rpg = M // nbg
        assert rpg % tm == 0
        args += [res, gate.reshape(nbg, 1, N).astype(F32)]
        in_specs += [pl.BlockSpec((tm, tn), lambda i, j: (i, j)),
                     pl.BlockSpec((1, 1, tn), lambda i, j: ((i * tm) // rpg, 0, j))]
    out_shape = jax.ShapeDtypeStruct((M, N), out_dtype)
    out_specs = pl.BlockSpec((tm, tn), lambda i, j: (i, j))
    if emit_h:
        out_shape = (out_shape, jax.ShapeDtypeStruct((M, K), BF16))
        out_specs = (out_specs, pl.BlockSpec((tm, K), lambda i, j: (i, 0)))
    return pl.pallas_call(
        functools.partial(_linear_kernel, mode=mode, gated=gated is not None, precise=precise, emit_h=emit_h),
        out_shape=out_shape,
        grid=(M // tm, N // tn),
        in_specs=in_specs,
        out_specs=out_specs,
        scratch_shapes=[pltpu.VMEM((tm, K), wdt)],
        compiler_params=_params("parallel", "arbitrary"),
        name="linear_" + mode,
    )(*args)


def _qkv_kernel(x_ref, g_ref, sc_ref, sh_ref, w_ref, gain_ref, cos_ref, sin_ref, o_ref, h_ref, *, rope):
    j = pl.program_id(1)

    @pl.when(j == 0)
    def _():
        h_ref[...] = _rms_modulate(x_ref[...].astype(F32), g_ref[...], sc_ref[0], sh_ref[0]).astype(h_ref.dtype)

    acc = _mm(h_ref[...], w_ref[...])
    tm = acc.shape[0]

    def emit(kinds):
        for hh, kind in enumerate(kinds):
            sl = slice(hh * HEAD_DIM, (hh + 1) * HEAD_DIM)
            y = acc[:, sl]
            if kind != "plain":
                y = y * lax.rsqrt(jnp.mean(y * y, axis=-1, keepdims=True) + NORM_EPS) * gain_ref[:, sl]
                if kind == "rope" and rope:
                    lane = lax.broadcasted_iota(jnp.int32, (tm, HEAD_DIM), 1)
                    first = (lane % (HEAD_DIM // 2)) < HEAD_DIM // 4
                    partner = jnp.where(first, pltpu.roll(y, HEAD_DIM - HEAD_DIM // 4, 1),
                                        pltpu.roll(y, HEAD_DIM // 4, 1))
                    y = y * cos_ref[...] + partner * sin_ref[...]
            o_ref[:, sl] = y.astype(o_ref.dtype)

    per = QKV_TN // HEAD_DIM
    kinds = (["rope"] * A_HEADS + ["rope"] * A_KV_HEADS + ["plain"] * A_KV_HEADS + ["norm"] * B_HEADS
             + ["norm"] * B_HEADS + ["plain"] * B_HEADS)
    tiles = [tuple(kinds[t * per:(t + 1) * per]) for t in range(len(kinds) // per)]
    for kind_set in sorted(set(tiles)):
        cond = None
        for t, ks in enumerate(tiles):
            if ks == kind_set:
                cond = (j == t) if cond is None else (cond | (j == t))
        pl.when(cond)(functools.partial(emit, kind_set))


def qkv_projection(x, w_in, norm_g, sc, sh, gains, rope_len, tm=512):
    M, K = x.shape
    N = w_in.shape[1]
    assert N == QKV_COLS and N % QKV_TN == 0 and M % tm == 0
    nb = sc.shape[0]
    rpb = M // nb
    assert rpb % tm == 0
    a_q, a_k, b_q, b_k = (t.astype(F32) for t in gains)
    ones = jnp.ones((HEAD_DIM,), F32)
    gain = jnp.concatenate([jnp.tile(a_q, A_HEADS), jnp.tile(a_k, A_KV_HEADS), jnp.tile(ones, A_KV_HEADS),
                            jnp.tile(b_q, B_HEADS), jnp.tile(b_k, B_HEADS), jnp.tile(ones, B_HEADS)]).reshape(1, N)
    rope = rope_len is not None
    if rope:
        assert rope_len % tm == 0
        t = jnp.arange(rope_len)
        quarter = HEAD_DIM // 4
        inv_freq = ROPE_THETA ** (-jnp.arange(quarter, dtype=F32) / quarter)
        ar = (t // GRID_W).astype(F32)[:, None] * inv_freq[None, :]
        ac = (t % GRID_W).astype(F32)[:, None] * inv_freq[None, :]
        cos_t = jnp.concatenate([jnp.cos(ar), jnp.cos(ar), jnp.cos(ac), jnp.cos(ac)], axis=1)
        sin_t = jnp.concatenate([-jnp.sin(ar), jnp.sin(ar), -jnp.sin(ac), jnp.sin(ac)], axis=1)
        nt = rope_len // tm
    else:
        cos_t = jnp.ones((tm, HEAD_DIM), F32)
        sin_t = jnp.zeros((tm, HEAD_DIM), F32)
        nt = 1
    tab_spec = pl.BlockSpec((tm, HEAD_DIM), lambda i, j: (i % nt, 0))
    return pl.pallas_call(
        functools.partial(_qkv_kernel, rope=rope),
        out_shape=jax.ShapeDtypeStruct((M, N), BF16),
        grid=(M // tm, N // QKV_TN),
        in_specs=[pl.BlockSpec((tm, K), lambda i, j: (i, 0)),
                  pl.BlockSpec((1, K), lambda i, j: (0, 0)),
                  pl.BlockSpec((1, 1, K), lambda i, j: ((i * tm) // rpb, 0, 0)),
                  pl.BlockSpec((1, 1, K), lambda i, j: ((i * tm) // rpb, 0, 0)),
                  pl.BlockSpec((K, QKV_TN), lambda i, j: (0, j)),
                  pl.BlockSpec((1, QKV_TN), lambda i, j: (0, j)),
                  tab_spec, tab_spec],
        out_specs=pl.BlockSpec((tm, QKV_TN), lambda i, j: (i, j)),
        scratch_shapes=[pltpu.VMEM((tm, K), BF16)],
        compiler_params=_params("parallel", "arbitrary"),
        name="qkv_projection",
    )(x, norm_g.reshape(1, K).astype(F32), sc.reshape(nb, 1, K).astype(F32), sh.reshape(nb, 1, K).astype(F32),
      w_in.astype(BF16), gain, cos_t, sin_t)


def _win_attn_kernel(q_ref, kp_ref, kc_ref, kn_ref, vp_ref, vc_ref, vn_ref, kx_ref, vx_ref, sink_ref,
                     o_ref, *, nblk, groups, scale):
    n = pl.program_id(2)
    blk = A_BLOCK
    q = q_ref[0]
    qs = jnp.concatenate([q[:, g * HEAD_DIM:(g + 1) * HEAD_DIM] for g in range(groups)], axis=0)
    rows = groups * blk
    s_p = _nt(qs, kp_ref[0]) * scale
    s_c = _nt(qs, kc_ref[0]) * scale
    s_n = _nt(qs, kn_ref[0]) * scale
    s_x = _nt(qs, kx_ref[0]) * scale
    qi = lax.broadcasted_iota(jnp.int32, (rows, blk), 0) % blk
    kj = lax.broadcasted_iota(jnp.int32, (rows, blk), 1)
    s_p = jnp.where((kj >= qi) & (n > 0), s_p, NEG_INF)
    s_n = jnp.where((kj <= qi) & (n < nblk - 1), s_n, NEG_INF)
    sink = sink_ref[0][:, :1]
    m = jnp.maximum(jnp.maximum(jnp.max(s_p, axis=-1, keepdims=True), jnp.max(s_c, axis=-1, keepdims=True)),
                    jnp.maximum(jnp.max(s_n, axis=-1, keepdims=True), jnp.max(s_x, axis=-1, keepdims=True)))
    m = jnp.maximum(m, sink)
    p_p, p_c, p_n, p_x = jnp.exp(s_p - m), jnp.exp(s_c - m), jnp.exp(s_n - m), jnp.exp(s_x - m)
    den = (jnp.sum(p_p, axis=-1, keepdims=True) + jnp.sum(p_c, axis=-1, keepdims=True)
           + jnp.sum(p_n, axis=-1, keepdims=True) + jnp.sum(p_x, axis=-1, keepdims=True) + jnp.exp(sink - m))
    o = (_mm(p_p.astype(BF16), vp_ref[0]) + _mm(p_c.astype(BF16), vc_ref[0])
         + _mm(p_n.astype(BF16), vn_ref[0]) + _mm(p_x.astype(BF16), vx_ref[0])) / den
    for g in range(groups):
        o_ref[0, :, g * HEAD_DIM:(g + 1) * HEAD_DIM] = o[g * blk:(g + 1) * blk].astype(o_ref.dtype)


def window_attention(qkv, qkv_c, sink):
    B, L, _ = qkv.shape
    C = qkv_c.shape[1]
    groups = A_HEADS // A_KV_HEADS
    nblk = L // A_BLOCK
    gw = groups * HEAD_DIM
    sink_t = jnp.broadcast_to(
        jnp.repeat(sink.astype(F32).reshape(A_KV_HEADS, groups), A_BLOCK, axis=1)[:, :, None],
        (A_KV_HEADS, groups * A_BLOCK, LANES))
    blkspec = lambda f: pl.BlockSpec((1, A_BLOCK, HEAD_DIM), f)
    prev = lambda c0: (lambda b, h, n: (b, jnp.maximum(n - 1, 0), c0 + h))
    cur = lambda c0: (lambda b, h, n: (b, n, c0 + h))
    nxt = lambda c0: (lambda b, h, n: (b, jnp.minimum(n + 1, nblk - 1), c0 + h))
    ctx = lambda c0: pl.BlockSpec((1, C, HEAD_DIM), lambda b, h, n: (b, 0, c0 + h))
    return pl.pallas_call(
        functools.partial(_win_attn_kernel, nblk=nblk, groups=groups, scale=HEAD_DIM ** -0.5),
        out_shape=jax.ShapeDtypeStruct((B, L, MIX_WIDTH), BF16),
        grid=(B, A_KV_HEADS, nblk),
        in_specs=[pl.BlockSpec((1, A_BLOCK, gw), lambda b, h, n: (b, n, h)),
                  blkspec(prev(KA_BLK)), blkspec(cur(KA_BLK)), blkspec(nxt(KA_BLK)),
                  blkspec(prev(VA_BLK)), blkspec(cur(VA_BLK)), blkspec(nxt(VA_BLK)),
                  ctx(KA_BLK), ctx(VA_BLK),
                  pl.BlockSpec((1, groups * A_BLOCK, LANES), lambda b, h, n: (h, 0, 0))],
        out_specs=pl.BlockSpec((1, A_BLOCK, gw), lambda b, h, n: (b, n, h)),
        compiler_params=_params("parallel", "parallel", "arbitrary"),
        name="window_attention",
    )(qkv, qkv, qkv, qkv, qkv, qkv, qkv, qkv_c, qkv_c, sink_t)


def _nbr_attn_kernel(q_ref, kp_ref, kc_ref, kn_ref, vp_ref, vc_ref, vn_ref, kx_ref, vx_ref, bias_ref, mix_ref,
                     o_ref, kbuf, vbuf, *, grid_rows, scale):
    del mix_ref
    rb = pl.program_id(2)
    tok = NB_ROWS * GRID_W
    nk = NA_ROWS * GRID_W
    kbuf[0:tok] = kp_ref[0]
    kbuf[tok:2 * tok] = kc_ref[0]
    kbuf[2 * tok:3 * tok] = kn_ref[0]
    vbuf[0:tok] = vp_ref[0]
    vbuf[tok:2 * tok] = vc_ref[0]
    vbuf[2 * tok:3 * tok] = vn_ref[0]
    kx = kx_ref[0]
    vx = vx_ref[0]
    for rr in range(NB_ROWS):
        r = rb * NB_ROWS + rr
        r0 = jnp.clip(r - NA_ROWS // 2, 0, grid_rows - NA_ROWS)
        off = pl.multiple_of((r0 - (rb - 1) * NB_ROWS) * GRID_W, GRID_W)
        ks = kbuf[pl.ds(off, nk), :]
        vs = vbuf[pl.ds(off, nk), :]
        qr = q_ref[0, rr * GRID_W:(rr + 1) * GRID_W, :]
        s = _nt(qr, ks) * scale + bias_ref[0, r0 - r + NA_ROWS - 1]
        sx = _nt(qr, kx) * scale
        m = jnp.maximum(jnp.max(s, axis=-1, keepdims=True), jnp.max(sx, axis=-1, keepdims=True))
        p = jnp.exp(s - m)
        px = jnp.exp(sx - m)
        den = jnp.sum(p, axis=-1, keepdims=True) + jnp.sum(px, axis=-1, keepdims=True)
        o = (_mm(p.astype(BF16), vs) + _mm(px.astype(BF16), vx)) / den
        o_ref[0, rr * GRID_W:(rr + 1) * GRID_W, :] = o.astype(o_ref.dtype)


def _nbr_bias_table(rpb):
    col = np.arange(GRID_W)
    c0 = np.clip(col - NA_COLS // 2, 0, GRID_W - NA_COLS)
    col_ok = (col[None, :] >= c0[:, None]) & (col[None, :] < c0[:, None] + NA_COLS)
    dc = np.clip(col[None, :] - col[:, None] + (NA_COLS - 1), 0, 2 * NA_COLS - 2)
    dr = np.arange(NA_ROWS)[:, None] + np.arange(NA_ROWS)[None, :]
    sel_r = np.eye(2 * NA_ROWS - 1, dtype=np.float32)[dr]
    sel_c = np.eye(2 * NA_COLS - 1, dtype=np.float32)[dc]
    t = jnp.einsum('djr,hrc->hdjc', sel_r, rpb.astype(F32), precision=lax.Precision.HIGHEST)
    t = jnp.einsum('hdjc,qkc->hdqjk', t, sel_c, precision=lax.Precision.HIGHEST)
    t = jnp.where(col_ok[None, None, :, None, :], t, NEG_INF)
    return t.reshape(rpb.shape[0], NA_ROWS, GRID_W, NA_ROWS * GRID_W)


def neighbourhood_attention(qkv, qkv_c, rpb, mix):
    B, L, _ = qkv.shape
    C = qkv_c.shape[1]
    grid_rows = L // GRID_W
    tok = NB_ROWS * GRID_W
    assert grid_rows >= NA_ROWS and L % tok == 0 and NB_ROWS == NA_ROWS
    nrb = L // tok
    bias = _nbr_bias_table(rpb)
    spec = lambda f: pl.BlockSpec((1, tok, HEAD_DIM), f)
    prev = lambda c0: (lambda b, h, n: (b, jnp.maximum(n - 1, 0), c0 + h))
    cur = lambda c0: (lambda b, h, n: (b, n, c0 + h))
    nxt = lambda c0: (lambda b, h, n: (b, jnp.minimum(n + 1, nrb - 1), c0 + h))
    ctx = lambda c0: pl.BlockSpec((1, C, HEAD_DIM), lambda b, h, n: (b, 0, c0 + h))
    return pl.pallas_call(
        functools.partial(_nbr_attn_kernel, grid_rows=grid_rows, scale=HEAD_DIM ** -0.5),
        out_shape=jax.ShapeDtypeStruct(mix.shape, mix.dtype),
        grid=(B, B_HEADS, nrb),
        in_specs=[spec(cur(QB_BLK)), spec(prev(KB_BLK)), spec(cur(KB_BLK)), spec(nxt(KB_BLK)),
                  spec(prev(VB_BLK)), spec(cur(VB_BLK)), spec(nxt(VB_BLK)),
                  ctx(KB_BLK), ctx(VB_BLK),
                  pl.BlockSpec((1, NA_ROWS, GRID_W, NA_ROWS * GRID_W), lambda b, h, n: (h, 0, 0, 0)),
                  pl.BlockSpec(memory_space=pl.ANY)],
        out_specs=spec(cur(A_HEADS)),
        scratch_shapes=[pltpu.VMEM((3 * tok, HEAD_DIM), BF16), pltpu.VMEM((3 * tok, HEAD_DIM), BF16)],
        input_output_aliases={10: 0},
        compiler_params=_params("parallel", "parallel", "arbitrary"),
        name="neighbourhood_attention",
    )(qkv, qkv, qkv, qkv, qkv, qkv, qkv, qkv_c, qkv_c, bias, mix)


def _ctx_attn_kernel(q_ref, k_ref, v_ref, sink_ref, o_ref, *, scale):
    s = _nt(q_ref[0], k_ref[0]) * scale
    sink = sink_ref[0][:1, :1]
    m = jnp.maximum(jnp.max(s, axis=-1, keepdims=True), sink)
    p = jnp.exp(s - m)
    den = jnp.sum(p, axis=-1, keepdims=True) + jnp.exp(sink - m)
    o_ref[0] = (_mm(p.astype(BF16), v_ref[0]) / den).astype(o_ref.dtype)


def context_attention(qkv_c, a_sink):
    B, C, _ = qkv_c.shape
    grp = A_HEADS // A_KV_HEADS
    heads = A_HEADS + B_HEADS
    sink = jnp.concatenate([a_sink.astype(F32), jnp.full((B_HEADS,), NEG_INF, F32)])
    sink_t = jnp.broadcast_to(sink[:, None, None], (heads, 8, LANES))
    is_a = lambda h: h < A_HEADS
    qcol = lambda h: jnp.where(is_a(h), QA_BLK + h, QB_BLK + h - A_HEADS)
    kcol = lambda h: jnp.where(is_a(h), KA_BLK + h // grp, KB_BLK + h - A_HEADS)
    vcol = lambda h: jnp.where(is_a(h), VA_BLK + h // grp, VB_BLK + h - A_HEADS)
    return pl.pallas_call(
        functools.partial(_ctx_attn_kernel, scale=HEAD_DIM ** -0.5),
        out_shape=jax.ShapeDtypeStruct((B, C, MIX_WIDTH), BF16),
        grid=(B, heads),
        in_specs=[pl.BlockSpec((1, C, HEAD_DIM), lambda b, h: (b, 0, qcol(h))),
                  pl.BlockSpec((1, C, HEAD_DIM), lambda b, h: (b, 0, kcol(h))),
                  pl.BlockSpec((1, C, HEAD_DIM), lambda b, h: (b, 0, vcol(h))),
                  pl.BlockSpec((1, 8, LANES), lambda b, h: (h, 0, 0))],
        out_specs=pl.BlockSpec((1, C, HEAD_DIM), lambda b, h: (b, 0, h)),
        compiler_params=_params("parallel", "parallel"),
        name="context_attention",
    )(qkv_c, qkv_c, qkv_c, sink_t)


def _wkv_pairs(ins, sts, tri_cum, strict_bd, incl_bd, lane_lo, st_bd):
    C = ins[0][0].shape[0]
    P = range(len(ins))
    steps = int(math.log2(C))
    cat = jnp.concatenate
    cum = [jnp.dot(tri_cum, ins[p][1], preferred_element_type=F32, precision=lax.Precision.HIGHEST) for p in P]
    tot = [jnp.sum(ins[p][1], axis=0, keepdims=True) for p in P]
    g_inv = [jnp.exp(-cum[p]) for p in P]
    lhs = [cat([ins[p][4] * jnp.exp(cum[p] - ins[p][1]), ins[p][0] * jnp.exp(cum[p])], axis=0).astype(BF16)
           for p in P]
    b_t = [ins[p][5] * g_inv[p] for p in P]
    k_t = [ins[p][2] * g_inv[p] for p in P]
    rhs = [cat([jnp.where(lane_lo, b_t[p], 0.0), jnp.where(lane_lo, 0.0, b_t[p]),
                jnp.where(lane_lo, k_t[p], 0.0), jnp.where(lane_lo, 0.0, k_t[p])], axis=0).astype(BF16) for p in P]
    prod = [_nt(lhs[p], rhs[p]) for p in P]
    sh = [_nt(lhs[p], sts[p].astype(BF16)) for p in P]

    def block_diag(x, mask):
        return jnp.where(mask, cat([x, x], axis=0), 0.0).astype(BF16)

    npow = [block_diag(prod[p][0:C, 0:2 * C], strict_bd) for p in P]
    n_ak = [block_diag(prod[p][0:C, 2 * C:4 * C], strict_bd) for p in P]
    n_r = [cat([block_diag(prod[p][C:2 * C, 0:2 * C], incl_bd),
                block_diag(prod[p][C:2 * C, 2 * C:4 * C], incl_bd)], axis=1) for p in P]
    v2 = [cat([ins[p][3], ins[p][3]], axis=0).astype(BF16) for p in P]
    x = [cat([sh[p][0:C], sh[p][0:C]], axis=0) + _mm(n_ak[p], v2[p]) for p in P]
    for i in range(steps):
        x = [x[p] + _mm(npow[p], x[p].astype(BF16)) for p in P]
        if i + 1 < steps:
            npow = [_mm(npow[p], npow[p]).astype(BF16) for p in P]
    ys = [cat([sh[p][C:2 * C], sh[p][C:2 * C]], axis=0)
          + _mm(n_r[p], cat([x[p].astype(BF16), v2[p]], axis=0)) for p in P]
    y = [jnp.where(lane_lo, ys[p][0:C], ys[p][C:2 * C]) for p in P]
    u = [jnp.where(lane_lo, x[p][0:C], x[p][C:2 * C]) for p in P]
    uv_t = [cat([u[p], ins[p][3]], axis=0).T.astype(BF16) for p in P]
    g_tail = [jnp.exp(tot[p] - cum[p]) for p in P]
    bk = [cat([ins[p][5] * g_tail[p], ins[p][2] * g_tail[p]], axis=0).astype(BF16) for p in P]
    st_new = [sts[p] * jnp.exp(tot[p]) + jnp.where(st_bd, _mm(uv_t[p], bk[p]), 0.0) for p in P]
    return y, st_new


def _head_sum(x, lane_lo):
    lo = jnp.sum(jnp.where(lane_lo, x, 0.0), axis=-1, keepdims=True)
    hi = jnp.sum(jnp.where(lane_lo, 0.0, x), axis=-1, keepdims=True)
    return jnp.where(lane_lo, lo, hi)


_V_W0, _V_A0, _V_KK, _V_KA, _V_A0_OTHER, _V_RK, _V_LNW, _V_LNB = range(8)


def _wkv_kernel(*refs, reverse, pairs):
    C = WKV_CHUNK
    if reverse:
        (r_ref, k_ref, v_ref, wl_ref, al_ref, w2_ref, a2_ref, vec_ref, a2o_ref, yf_ref, g_ref,
         o_ref, st_ref) = refs
    else:
        r_ref, k_ref, v_ref, wl_ref, al_ref, w2_ref, a2_ref, vec_ref, o_ref, st_ref = refs

    @pl.when(pl.program_id(2) == 0)
    def _():
        st_ref[...] = jnp.zeros_like(st_ref)

    ti = lax.broadcasted_iota(jnp.int32, (C, C), 0)
    si = lax.broadcasted_iota(jnp.int32, (C, C), 1)
    tri_cum = ((si >= ti) if reverse else (si <= ti)).astype(F32)
    rho = lax.broadcasted_iota(jnp.int32, (2 * C, 2 * C), 0)
    sig = lax.broadcasted_iota(jnp.int32, (2 * C, 2 * C), 1)
    same = (rho // C) == (sig // C)
    tt, ss = rho % C, sig % C
    strict_bd = same & ((ss > tt) if reverse else (ss < tt))
    incl_bd = same & ((ss >= tt) if reverse else (ss <= tt))
    lane_lo = lax.broadcasted_iota(jnp.int32, (C, LANES), 1) < RW_HEAD
    st_bd = (lax.broadcasted_iota(jnp.int32, (LANES, LANES), 0) // RW_HEAD) == (
        lax.broadcasted_iota(jnp.int32, (LANES, LANES), 1) // RW_HEAD)
    sls = [slice(p * LANES, (p + 1) * LANES) for p in range(pairs)]
    vec = lambda row, sl: vec_ref[row:row + 1, sl]

    wl = jnp.tanh(wl_ref[0]).astype(BF16)
    al = al_ref[0].astype(BF16)
    ins, ags = [], []
    for sl in sls:
        w_raw = vec(_V_W0, sl) + _mm(wl, w2_ref[:, sl])
        lw = -jnp.exp(-jax.nn.softplus(-w_raw) - 0.5)
        ag = jax.nn.sigmoid(vec(_V_A0, sl) + _mm(al, a2_ref[:, sl]))
        k_raw = k_ref[0, :, sl]
        kkf = k_raw * vec(_V_KK, sl)
        kk = kkf * lax.rsqrt(_head_sum(kkf * kkf, lane_lo) + 1e-12)
        kd = k_raw * (1.0 + (ag - 1.0) * vec(_V_KA, sl))
        ins.append((r_ref[0, :, sl], lw, kd, v_ref[0, :, sl], -kk, kk * ag))
    y, st_new = _wkv_pairs(ins, [st_ref[p] for p in range(pairs)], tri_cum, strict_bd, incl_bd, lane_lo, st_bd)
    for p, sl in enumerate(sls):
        st_ref[p] = st_new[p]
        if not reverse:
            o_ref[0, :, sl] = y[p]
            continue
        r, _, kd, v, _, _ = ins[p]
        ag_o = jax.nn.sigmoid(vec(_V_A0_OTHER, sl) + _mm(al, a2o_ref[:, sl]))
        k_sum = kd + k_ref[0, :, sl] * (1.0 + (ag_o - 1.0) * vec(_V_KA, sl))
        bonus = _head_sum(r * k_sum * vec(_V_RK, sl), lane_lo) * v
        ysum = yf_ref[0, :, sl] + y[p]
        mean = _head_sum(ysum, lane_lo) * (1.0 / RW_HEAD)
        dev = ysum - mean
        var = _head_sum(dev * dev, lane_lo) * (1.0 / RW_HEAD)
        yn = dev * lax.rsqrt(var + RW_LNX_EPS) * vec(_V_LNW, sl) + vec(_V_LNB, sl)
        o_ref[0, :, sl] = ((yn + bonus) * g_ref[0, :, sl]).astype(o_ref.dtype)


def wkv_scan(r, k, v, wl, al, w2, a2, vecs, *, ctx_len, reverse, a2_other=None, y_fwd=None, g=None):
    B, T, D = r.shape
    C = WKV_CHUNK
    L = T - 2 * ctx_len
    assert ctx_len % C == 0 and L % C == 0 and 2 * C == LANES
    nchunk = (ctx_len + L) // C
    cc = ctx_len // C
    nl = L // C
    pairs = min(WKV_PAIRS, D // LANES)
    width = pairs * LANES
    assert D % width == 0
    if reverse:
        tchunk = lambda ci: cc + nchunk - 1 - ci
    else:
        tchunk = lambda ci: ci
    lchunk = lambda ci: jnp.clip(tchunk(ci) - cc, 0, nl - 1)
    tspec = lambda w: pl.BlockSpec((1, C, w), lambda bi, di, ci: (bi, tchunk(ci), di))
    tfull = lambda w: pl.BlockSpec((1, C, w), lambda bi, di, ci: (bi, tchunk(ci), 0))
    lspec = pl.BlockSpec((1, C, width), lambda bi, di, ci: (bi, lchunk(ci), di))
    wspec = lambda rows: pl.BlockSpec((rows, width), lambda bi, di, ci: (0, di))
    nlora = wl.shape[2]
    args = [r, k, v, wl, al, w2, a2, vecs]
    in_specs = [tspec(width)] * 3 + [tfull(nlora), tfull(nlora), wspec(nlora), wspec(nlora), wspec(vecs.shape[0])]
    if reverse:
        args += [a2_other, y_fwd, g]
        in_specs += [wspec(nlora), lspec, tspec(width)]
    return pl.pallas_call(
        functools.partial(_wkv_kernel, reverse=reverse, pairs=pairs),
        out_shape=jax.ShapeDtypeStruct((B, L, D), BF16 if reverse else F32),
        grid=(B, D // width, nchunk),
        in_specs=in_specs,
        out_specs=lspec,
        scratch_shapes=[pltpu.VMEM((pairs, LANES, LANES), F32)],
        compiler_params=_params("parallel", "parallel", "arbitrary"),
        name="wkv_scan_bwd" if reverse else "wkv_scan_fwd",
    )(*args)


def _moe_ffn_kernel(be_ref, nused_ref, x_ref, w1_ref, w3_ref, w2_ref, o_ref, w1b, w3b, w2b):
    i = pl.program_id(0)
    used = i < nused_ref[0]

    @pl.when(used & ((i == 0) | (be_ref[i] != be_ref[jnp.maximum(i - 1, 0)])))
    def _():
        w1b[...] = w1_ref[0, 0].astype(BF16)
        w3b[...] = w3_ref[0, 0].astype(BF16)
        w2b[...] = w2_ref[0, 0].astype(BF16)

    @pl.when(used)
    def _():
        x = x_ref[...]
        h1 = _mm(x, w1b[...])
        h3 = _mm(x, w3b[...])
        hh = (h1 * jax.nn.sigmoid(h1)) * h3
        o_ref[...] = _mm(hh.astype(BF16), w2b[...]).astype(o_ref.dtype)

    @pl.when(i >= nused_ref[0])
    def _():
        o_ref[...] = jnp.zeros_like(o_ref)


def moe_ffn(x_rows, blk_expert, n_used, w1, w3, w2, layer):
    slots, D = x_rows.shape
    F = w1.shape[3]
    nblk = slots // MOE_ROWS
    grid_spec = pltpu.PrefetchScalarGridSpec(
        num_scalar_prefetch=2,
        grid=(nblk,),
        in_specs=[pl.BlockSpec((MOE_ROWS, D), lambda i, be, nu: (i, 0)),
                  pl.BlockSpec((1, 1, D, F), lambda i, be, nu: (layer, be[i], 0, 0)),
                  pl.BlockSpec((1, 1, D, F), lambda i, be, nu: (layer, be[i], 0, 0)),
                  pl.BlockSpec((1, 1, F, D), lambda i, be, nu: (layer, be[i], 0, 0))],
        out_specs=pl.BlockSpec((MOE_ROWS, D), lambda i, be, nu: (i, 0)),
        scratch_shapes=[pltpu.VMEM((D, F), BF16), pltpu.VMEM((D, F), BF16), pltpu.VMEM((F, D), BF16)],
    )
    return pl.pallas_call(
        _moe_ffn_kernel,
        out_shape=jax.ShapeDtypeStruct((slots, D), BF16),
        grid_spec=grid_spec,
        compiler_params=_params("arbitrary"),
        name="moe_ffn",
    )(blk_expert, n_used, x_rows, w1.astype(F32), w3.astype(F32), w2.astype(F32))


def hierarchical_moe(route, hb, w1, w3, w2, layer):
    n, D = hb.shape
    p_grp = jax.nn.softmax(route[:, :N_GROUPS], axis=-1)
    g_sel = jnp.argmax(p_grp, axis=-1)
    p_sel = jnp.take_along_axis(p_grp, g_sel[:, None], axis=-1)
    logits = route[:, N_GROUPS:N_GROUPS + N_EXPERTS].reshape(n, N_GROUPS, EXPERTS_PER_GROUP)
    logits_g = jnp.take_along_axis(logits, g_sel[:, None, None], axis=1)[:, 0]
    top_val, top_idx = lax.top_k(logits_g, TOP_K)
    gate = p_sel * jax.nn.softmax(top_val, axis=-1)
    expert = (g_sel[:, None] * EXPERTS_PER_GROUP + top_idx).astype(jnp.int32)
    flat_e = expert.reshape(-1)
    onehot = (flat_e[:, None] == jnp.arange(N_EXPERTS, dtype=jnp.int32)[None, :]).astype(jnp.int32)
    csum = jnp.cumsum(onehot, axis=0)
    rank = jnp.take_along_axis(csum, flat_e[:, None], axis=1)[:, 0] - 1
    counts = csum[-1]
    padded = (counts + MOE_ROWS - 1) // MOE_ROWS * MOE_ROWS
    pad_end = jnp.cumsum(padded)
    dest = (pad_end - padded)[flat_e] + rank
    nblk = (n * TOP_K + N_EXPERTS * (MOE_ROWS - 1) + MOE_ROWS - 1) // MOE_ROWS
    slots = nblk * MOE_ROWS
    flat_tok = jnp.repeat(jnp.arange(n, dtype=jnp.int32), TOP_K)
    slot_tok = jnp.zeros((slots,), jnp.int32).at[dest].set(flat_tok)
    blk_start = jnp.arange(nblk, dtype=jnp.int32) * MOE_ROWS
    blk_expert = jnp.minimum(jnp.sum((pad_end[None, :] <= blk_start[:, None]).astype(jnp.int32), axis=1),
                             N_EXPERTS - 1).astype(jnp.int32)
    n_used = (pad_end[-1] // MOE_ROWS).astype(jnp.int32).reshape(1)
    x_rows = hb[slot_tok]
    y = moe_ffn(x_rows, blk_expert, n_used, w1, w3, w2, layer)
    d2 = dest.reshape(n, TOP_K)
    return y[d2[:, 0]].astype(F32) * gate[:, 0:1] + y[d2[:, 1]].astype(F32) * gate[:, 1:2]


def moe_router(x, norm_g, sc, sh, w_grp, w_exp):
    w_route = jnp.concatenate([w_grp, w_exp], axis=1).astype(F32)
    w_route = jnp.pad(w_route, ((0, 0), (0, LANES - w_route.shape[1])))
    return linear(x, w_route, norm=(norm_g, sc, sh), precise=True, emit_h=True)


def attention_layer(x, xc, norm_g, sc, sh, csc, csh, w_in, w_out, a_q_gain, a_k_gain, a_sink, b_q_gain, b_k_gain,
                    b_rpb, gt, cgt):
    B, L, D = x.shape
    C = xc.shape[1]
    gains = (a_q_gain, a_k_gain, b_q_gain, b_k_gain)
    qkv = qkv_projection(x.reshape(B * L, D), w_in, norm_g, sc, sh, gains, L).reshape(B, L, -1)
    qkv_c = qkv_projection(xc.reshape(B * C, D), w_in, norm_g, csc, csh, gains, None,
                           tm=min(512, B * C)).reshape(B, C, -1)
    mix = window_attention(qkv, qkv_c, a_sink)
    mix = neighbourhood_attention(qkv, qkv_c, b_rpb, mix)
    x_new = linear(mix.reshape(B * L, -1), w_out, gated=(x.reshape(B * L, D), gt)).reshape(B, L, D)
    mix_c = context_attention(qkv_c, a_sink)
    xc_new = linear(mix_c.reshape(B * C, -1), w_out, gated=(xc.reshape(B * C, D), cgt)).reshape(B, C, D)
    return x_new, xc_new


def rwkv_layer(x, xc, norm_g, sc, sh, csc, csh, gt, mu, wr, wk, wv, wo, w0, w1, w2, a0, a1, a2, g1, g2, k_k, k_a,
               r_k, lnx_w, lnx_b):
    B, L, D = x.shape
    C = xc.shape[1]
    T = L + 2 * C

    def modulated(t, s, b):
        return _rms_modulate(t.astype(F32), norm_g.astype(F32), s[:, None, :], b[:, None, :])

    def shift(t):
        tp = jnp.pad(t, ((0, 0), (1, 1), (0, 0)))
        return 0.5 * (tp[:, :-2] + tp[:, 2:]) - t

    h = modulated(x, sc, sh)
    hc = modulated(xc, jnp.broadcast_to(csc, (B, D)), jnp.broadcast_to(csh, (B, D)))
    hcat = jnp.concatenate([hc, h, hc], axis=1).reshape(B * T, D)
    xx = jnp.concatenate([shift(hc), shift(h), shift(hc)], axis=1).reshape(B * T, D)
    proj = lambda m, w: linear(hcat, w, mix=(xx, mu[m]))
    to3 = lambda t: t.reshape(B, T, -1)
    r = to3(proj(0, wr))
    k = to3(proj(2, wk))
    v = to3(proj(3, wv))
    g = to3(linear(proj(5, g1), g2, act="sigmoid"))
    wl = to3(proj(1, jnp.concatenate([w1[0], w1[1]], axis=1)))
    al = to3(proj(4, jnp.concatenate([a1[0], a1[1]], axis=1)))
    zw, za = jnp.zeros_like(w2[0]), jnp.zeros_like(a2[0])
    w2p = [jnp.concatenate([w2[0], zw], axis=0).astype(BF16), jnp.concatenate([zw, w2[1]], axis=0).astype(BF16)]
    a2p = [jnp.concatenate([a2[0], za], axis=0).astype(BF16), jnp.concatenate([za, a2[1]], axis=0).astype(BF16)]
    vecs = lambda d: jnp.stack([w0[d], a0[d], k_k, k_a, a0[1 - d], r_k.reshape(D), lnx_w, lnx_b]).astype(F32)
    y_fwd = wkv_scan(r, k, v, wl, al, w2p[0], a2p[0], vecs(0), ctx_len=C, reverse=False)
    z = wkv_scan(r, k, v, wl, al, w2p[1], a2p[1], vecs(1), ctx_len=C, reverse=True,
                 a2_other=a2p[0], y_fwd=y_fwd, g=g)
    return linear(z.reshape(B * L, D), wo, gated=(x.reshape(B * L, D), gt)).reshape(B, L, D)


def kernel(x, c, ctx, c_ctx, ada_w, ada_b, norm_mix_g, norm_ffn_g, attn_w_in, attn_w_out, a_q_gain, a_k_gain, a_sink, b_q_gain, b_k_gain, b_rpb, rw_mu, rw_wr, rw_wk, rw_wv, rw_wo, rw_w0, rw_w1, rw_w2, rw_a0, rw_a1, rw_a2, rw_g1, rw_g2, rw_k_k, rw_k_a, rw_r_k, rw_lnx_w, rw_lnx_b, moe_w_grp, moe_w_exp, moe_w1, moe_w3, moe_w2):
    B, L, D = x.shape
    C = ctx.shape[1]
    depth = ada_w.shape[0]
    assert depth == 2, "layer 0 is the attention mixer, layer 1 (last) the RWKV-7 mixer"
    pad = (-(B + 1)) % 8
    cond = jnp.concatenate([jax.nn.silu(c), jax.nn.silu(c_ctx)[None, :], jnp.zeros((pad, D), F32)], axis=0)
    xc = ctx
    for i in range(depth):
        last = i == depth - 1
        j = i // 2
        mod = linear(cond, ada_w[i], precise=True) + ada_b[i]
        sh1, sc1, gt1, sh2, sc2, gt2 = jnp.split(mod[:B], 6, axis=-1)
        csh1, csc1, cgt1, csh2, csc2, cgt2 = jnp.split(mod[B:B + 1], 6, axis=-1)
        if i % 2 == 0:
            x, xc = attention_layer(x, xc, norm_mix_g[i], sc1, sh1, csc1, csh1, attn_w_in[j], attn_w_out[j],
                                    a_q_gain[j], a_k_gain[j], a_sink[j], b_q_gain[j], b_k_gain[j], b_rpb[j],
                                    gt1, cgt1)
        else:
            x = rwkv_layer(x, xc, norm_mix_g[i], sc1, sh1, csc1, csh1, gt1, rw_mu[j], rw_wr[j], rw_wk[j], rw_wv[j],
                           rw_wo[j], rw_w0[j], rw_w1[j], rw_w2[j], rw_a0[j], rw_a1[j], rw_a2[j], rw_g1[j], rw_g2[j],
                           rw_k_k[j], rw_k_a[j], rw_r_k[j], rw_lnx_w[j], rw_lnx_b[j])
        route, hb = moe_router(x.reshape(B * L, D), norm_ffn_g[i], sc2, sh2, moe_w_grp[i], moe_w_exp[i])
        if last:
            f = hierarchical_moe(route, hb, moe_w1, moe_w3, moe_w2, i)
            x = x + gt2[:, None, :] * f.reshape(B, L, D)
        else:
            route_c, hb_c = moe_router(xc.reshape(B * C, D), norm_ffn_g[i], csc2, csh2, moe_w_grp[i], moe_w_exp[i])
            f = hierarchical_moe(jnp.concatenate([route, route_c], axis=0), jnp.concatenate([hb, hb_c], axis=0),
                                 moe_w1, moe_w3, moe_w2, i)
            x = x + gt2[:, None, :] * f[:B * L].reshape(B, L, D)
            xc = xc + cgt2[:, None, :] * f[B * L:].reshape(B, C, D)
    return x
```

```python
import functools
import math

import numpy as np
import jax
import jax.numpy as jnp
from jax import lax
from jax.experimental import pallas as pl
from jax.experimental.pallas import tpu as pltpu

GRID_W = 64
HEAD_DIM = 128
A_HEADS = 8
A_KV_HEADS = 2
A_BLOCK = 128
B_HEADS = 8
NA_ROWS = 8
NA_COLS = 16
ROPE_THETA = 10000.0
A_Q_COLS = A_HEADS * HEAD_DIM
A_KV_COLS = A_KV_HEADS * HEAD_DIM
B_COLS = B_HEADS * HEAD_DIM
QKV_COLS = A_Q_COLS + 2 * A_KV_COLS + 3 * B_COLS
MIX_WIDTH = A_Q_COLS + B_COLS
RW_HEAD = 64
RW_LNX_EPS = 64e-5
N_GROUPS = 4
EXPERTS_PER_GROUP = 8
N_EXPERTS = N_GROUPS * EXPERTS_PER_GROUP
TOP_K = 2
NORM_EPS = 1e-6
NEG_INF = -1e30

LANES = 128
VMEM_LIMIT_BYTES = 48 * 1024 * 1024
MOE_ROWS = 256
NB_ROWS = 8
WKV_CHUNK = 64
WKV_PAIRS = 16
QKV_TN = 4 * HEAD_DIM

QA_BLK = 0
KA_BLK = A_Q_COLS // HEAD_DIM
VA_BLK = KA_BLK + A_KV_HEADS
QB_BLK = VA_BLK + A_KV_HEADS
KB_BLK = QB_BLK + B_HEADS
VB_BLK = KB_BLK + B_HEADS

BF16 = jnp.bfloat16
F32 = jnp.float32


def _params(*sem):
    return pltpu.CompilerParams(dimension_semantics=sem, vmem_limit_bytes=VMEM_LIMIT_BYTES)


def _nt(a, b):
    return lax.dot_general(a, b, (((1,), (1,)), ((), ())), preferred_element_type=F32)


def _mm(a, b):
    return jnp.dot(a, b, preferred_element_type=F32)


def _rms_modulate(x, g, sc, sh):
    y = x * lax.rsqrt(jnp.mean(x * x, axis=-1, keepdims=True) + NORM_EPS)
    return (y * g) * (1.0 + sc) + sh


def _linear_kernel(*refs, mode, gated, precise, emit_h, seg):
    it = iter(refs)
    x_ref = next(it)
    if mode == "norm":
        g_ref, sc_ref, sh_ref = next(it), next(it), next(it)
    elif mode == "cat2":
        x2_ref = next(it)
    elif mode == "shiftmix":
        prev_ref, next_ref, g_ref, sc_ref, sh_ref, csc_ref, csh_ref, mu_ref = (next(it) for _ in range(8))
    w_ref = next(it)
    if gated:
        res_ref, gate_ref = next(it), next(it)
    o_ref = next(it)
    if emit_h:
        hb_ref = next(it)
    h_ref = next(it)

    @pl.when(pl.program_id(1) == 0)
    def _():
        if mode == "cat2":
            k1 = x_ref.shape[1]
            h_ref[:, 0:k1] = x_ref[...].astype(h_ref.dtype)
            h_ref[:, k1:] = x2_ref[...].astype(h_ref.dtype)
            return
        x = x_ref[...].astype(F32)
        if mode == "norm":
            x = _rms_modulate(x, g_ref[...], sc_ref[0], sh_ref[0])
        elif mode == "sigmoid":
            x = jax.nn.sigmoid(x)
        elif mode == "shiftmix":
            t_len, c_len = seg
            tm = x.shape[0]
            t0 = (pl.program_id(0) * tm) % t_len

            def hmod(rows, t):
                is_ctx = (t < c_len) | (t >= t_len - c_len)
                return _rms_modulate(rows, g_ref[...], jnp.where(is_ctx, csc_ref[...], sc_ref[0]),
                                     jnp.where(is_ctx, csh_ref[...], sh_ref[0]))

            row = lax.broadcasted_iota(jnp.int32, (tm, 1), 0)
            row8 = lax.broadcasted_iota(jnp.int32, (8, 1), 0)
            t = t0 + row
            h = hmod(x, t)
            h_before = hmod(prev_ref[...].astype(F32), t0 - 8 + row8)[7:8]
            h_after = hmod(next_ref[...].astype(F32), t0 + tm + row8)[0:1]
            up = jnp.where(row == 0, h_before, pltpu.roll(h, 1, 0))
            dn = jnp.where(row == tm - 1, h_after, pltpu.roll(h, tm - 1, 0))
            has_prev = (t != 0) & (t != c_len) & (t != t_len - c_len)
            has_next = (t != c_len - 1) & (t != t_len - c_len - 1) & (t != t_len - 1)
            xx = 0.5 * (jnp.where(has_prev, up, 0.0) + jnp.where(has_next, dn, 0.0)) - h
            x = h + xx * mu_ref[...]
        h_ref[...] = x.astype(h_ref.dtype)
        if emit_h:
            hb_ref[...] = x.astype(hb_ref.dtype)

    if precise:
        acc = jnp.dot(h_ref[...], w_ref[...], preferred_element_type=F32,
                      precision=lax.Precision.HIGHEST)
    else:
        acc = _mm(h_ref[...], w_ref[...])
    if gated:
        acc = res_ref[...] + gate_ref[0] * acc
    o_ref[...] = acc.astype(o_ref.dtype)


def linear(x, w, *, norm=None, shiftmix=None, x2=None, act=None, gated=None, out_dtype=F32, precise=False,
           emit_h=False, tm=512, tn=512):
    M, K = x.shape
    N = w.shape[1]
    tm = min(tm, M)
    tn = tn if N % tn == 0 else N
    assert M % tm == 0 and N % tn == 0
    mode = ("norm" if norm is not None else "shiftmix" if shiftmix is not None else "cat2" if x2 is not None
            else (act or "plain"))
    wdt = F32 if precise else BF16
    args = [x]
    in_specs = [pl.BlockSpec((tm, K), lambda i, j: (i, 0))]
    seg = None
    vec = lambda t: t.reshape(1, -1).astype(F32)
    vec_spec = lambda k: pl.BlockSpec((1, k), lambda i, j: (0, 0))
    if mode == "norm":
        g, sc, sh = norm
        nb = sc.shape[0]
        rpb = M // nb
        assert rpb % tm == 0
        args += [vec(g), sc.reshape(nb, 1, K).astype(F32), sh.reshape(nb, 1, K).astype(F32)]
        in_specs += [vec_spec(K),
                     pl.BlockSpec((1, 1, K), lambda i, j: ((i * tm) // rpb, 0, 0)),
                     pl.BlockSpec((1, 1, K), lambda i, j: ((i * tm) // rpb, 0, 0))]
    elif mode == "shiftmix":
        g, sc, sh, csc, csh, mu, t_len, c_len = shiftmix
        nb = sc.shape[0]
        assert M == nb * t_len and t_len % tm == 0 and tm % 8 == 0
        seg = (t_len, c_len)
        per = tm // 8
        args += [x, x, vec(g), sc.reshape(nb, 1, K).astype(F32), sh.reshape(nb, 1, K).astype(F32),
                 vec(csc), vec(csh), vec(mu)]
        in_specs += [pl.BlockSpec((8, K), lambda i, j: (jnp.maximum(i * per - 1, 0), 0)),
                     pl.BlockSpec((8, K), lambda i, j: (jnp.minimum((i + 1) * per, M // 8 - 1), 0)),
                     vec_spec(K),
                     pl.BlockSpec((1, 1, K), lambda i, j: ((i * tm) // t_len, 0, 0)),
                     pl.BlockSpec((1, 1, K), lambda i, j: ((i * tm) // t_len, 0, 0)),
                     vec_spec(K), vec_spec(K), vec_spec(K)]
    elif mode == "cat2":
        args.append(x2)
        in_specs.append(pl.BlockSpec((tm, x2.shape[1]), lambda i, j: (i, 0)))
        K = K + x2.shape[1]
    assert w.shape[0] == K
    args.append(w.astype(wdt))
    in_specs.append(pl.BlockSpec((K, tn), lambda i, j: (0, j)))
    if gated is not None:
        res, gate = gated
        nbg = gate.shape[0]
        rpg = M // nbg
        assert rpg % tm == 0
        args += [res, gate.reshape(nbg, 1, N).astype(F32)]
        in_specs += [pl.BlockSpec((tm, tn), lambda i, j: (i, j)),
                     pl.BlockSpec((1, 1, tn), lambda i, j: ((i * tm) // rpg, 0, j))]
    out_shape = jax.ShapeDtypeStruct((M, N), out_dtype)
    out_specs = pl.BlockSpec((tm, tn), lambda i, j: (i, j))
    if emit_h:
        out_shape = (out_shape, jax.ShapeDtypeStruct((M, K), BF16))
        out_specs = (out_specs, pl.BlockSpec((tm, K), lambda i, j: (i, 0)))
    return pl.pallas_call(
        functools.partial(_linear_kernel, mode=mode, gated=gated is not None, precise=precise, emit_h=emit_h,
                          seg=seg),
        out_shape=out_shape,
        grid=(M // tm, N // tn),
        in_specs=in_specs,
        out_specs=out_specs,
        scratch_shapes=[pltpu.VMEM((tm, K), wdt)],
        compiler_params=_params("parallel", "arbitrary"),
        name="linear_" + mode,
    )(*args)


def _qkv_kernel(x_ref, g_ref, sc_ref, sh_ref, w_ref, gain_ref, cos_ref, sin_ref, o_ref, h_ref, *, rope):
    j = pl.program_id(1)

    @pl.when(j == 0)
    def _():
        h_ref[...] = _rms_modulate(x_ref[...].astype(F32), g_ref[...], sc_ref[0], sh_ref[0]).astype(h_ref.dtype)

    acc = _mm(h_ref[...], w_ref[...])
    tm = acc.shape[0]

    def emit(kinds):
        for hh, kind in enumerate(kinds):
            sl = slice(hh * HEAD_DIM, (hh + 1) * HEAD_DIM)
            y = acc[:, sl]
            if kind != "plain":
                y = y * lax.rsqrt(jnp.mean(y * y, axis=-1, keepdims=True) + NORM_EPS) * gain_ref[:, sl]
                if kind == "rope" and rope:
                    lane = lax.broadcasted_iota(jnp.int32, (tm, HEAD_DIM), 1)
                    first = (lane % (HEAD_DIM // 2)) < HEAD_DIM // 4
                    partner = jnp.where(first, pltpu.roll(y, HEAD_DIM - HEAD_DIM // 4, 1),
                                        pltpu.roll(y, HEAD_DIM // 4, 1))
                    y = y * cos_ref[...] + partner * sin_ref[...]
            o_ref[:, sl] = y.astype(o_ref.dtype)

    per = QKV_TN // HEAD_DIM
    kinds = (["rope"] * A_HEADS + ["rope"] * A_KV_HEADS + ["plain"] * A_KV_HEADS + ["norm"] * B_HEADS
             + ["norm"] * B_HEADS + ["plain"] * B_HEADS)
    tiles = [tuple(kinds[t * per:(t + 1) * per]) for t in range(len(kinds) // per)]
    for kind_set in sorted(set(tiles)):
        cond = None
        for t, ks in enumerate(tiles):
            if ks == kind_set:
                cond = (j == t) if cond is None else (cond | (j == t))
        pl.when(cond)(functools.partial(emit, kind_set))


def qkv_projection(x, w_in, norm_g, sc, sh, gains, rope_len, tm=512):
    M, K = x.shape
    N = w_in.shape[1]
    assert N == QKV_COLS and N % QKV_TN == 0 and M % tm == 0
    nb = sc.shape[0]
    rpb = M // nb
    assert rpb % tm == 0
    a_q, a_k, b_q, b_k = (t.astype(F32) for t in gains)
    ones = jnp.ones((HEAD_DIM,), F32)
    gain = jnp.concatenate([jnp.tile(a_q, A_HEADS), jnp.tile(a_k, A_KV_HEADS), jnp.tile(ones, A_KV_HEADS),
                            jnp.tile(b_q, B_HEADS), jnp.tile(b_k, B_HEADS), jnp.tile(ones, B_HEADS)]).reshape(1, N)
    rope = rope_len is not None
    if rope:
        assert rope_len % tm == 0
        t = jnp.arange(rope_len)
        quarter = HEAD_DIM // 4
        inv_freq = ROPE_THETA ** (-jnp.arange(quarter, dtype=F32) / quarter)
        ar = (t // GRID_W).astype(F32)[:, None] * inv_freq[None, :]
        ac = (t % GRID_W).astype(F32)[:, None] * inv_freq[None, :]
        cos_t = jnp.concatenate([jnp.cos(ar), jnp.cos(ar), jnp.cos(ac), jnp.cos(ac)], axis=1)
        sin_t = jnp.concatenate([-jnp.sin(ar), jnp.sin(ar), -jnp.sin(ac), jnp.sin(ac)], axis=1)
        nt = rope_len // tm
    else:
        cos_t = jnp.ones((tm, HEAD_DIM), F32)
        sin_t = jnp.zeros((tm, HEAD_DIM), F32)
        nt = 1
    tab_spec = pl.BlockSpec((tm, HEAD_DIM), lambda i, j: (i % nt, 0))
    return pl.pallas_call(
        functools.partial(_qkv_kernel, rope=rope),
        out_shape=jax.ShapeDtypeStruct((M, N), BF16),
        grid=(M // tm, N // QKV_TN),
        in_specs=[pl.BlockSpec((tm, K), lambda i, j: (i, 0)),
                  pl.BlockSpec((1, K), lambda i, j: (0, 0)),
                  pl.BlockSpec((1, 1, K), lambda i, j: ((i * tm) // rpb, 0, 0)),
                  pl.BlockSpec((1, 1, K), lambda i, j: ((i * tm) // rpb, 0, 0)),
                  pl.BlockSpec((K, QKV_TN), lambda i, j: (0, j)),
                  pl.BlockSpec((1, QKV_TN), lambda i, j: (0, j)),
                  tab_spec, tab_spec],
        out_specs=pl.BlockSpec((tm, QKV_TN), lambda i, j: (i, j)),
        scratch_shapes=[pltpu.VMEM((tm, K), BF16)],
        compiler_params=_params("parallel", "arbitrary"),
        name="qkv_projection",
    )(x, norm_g.reshape(1, K).astype(F32), sc.reshape(nb, 1, K).astype(F32), sh.reshape(nb, 1, K).astype(F32),
      w_in.astype(BF16), gain, cos_t, sin_t)


def _win_attn_kernel(q_ref, kp_ref, kc_ref, kn_ref, vp_ref, vc_ref, vn_ref, kx_ref, vx_ref, sink_ref,
                     o_ref, *, nblk, groups, scale):
    n = pl.program_id(2)
    blk = A_BLOCK
    q = q_ref[0]
    qs = jnp.concatenate([q[:, g * HEAD_DIM:(g + 1) * HEAD_DIM] for g in range(groups)], axis=0)
    rows = groups * blk
    keys = jnp.concatenate([kp_ref[0], kc_ref[0], kn_ref[0], kx_ref[0]], axis=0)
    vals = jnp.concatenate([vp_ref[0], vc_ref[0], vn_ref[0], vx_ref[0]], axis=0)
    nkeys = keys.shape[0]
    s = _nt(qs, keys) * scale
    qi = lax.broadcasted_iota(jnp.int32, (rows, nkeys), 0) % blk
    kj = lax.broadcasted_iota(jnp.int32, (rows, nkeys), 1)
    ok = (((kj >= blk) | ((kj >= qi) & (n > 0)))
          & ((kj < 2 * blk) | (kj >= 3 * blk) | ((kj - 2 * blk <= qi) & (n < nblk - 1))))
    s = jnp.where(ok, s, NEG_INF)
    sink = sink_ref[0][:, :1]
    m = jnp.maximum(jnp.max(s, axis=-1, keepdims=True), sink)
    p = jnp.exp(s - m)
    den = jnp.sum(p, axis=-1, keepdims=True) + jnp.exp(sink - m)
    o = _mm(p.astype(BF16), vals) / den
    for g in range(groups):
        o_ref[0, :, g * HEAD_DIM:(g + 1) * HEAD_DIM] = o[g * blk:(g + 1) * blk].astype(o_ref.dtype)


def window_attention(qkv, qkv_c, sink):
    B, L, _ = qkv.shape
    C = qkv_c.shape[1]
    groups = A_HEADS // A_KV_HEADS
    nblk = L // A_BLOCK
    gw = groups * HEAD_DIM
    sink_t = jnp.broadcast_to(
        jnp.repeat(sink.astype(F32).reshape(A_KV_HEADS, groups), A_BLOCK, axis=1)[:, :, None],
        (A_KV_HEADS, groups * A_BLOCK, LANES))
    blkspec = lambda f: pl.BlockSpec((1, A_BLOCK, HEAD_DIM), f)
    prev = lambda c0: (lambda b, h, n: (b, jnp.maximum(n - 1, 0), c0 + h))
    cur = lambda c0: (lambda b, h, n: (b, n, c0 + h))
    nxt = lambda c0: (lambda b, h, n: (b, jnp.minimum(n + 1, nblk - 1), c0 + h))
    ctx = lambda c0: pl.BlockSpec((1, C, HEAD_DIM), lambda b, h, n: (b, 0, c0 + h))
    return pl.pallas_call(
        functools.partial(_win_attn_kernel, nblk=nblk, groups=groups, scale=HEAD_DIM ** -0.5),
        out_shape=jax.ShapeDtypeStruct((B, L, A_Q_COLS), BF16),
        grid=(B, A_KV_HEADS, nblk),
        in_specs=[pl.BlockSpec((1, A_BLOCK, gw), lambda b, h, n: (b, n, h)),
                  blkspec(prev(KA_BLK)), blkspec(cur(KA_BLK)), blkspec(nxt(KA_BLK)),
                  blkspec(prev(VA_BLK)), blkspec(cur(VA_BLK)), blkspec(nxt(VA_BLK)),
                  ctx(KA_BLK), ctx(VA_BLK),
                  pl.BlockSpec((1, groups * A_BLOCK, LANES), lambda b, h, n: (h, 0, 0))],
        out_specs=pl.BlockSpec((1, A_BLOCK, gw), lambda b, h, n: (b, n, h)),
        compiler_params=_params("parallel", "parallel", "arbitrary"),
        name="window_attention",
    )(qkv, qkv, qkv, qkv, qkv, qkv, qkv, qkv_c, qkv_c, sink_t)


def _nbr_attn_kernel(q_ref, kp_ref, kc_ref, kn_ref, vp_ref, vc_ref, vn_ref, kx_ref, vx_ref, bias_ref,
                     o_ref, kbuf, vbuf, *, grid_rows, scale):
    rb = pl.program_id(2)
    tok = NB_ROWS * GRID_W
    nk = NA_ROWS * GRID_W
    nctx = kx_ref.shape[1]
    kbuf[0:tok] = kp_ref[0]
    kbuf[tok:2 * tok] = kc_ref[0]
    kbuf[2 * tok:3 * tok] = kn_ref[0]
    vbuf[0:tok] = vp_ref[0]
    vbuf[tok:2 * tok] = vc_ref[0]
    vbuf[2 * tok:3 * tok] = vn_ref[0]
    kx = kx_ref[0]
    vx = vx_ref[0]
    sx_all = _nt(q_ref[0], kx) * scale
    for rr in range(NB_ROWS):
        r = rb * NB_ROWS + rr
        r0 = jnp.clip(r - NA_ROWS // 2, 0, grid_rows - NA_ROWS)
        off = pl.multiple_of((r0 - (rb - 1) * NB_ROWS) * GRID_W, GRID_W)
        ks = kbuf[pl.ds(off, nk), :]
        vs = vbuf[pl.ds(off, nk), :]
        qr = q_ref[0, rr * GRID_W:(rr + 1) * GRID_W, :]
        s = jnp.concatenate([_nt(qr, ks) * scale + bias_ref[0, r0 - r + NA_ROWS - 1],
                             sx_all[rr * GRID_W:(rr + 1) * GRID_W]], axis=1)
        m = jnp.max(s, axis=-1, keepdims=True)
        p = jnp.exp(s - m)
        den = jnp.sum(p, axis=-1, keepdims=True)
        pb = p.astype(BF16)
        o = (_mm(pb[:, 0:nk], vs) + _mm(pb[:, nk:nk + nctx], vx)) / den
        o_ref[0, rr * GRID_W:(rr + 1) * GRID_W, :] = o.astype(o_ref.dtype)


def _nbr_bias_table(rpb):
    col = np.arange(GRID_W)
    c0 = np.clip(col - NA_COLS // 2, 0, GRID_W - NA_COLS)
    col_ok = (col[None, :] >= c0[:, None]) & (col[None, :] < c0[:, None] + NA_COLS)
    dc = np.clip(col[None, :] - col[:, None] + (NA_COLS - 1), 0, 2 * NA_COLS - 2)
    dr = np.arange(NA_ROWS)[:, None] + np.arange(NA_ROWS)[None, :]
    sel_r = np.eye(2 * NA_ROWS - 1, dtype=np.float32)[dr]
    sel_c = np.eye(2 * NA_COLS - 1, dtype=np.float32)[dc]
    t = jnp.einsum('djr,hrc->hdjc', sel_r, rpb.astype(F32), precision=lax.Precision.HIGHEST)
    t = jnp.einsum('hdjc,qkc->hdqjk', t, sel_c, precision=lax.Precision.HIGHEST)
    t = jnp.where(col_ok[None, None, :, None, :], t, NEG_INF)
    return t.reshape(rpb.shape[0], NA_ROWS, GRID_W, NA_ROWS * GRID_W)


def neighbourhood_attention(qkv, qkv_c, rpb):
    B, L, _ = qkv.shape
    C = qkv_c.shape[1]
    grid_rows = L // GRID_W
    tok = NB_ROWS * GRID_W
    assert grid_rows >= NA_ROWS and L % tok == 0 and NB_ROWS == NA_ROWS
    nrb = L // tok
    bias = _nbr_bias_table(rpb)
    spec = lambda f: pl.BlockSpec((1, tok, HEAD_DIM), f)
    prev = lambda c0: (lambda b, h, n: (b, jnp.maximum(n - 1, 0), c0 + h))
    cur = lambda c0: (lambda b, h, n: (b, n, c0 + h))
    nxt = lambda c0: (lambda b, h, n: (b, jnp.minimum(n + 1, nrb - 1), c0 + h))
    ctx = lambda c0: pl.BlockSpec((1, C, HEAD_DIM), lambda b, h, n: (b, 0, c0 + h))
    return pl.pallas_call(
        functools.partial(_nbr_attn_kernel, grid_rows=grid_rows, scale=HEAD_DIM ** -0.5),
        out_shape=jax.ShapeDtypeStruct((B, L, B_COLS), BF16),
        grid=(B, B_HEADS, nrb),
        in_specs=[spec(cur(QB_BLK)), spec(prev(KB_BLK)), spec(cur(KB_BLK)), spec(nxt(KB_BLK)),
                  spec(prev(VB_BLK)), spec(cur(VB_BLK)), spec(nxt(VB_BLK)),
                  ctx(KB_BLK), ctx(VB_BLK),
                  pl.BlockSpec((1, NA_ROWS, GRID_W, NA_ROWS * GRID_W), lambda b, h, n: (h, 0, 0, 0))],
        out_specs=spec(cur(0)),
        scratch_shapes=[pltpu.VMEM((3 * tok, HEAD_DIM), BF16), pltpu.VMEM((3 * tok, HEAD_DIM), BF16)],
        compiler_params=_params("parallel", "parallel", "arbitrary"),
        name="neighbourhood_attention",
    )(qkv, qkv, qkv, qkv, qkv, qkv, qkv, qkv_c, qkv_c, bias)


def _ctx_attn_kernel(q_ref, k_ref, v_ref, sink_ref, o_ref, *, scale):
    s = _nt(q_ref[0], k_ref[0]) * scale
    sink = sink_ref[0][:1, :1]
    m = jnp.maximum(jnp.max(s, axis=-1, keepdims=True), sink)
    p = jnp.exp(s - m)
    den = jnp.sum(p, axis=-1, keepdims=True) + jnp.exp(sink - m)
    o_ref[0] = (_mm(p.astype(BF16), v_ref[0]) / den).astype(o_ref.dtype)


def context_attention(qkv_c, a_sink):
    B, C, _ = qkv_c.shape
    grp = A_HEADS // A_KV_HEADS
    heads = A_HEADS + B_HEADS
    sink = jnp.concatenate([a_sink.astype(F32), jnp.full((B_HEADS,), NEG_INF, F32)])
    sink_t = jnp.broadcast_to(sink[:, None, None], (heads, 8, LANES))
    is_a = lambda h: h < A_HEADS
    qcol = lambda h: jnp.where(is_a(h), QA_BLK + h, QB_BLK + h - A_HEADS)
    kcol = lambda h: jnp.where(is_a(h), KA_BLK + h // grp, KB_BLK + h - A_HEADS)
    vcol = lambda h: jnp.where(is_a(h), VA_BLK + h // grp, VB_BLK + h - A_HEADS)
    return pl.pallas_call(
        functools.partial(_ctx_attn_kernel, scale=HEAD_DIM ** -0.5),
        out_shape=jax.ShapeDtypeStruct((B, C, MIX_WIDTH), BF16),
        grid=(B, heads),
        in_specs=[pl.BlockSpec((1, C, HEAD_DIM), lambda b, h: (b, 0, qcol(h))),
                  pl.BlockSpec((1, C, HEAD_DIM), lambda b, h: (b, 0, kcol(h))),
                  pl.BlockSpec((1, C, HEAD_DIM), lambda b, h: (b, 0, vcol(h))),
                  pl.BlockSpec((1, 8, LANES), lambda b, h: (h, 0, 0))],
        out_specs=pl.BlockSpec((1, C, HEAD_DIM), lambda b, h: (b, 0, h)),
        compiler_params=_params("parallel", "parallel"),
        name="context_attention",
    )(qkv_c, qkv_c, qkv_c, sink_t)


def _wkv_pairs(ins, cum, sts, strict_s, incl_s, lane_lo, bd):
    C = ins[0][0].shape[0]
    P = range(len(ins))
    steps = int(math.log2(C))
    cat = jnp.concatenate

    def bdw(x):
        return jnp.where(bd, cat([x, x], axis=0), 0.0).astype(BF16)

    tot = [jnp.sum(ins[p][1], axis=0, keepdims=True) for p in P]
    g_inv = [jnp.exp(-cum[p]) for p in P]
    lhs = [cat([ins[p][4] * jnp.exp(cum[p] - ins[p][1]), ins[p][0] * jnp.exp(cum[p])], axis=0).astype(BF16)
           for p in P]
    b_t = [ins[p][5] * g_inv[p] for p in P]
    k_t = [ins[p][2] * g_inv[p] for p in P]
    rhs = [cat([jnp.where(lane_lo, b_t[p], 0.0), jnp.where(lane_lo, 0.0, b_t[p]),
                jnp.where(lane_lo, k_t[p], 0.0), jnp.where(lane_lo, 0.0, k_t[p])], axis=0).astype(BF16) for p in P]
    prod = [_nt(lhs[p], rhs[p]) for p in P]
    sh = [_nt(lhs[p], sts[p].astype(BF16)) for p in P]
    ns = [jnp.where(strict_s, prod[p][0:C, 0:2 * C], 0.0) for p in P]
    n_k = [cat([jnp.where(strict_s, prod[p][0:C, 2 * C:4 * C], 0.0),
                jnp.where(incl_s, prod[p][C:2 * C, 2 * C:4 * C], 0.0)], axis=0).astype(BF16) for p in P]
    n_rb = [jnp.where(incl_s, prod[p][C:2 * C, 0:2 * C], 0.0).astype(BF16) for p in P]
    kv = [_mm(n_k[p], bdw(ins[p][3])) for p in P]
    x = [sh[p][0:C] + kv[p][0:C] for p in P]
    for i in range(steps):
        nsb = [ns[p].astype(BF16) for p in P]
        if i + 1 < steps:
            z = [_mm(nsb[p], cat([bdw(x[p]), bdw(ns[p])], axis=1)) for p in P]
            x = [x[p] + z[p][:, 0:LANES] for p in P]
            ns = [z[p][:, LANES:2 * LANES] for p in P]
        else:
            x = [x[p] + _mm(nsb[p], bdw(x[p])) for p in P]
    y = [sh[p][C:2 * C] + kv[p][C:2 * C] + _mm(n_rb[p], bdw(x[p])) for p in P]
    uv_t = [cat([x[p], ins[p][3]], axis=0).T.astype(BF16) for p in P]
    g_tail = [jnp.exp(tot[p] - cum[p]) for p in P]
    bk = [cat([ins[p][5] * g_tail[p], ins[p][2] * g_tail[p]], axis=0).astype(BF16) for p in P]
    st_new = [sts[p] * jnp.exp(tot[p]) + jnp.where(bd, _mm(uv_t[p], bk[p]), 0.0) for p in P]
    return y, st_new


def _cumsum_time(x, reverse):
    C = x.shape[0]
    row = lax.broadcasted_iota(jnp.int32, x.shape, 0)
    s = 1
    while s < C:
        if reverse:
            x = x + jnp.where(row < C - s, pltpu.roll(x, C - s, 0), 0.0)
        else:
            x = x + jnp.where(row >= s, pltpu.roll(x, s, 0), 0.0)
        s *= 2
    return x


def _head_sum(x, lane_lo):
    lo = jnp.sum(jnp.where(lane_lo, x, 0.0), axis=-1, keepdims=True)
    hi = jnp.sum(jnp.where(lane_lo, 0.0, x), axis=-1, keepdims=True)
    return jnp.where(lane_lo, lo, hi)


_V_W0, _V_A0, _V_KK, _V_KA, _V_A0_OTHER, _V_RK, _V_LNW, _V_LNB = range(8)


def _wkv_kernel(*refs, reverse, pairs):
    C = WKV_CHUNK
    if reverse:
        (r_ref, k_ref, v_ref, wl_ref, al_ref, w2_ref, a2_ref, vec_ref, a2o_ref, yf_ref, g_ref,
         o_ref, st_ref) = refs
    else:
        r_ref, k_ref, v_ref, wl_ref, al_ref, w2_ref, a2_ref, vec_ref, o_ref, st_ref = refs

    @pl.when(pl.program_id(2) == 0)
    def _():
        st_ref[...] = jnp.zeros_like(st_ref)

    tt = lax.broadcasted_iota(jnp.int32, (C, 2 * C), 0)
    ss = lax.broadcasted_iota(jnp.int32, (C, 2 * C), 1) % C
    incl_s = (ss >= tt) if reverse else (ss <= tt)
    strict_s = (ss > tt) if reverse else (ss < tt)
    lane_lo = lax.broadcasted_iota(jnp.int32, (C, LANES), 1) < RW_HEAD
    bd = (lax.broadcasted_iota(jnp.int32, (LANES, LANES), 0) // RW_HEAD) == (
        lax.broadcasted_iota(jnp.int32, (LANES, LANES), 1) // RW_HEAD)
    sls = [slice(p * LANES, (p + 1) * LANES) for p in range(pairs)]
    vec = lambda row, sl: vec_ref[row:row + 1, sl]

    wl = jnp.tanh(wl_ref[0]).astype(BF16)
    al = al_ref[0].astype(BF16)
    w_raw = vec_ref[_V_W0:_V_W0 + 1, :] + _mm(wl, w2_ref[...])
    lw_all = -jnp.exp(-jax.nn.softplus(-w_raw) - 0.5)
    cum_all = _cumsum_time(lw_all, reverse)
    ag_all = jax.nn.sigmoid(vec_ref[_V_A0:_V_A0 + 1, :] + _mm(al, a2_ref[...]))
    if reverse:
        ag_o_all = jax.nn.sigmoid(vec_ref[_V_A0_OTHER:_V_A0_OTHER + 1, :] + _mm(al, a2o_ref[...]))
    ins = []
    for sl in sls:
        lw, ag = lw_all[:, sl], ag_all[:, sl]
        k_raw = k_ref[0, :, sl]
        kkf = k_raw * vec(_V_KK, sl)
        kk = kkf * lax.rsqrt(_head_sum(kkf * kkf, lane_lo) + 1e-12)
        kd = k_raw * (1.0 + (ag - 1.0) * vec(_V_KA, sl))
        ins.append((r_ref[0, :, sl], lw, kd, v_ref[0, :, sl], -kk, kk * ag))
    y, st_new = _wkv_pairs(ins, [cum_all[:, sl] for sl in sls], [st_ref[p] for p in range(pairs)],
                           strict_s, incl_s, lane_lo, bd)
    for p, sl in enumerate(sls):
        st_ref[p] = st_new[p]
        if not reverse:
            o_ref[0, :, sl] = y[p]
            continue
        r, _, kd, v, _, _ = ins[p]
        k_sum = kd + k_ref[0, :, sl] * (1.0 + (ag_o_all[:, sl] - 1.0) * vec(_V_KA, sl))
        bonus = _head_sum(r * k_sum * vec(_V_RK, sl), lane_lo) * v
        ysum = yf_ref[0, :, sl] + y[p]
        mean = _head_sum(ysum, lane_lo) * (1.0 / RW_HEAD)
        dev = ysum - mean
        var = _head_sum(dev * dev, lane_lo) * (1.0 / RW_HEAD)
        yn = dev * lax.rsqrt(var + RW_LNX_EPS) * vec(_V_LNW, sl) + vec(_V_LNB, sl)
        o_ref[0, :, sl] = ((yn + bonus) * g_ref[0, :, sl]).astype(o_ref.dtype)


def wkv_scan(r, k, v, wl, al, w2, a2, vecs, *, ctx_len, reverse, a2_other=None, y_fwd=None, g=None):
    B, T, D = r.shape
    C = WKV_CHUNK
    L = T - 2 * ctx_len
    assert ctx_len % C == 0 and L % C == 0 and 2 * C == LANES
    nchunk = (ctx_len + L) // C
    cc = ctx_len // C
    nl = L // C
    pairs = min(WKV_PAIRS, D // LANES)
    width = pairs * LANES
    assert D % width == 0
    if reverse:
        tchunk = lambda ci: cc + nchunk - 1 - ci
    else:
        tchunk = lambda ci: ci
    lchunk = lambda ci: jnp.clip(tchunk(ci) - cc, 0, nl - 1)
    tspec = lambda w: pl.BlockSpec((1, C, w), lambda bi, di, ci: (bi, tchunk(ci), di))
    tfull = lambda w: pl.BlockSpec((1, C, w), lambda bi, di, ci: (bi, tchunk(ci), 0))
    lspec = pl.BlockSpec((1, C, width), lambda bi, di, ci: (bi, lchunk(ci), di))
    wspec = lambda rows: pl.BlockSpec((rows, width), lambda bi, di, ci: (0, di))
    nlora = wl.shape[2]
    args = [r, k, v, wl, al, w2, a2, vecs]
    in_specs = [tspec(width)] * 3 + [tfull(nlora), tfull(nlora), wspec(nlora), wspec(nlora), wspec(vecs.shape[0])]
    if reverse:
        args += [a2_other, y_fwd, g]
        in_specs += [wspec(nlora), lspec, tspec(width)]
    return pl.pallas_call(
        functools.partial(_wkv_kernel, reverse=reverse, pairs=pairs),
        out_shape=jax.ShapeDtypeStruct((B, L, D), BF16 if reverse else F32),
        grid=(B, D // width, nchunk),
        in_specs=in_specs,
        out_specs=lspec,
        scratch_shapes=[pltpu.VMEM((pairs, LANES, LANES), F32)],
        compiler_params=_params("parallel", "parallel", "arbitrary"),
        name="wkv_scan_bwd" if reverse else "wkv_scan_fwd",
    )(*args)


def _moe_ffn_kernel(be_ref, nused_ref, x_ref, w1_ref, w3_ref, w2_ref, o_ref, w1b, w3b, w2b):
    i = pl.program_id(0)
    used = i < nused_ref[0]

    @pl.when(used & ((i == 0) | (be_ref[i] != be_ref[jnp.maximum(i - 1, 0)])))
    def _():
        w1b[...] = w1_ref[0, 0].astype(BF16)
        w3b[...] = w3_ref[0, 0].astype(BF16)
        w2b[...] = w2_ref[0, 0].astype(BF16)

    @pl.when(used)
    def _():
        x = x_ref[...]
        h1 = _mm(x, w1b[...])
        h3 = _mm(x, w3b[...])
        hh = (h1 * jax.nn.sigmoid(h1)) * h3
        o_ref[...] = _mm(hh.astype(BF16), w2b[...]).astype(o_ref.dtype)

    @pl.when(i >= nused_ref[0])
    def _():
        o_ref[...] = jnp.zeros_like(o_ref)


def moe_ffn(x_rows, blk_expert, n_used, w1, w3, w2, layer):
    slots, D = x_rows.shape
    F = w1.shape[3]
    nblk = slots // MOE_ROWS
    grid_spec = pltpu.PrefetchScalarGridSpec(
        num_scalar_prefetch=2,
        grid=(nblk,),
        in_specs=[pl.BlockSpec((MOE_ROWS, D), lambda i, be, nu: (i, 0)),
                  pl.BlockSpec((1, 1, D, F), lambda i, be, nu: (layer, be[i], 0, 0)),
                  pl.BlockSpec((1, 1, D, F), lambda i, be, nu: (layer, be[i], 0, 0)),
                  pl.BlockSpec((1, 1, F, D), lambda i, be, nu: (layer, be[i], 0, 0))],
        out_specs=pl.BlockSpec((MOE_ROWS, D), lambda i, be, nu: (i, 0)),
        scratch_shapes=[pltpu.VMEM((D, F), BF16), pltpu.VMEM((D, F), BF16), pltpu.VMEM((F, D), BF16)],
    )
    return pl.pallas_call(
        _moe_ffn_kernel,
        out_shape=jax.ShapeDtypeStruct((slots, D), BF16),
        grid_spec=grid_spec,
        compiler_params=_params("arbitrary"),
        name="moe_ffn",
    )(blk_expert, n_used, x_rows, w1.astype(F32), w3.astype(F32), w2.astype(F32))


def hierarchical_moe(route, hb, w1, w3, w2, layer):
    n, D = hb.shape
    p_grp = jax.nn.softmax(route[:, :N_GROUPS], axis=-1)
    g_sel = jnp.argmax(p_grp, axis=-1)
    p_sel = jnp.take_along_axis(p_grp, g_sel[:, None], axis=-1)
    logits = route[:, N_GROUPS:N_GROUPS + N_EXPERTS].reshape(n, N_GROUPS, EXPERTS_PER_GROUP)
    logits_g = jnp.take_along_axis(logits, g_sel[:, None, None], axis=1)[:, 0]
    top_val, top_idx = lax.top_k(logits_g, TOP_K)
    gate = p_sel * jax.nn.softmax(top_val, axis=-1)
    expert = (g_sel[:, None] * EXPERTS_PER_GROUP + top_idx).astype(jnp.int32)
    flat_e = expert.reshape(-1)
    onehot = (flat_e[:, None] == jnp.arange(N_EXPERTS, dtype=jnp.int32)[None, :]).astype(jnp.int32)
    csum = jnp.cumsum(onehot, axis=0)
    rank = jnp.take_along_axis(csum, flat_e[:, None], axis=1)[:, 0] - 1
    counts = csum[-1]
    padded = (counts + MOE_ROWS - 1) // MOE_ROWS * MOE_ROWS
    pad_end = jnp.cumsum(padded)
    dest = (pad_end - padded)[flat_e] + rank
    nblk = (n * TOP_K + N_EXPERTS * (MOE_ROWS - 1) + MOE_ROWS - 1) // MOE_ROWS
    slots = nblk * MOE_ROWS
    flat_tok = jnp.repeat(jnp.arange(n, dtype=jnp.int32), TOP_K)
    slot_tok = jnp.zeros((slots,), jnp.int32).at[dest].set(flat_tok)
    blk_start = jnp.arange(nblk, dtype=jnp.int32) * MOE_ROWS
    blk_expert = jnp.minimum(jnp.sum((pad_end[None, :] <= blk_start[:, None]).astype(jnp.int32), axis=1),
                             N_EXPERTS - 1).astype(jnp.int32)
    n_used = (pad_end[-1] // MOE_ROWS).astype(jnp.int32).reshape(1)
    x_rows = hb[slot_tok]
    y = moe_ffn(x_rows, blk_expert, n_used, w1, w3, w2, layer)
    d2 = dest.reshape(n, TOP_K)
    return y[d2[:, 0]].astype(F32) * gate[:, 0:1] + y[d2[:, 1]].astype(F32) * gate[:, 1:2]


def moe_router(x, norm_g, sc, sh, w_grp, w_exp):
    w_route = jnp.concatenate([w_grp, w_exp], axis=1).astype(F32)
    w_route = jnp.pad(w_route, ((0, 0), (0, LANES - w_route.shape[1])))
    return linear(x, w_route, norm=(norm_g, sc, sh), precise=True, emit_h=True)


def attention_layer(x, xc, norm_g, sc, sh, csc, csh, w_in, w_out, a_q_gain, a_k_gain, a_sink, b_q_gain, b_k_gain,
                    b_rpb, gt, cgt):
    B, L, D = x.shape
    C = xc.shape[1]
    gains = (a_q_gain, a_k_gain, b_q_gain, b_k_gain)
    qkv = qkv_projection(x.reshape(B * L, D), w_in, norm_g, sc, sh, gains, L).reshape(B, L, -1)
    qkv_c = qkv_projection(xc.reshape(B * C, D), w_in, norm_g, csc, csh, gains, None,
                           tm=min(512, B * C)).reshape(B, C, -1)
    ya = window_attention(qkv, qkv_c, a_sink)
    yb = neighbourhood_attention(qkv, qkv_c, b_rpb)
    x_new = linear(ya.reshape(B * L, -1), w_out, x2=yb.reshape(B * L, -1),
                   gated=(x.reshape(B * L, D), gt)).reshape(B, L, D)
    mix_c = context_attention(qkv_c, a_sink)
    xc_new = linear(mix_c.reshape(B * C, -1), w_out, gated=(xc.reshape(B * C, D), cgt)).reshape(B, C, D)
    return x_new, xc_new


def rwkv_layer(x, xc, norm_g, sc, sh, csc, csh, gt, mu, wr, wk, wv, wo, w0, w1, w2, a0, a1, a2, g1, g2, k_k, k_a,
               r_k, lnx_w, lnx_b):
    B, L, D = x.shape
    C = xc.shape[1]
    T = L + 2 * C

    xcat = jnp.concatenate([xc, x, xc], axis=1).reshape(B * T, D)
    proj = lambda m, w: linear(xcat, w, shiftmix=(norm_g, sc, sh, csc, csh, mu[m], T, C))
    to3 = lambda t: t.reshape(B, T, -1)
    r = to3(proj(0, wr))
    k = to3(proj(2, wk))
    v = to3(proj(3, wv))
    g = to3(linear(proj(5, g1), g2, act="sigmoid"))
    wl = to3(proj(1, jnp.concatenate([w1[0], w1[1]], axis=1)))
    al = to3(proj(4, jnp.concatenate([a1[0], a1[1]], axis=1)))
    zw, za = jnp.zeros_like(w2[0]), jnp.zeros_like(a2[0])
    w2p = [jnp.concatenate([w2[0], zw], axis=0).astype(BF16), jnp.concatenate([zw, w2[1]], axis=0).astype(BF16)]
    a2p = [jnp.concatenate([a2[0], za], axis=0).astype(BF16), jnp.concatenate([za, a2[1]], axis=0).astype(BF16)]
    vecs = lambda d: jnp.stack([w0[d], a0[d], k_k, k_a, a0[1 - d], r_k.reshape(D), lnx_w, lnx_b]).astype(F32)
    y_fwd = wkv_scan(r, k, v, wl, al, w2p[0], a2p[0], vecs(0), ctx_len=C, reverse=False)
    z = wkv_scan(r, k, v, wl, al, w2p[1], a2p[1], vecs(1), ctx_len=C, reverse=True,
                 a2_other=a2p[0], y_fwd=y_fwd, g=g)
    return linear(z.reshape(B * L, D), wo, gated=(x.reshape(B * L, D), gt)).reshape(B, L, D)


def kernel(x, c, ctx, c_ctx, ada_w, ada_b, norm_mix_g, norm_ffn_g, attn_w_in, attn_w_out, a_q_gain, a_k_gain, a_sink, b_q_gain, b_k_gain, b_rpb, rw_mu, rw_wr, rw_wk, rw_wv, rw_wo, rw_w0, rw_w1, rw_w2, rw_a0, rw_a1, rw_a2, rw_g1, rw_g2, rw_k_k, rw_k_a, rw_r_k, rw_lnx_w, rw_lnx_b, moe_w_grp, moe_w_exp, moe_w1, moe_w3, moe_w2):
    B, L, D = x.shape
    C = ctx.shape[1]
    depth = ada_w.shape[0]
    assert depth == 2, "layer 0 is the attention mixer, layer 1 (last) the RWKV-7 mixer"
    pad = (-(B + 1)) % 8
    cond = jnp.concatenate([jax.nn.silu(c), jax.nn.silu(c_ctx)[None, :], jnp.zeros((pad, D), F32)], axis=0)
    xc = ctx
    for i in range(depth):
        last = i == depth - 1
        j = i // 2
        mod = linear(cond, ada_w[i], precise=True) + ada_b[i]
        sh1, sc1, gt1, sh2, sc2, gt2 = jnp.split(mod[:B], 6, axis=-1)
        csh1, csc1, cgt1, csh2, csc2, cgt2 = jnp.split(mod[B:B + 1], 6, axis=-1)
        if i % 2 == 0:
            x, xc = attention_layer(x, xc, norm_mix_g[i], sc1, sh1, csc1, csh1, attn_w_in[j], attn_w_out[j],
                                    a_q_gain[j], a_k_gain[j], a_sink[j], b_q_gain[j], b_k_gain[j], b_rpb[j],
                                    gt1, cgt1)
        else:
            x = rwkv_layer(x, xc, norm_mix_g[i], sc1, sh1, csc1, csh1, gt1, rw_mu[j], rw_wr[j], rw_wk[j], rw_wv[j],
                           rw_wo[j], rw_w0[j], rw_w1[j], rw_w2[j], rw_a0[j], rw_a1[j], rw_a2[j], rw_g1[j], rw_g2[j],
                           rw_k_k[j], rw_k_a[j], rw_r_k[j], rw_lnx_w[j], rw_lnx_b[j])
        route, hb = moe_router(x.reshape(B * L, D), norm_ffn_g[i], sc2, sh2, moe_w_grp[i], moe_w_exp[i])
        if last:
            f = hierarchical_moe(route, hb, moe_w1, moe_w3, moe_w2, i)
            x = x + gt2[:, None, :] * f.reshape(B, L, D)
        else:
            route_c, hb_c = moe_router(xc.reshape(B * C, D), norm_ffn_g[i], csc2, csh2, moe_w_grp[i], moe_w_exp[i])
            f = hierarchical_moe(jnp.concatenate([route, route_c], axis=0), jnp.concatenate([hb, hb_c], axis=0),
                                 moe_w1, moe_w3, moe_w2, i)
            x = x + gt2[:, None, :] * f[:B * L].reshape(B, L, D)
            xc = xc + cgt2[:, None, :] * f[B * L:].reshape(B, C, D)
    return x
```

```python
import functools
import math

import numpy as np
import jax
import jax.numpy as jnp
from jax import lax
from jax.experimental import pallas as pl
from jax.experimental.pallas import tpu as pltpu

GRID_W = 64
HEAD_DIM = 128
A_HEADS = 8
A_KV_HEADS = 2
A_BLOCK = 128
B_HEADS = 8
NA_ROWS = 8
NA_COLS = 16
ROPE_THETA = 10000.0
A_Q_COLS = A_HEADS * HEAD_DIM
A_KV_COLS = A_KV_HEADS * HEAD_DIM
B_COLS = B_HEADS * HEAD_DIM
QKV_COLS = A_Q_COLS + 2 * A_KV_COLS + 3 * B_COLS
MIX_WIDTH = A_Q_COLS + B_COLS
RW_HEAD = 64
RW_LNX_EPS = 64e-5
N_GROUPS = 4
EXPERTS_PER_GROUP = 8
N_EXPERTS = N_GROUPS * EXPERTS_PER_GROUP
TOP_K = 2
NORM_EPS = 1e-6
NEG_INF = -1e30

LANES = 128
VMEM_LIMIT_BYTES = 48 * 1024 * 1024
MOE_ROWS = 256
NB_ROWS = 8
WKV_CHUNK = 64
WKV_PAIRS = 16
QKV_TN = 4 * HEAD_DIM

QA_BLK = 0
KA_BLK = A_Q_COLS // HEAD_DIM
VA_BLK = KA_BLK + A_KV_HEADS
QB_BLK = VA_BLK + A_KV_HEADS
KB_BLK = QB_BLK + B_HEADS
VB_BLK = KB_BLK + B_HEADS

BF16 = jnp.bfloat16
F32 = jnp.float32


def _params(*sem):
    return pltpu.CompilerParams(dimension_semantics=sem, vmem_limit_bytes=VMEM_LIMIT_BYTES)


def _nt(a, b):
    return lax.dot_general(a, b, (((1,), (1,)), ((), ())), preferred_element_type=F32)


def _mm(a, b):
    return jnp.dot(a, b, preferred_element_type=F32)


def _rms_modulate(x, g, sc, sh):
    y = x * lax.rsqrt(jnp.mean(x * x, axis=-1, keepdims=True) + NORM_EPS)
    return (y * g) * (1.0 + sc) + sh


def _linear_kernel(*refs, mode, gated, precise, emit_h, layered):
    it = iter(refs)
    x_ref = next(it)
    if mode == "norm":
        g_ref, sc_ref, sh_ref = next(it), next(it), next(it)
    elif mode == "cat2":
        x2_ref = next(it)
    w_ref = next(it)
    if gated:
        res_ref, gate_ref = next(it), next(it)
    o_ref = next(it)
    if emit_h:
        hb_ref = next(it)
    h_ref = next(it)

    @pl.when(pl.program_id(1) == 0)
    def _():
        if mode == "cat2":
            k1 = x_ref.shape[1]
            h_ref[:, 0:k1] = x_ref[...].astype(h_ref.dtype)
            h_ref[:, k1:] = x2_ref[...].astype(h_ref.dtype)
            return
        x = x_ref[...].astype(F32)
        if mode == "norm":
            x = _rms_modulate(x, g_ref[...], sc_ref[0], sh_ref[0])
        elif mode == "sigmoid":
            x = jax.nn.sigmoid(x)
        h_ref[...] = x.astype(h_ref.dtype)
        if emit_h:
            hb_ref[...] = x.astype(hb_ref.dtype)

    w = w_ref[0] if layered else w_ref[...]
    if precise:
        acc = jnp.dot(h_ref[...], w, preferred_element_type=F32, precision=lax.Precision.HIGHEST)
    else:
        acc = _mm(h_ref[...], w)
    if gated:
        acc = res_ref[...] + gate_ref[0] * acc
    o_ref[...] = acc.astype(o_ref.dtype)


def linear(x, w, *, norm=None, x2=None, act=None, gated=None, out_dtype=F32, precise=False, emit_h=False,
           layer=None, x_block=None, tm=512, tn=512):
    M = x.shape[0]
    K = x.shape[1] if x_block is None else x_block[1]
    xcol = 0 if x_block is None else x_block[0]
    N = w.shape[-1]
    tm = min(tm, M)
    tn = tn if N % tn == 0 else N
    assert M % tm == 0 and N % tn == 0
    mode = "norm" if norm is not None else "cat2" if x2 is not None else (act or "plain")
    wdt = F32 if precise else BF16
    args = [x]
    in_specs = [pl.BlockSpec((tm, K), lambda i, j: (i, xcol))]
    vec = lambda t: t.reshape(1, -1).astype(F32)
    vec_spec = lambda k: pl.BlockSpec((1, k), lambda i, j: (0, 0))
    if mode == "norm":
        g, sc, sh = norm
        nb = sc.shape[0]
        rpb = M // nb
        assert rpb % tm == 0
        args += [vec(g), sc.reshape(nb, 1, K).astype(F32), sh.reshape(nb, 1, K).astype(F32)]
        in_specs += [vec_spec(K),
                     pl.BlockSpec((1, 1, K), lambda i, j: ((i * tm) // rpb, 0, 0)),
                     pl.BlockSpec((1, 1, K), lambda i, j: ((i * tm) // rpb, 0, 0))]
    elif mode == "cat2":
        args.append(x2)
        in_specs.append(pl.BlockSpec((tm, x2.shape[1]), lambda i, j: (i, 0)))
        K = K + x2.shape[1]
    assert w.shape[-2] == K
    args.append(w.astype(wdt))
    if layer is None:
        in_specs.append(pl.BlockSpec((K, tn), lambda i, j: (0, j)))
    else:
        in_specs.append(pl.BlockSpec((1, K, tn), lambda i, j: (layer, 0, j)))
    if gated is not None:
        res, gate = gated
        nbg = gate.shape[0]
        rpg = M // nbg
        assert rpg % tm == 0
        args += [res, gate.reshape(nbg, 1, N).astype(F32)]
        in_specs += [pl.BlockSpec((tm, tn), lambda i, j: (i, j)),
                     pl.BlockSpec((1, 1, tn), lambda i, j: ((i * tm) // rpg, 0, j))]
    out_shape = jax.ShapeDtypeStruct((M, N), out_dtype)
    out_specs = pl.BlockSpec((tm, tn), lambda i, j: (i, j))
    if emit_h:
        out_shape = (out_shape, jax.ShapeDtypeStruct((M, K), BF16))
        out_specs = (out_specs, pl.BlockSpec((tm, K), lambda i, j: (i, 0)))
    return pl.pallas_call(
        functools.partial(_linear_kernel, mode=mode, gated=gated is not None, precise=precise, emit_h=emit_h,
                          layered=layer is not None),
        out_shape=out_shape,
        grid=(M // tm, N // tn),
        in_specs=in_specs,
        out_specs=out_specs,
        scratch_shapes=[pltpu.VMEM((tm, K), wdt)],
        compiler_params=_params("parallel", "arbitrary"),
        name="linear_" + mode,
    )(*args)


RW_TN = 512


def _rwkv_proj_kernel(x_ref, prev_ref, next_ref, g_ref, sc_ref, sh_ref, csc_ref, csh_ref, mu_ref, w_ref,
                      o_ref, hs_ref, xs_ref, h_ref, *, t_len, c_len, wide_tiles):
    j = pl.program_id(1)
    tm = x_ref.shape[0]

    @pl.when(j == 0)
    def _():
        t0 = (pl.program_id(0) * tm) % t_len

        def hmod(rows, t):
            is_ctx = (t < c_len) | (t >= t_len - c_len)
            return _rms_modulate(rows, g_ref[...], jnp.where(is_ctx, csc_ref[...], sc_ref[0]),
                                 jnp.where(is_ctx, csh_ref[...], sh_ref[0]))

        row = lax.broadcasted_iota(jnp.int32, (tm, 1), 0)
        row8 = lax.broadcasted_iota(jnp.int32, (8, 1), 0)
        t = t0 + row
        h = hmod(x_ref[...].astype(F32), t)
        h_before = hmod(prev_ref[...].astype(F32), t0 - 8 + row8)[7:8]
        h_after = hmod(next_ref[...].astype(F32), t0 + tm + row8)[0:1]
        up = jnp.where(row == 0, h_before, pltpu.roll(h, 1, 0))
        dn = jnp.where(row == tm - 1, h_after, pltpu.roll(h, tm - 1, 0))
        has_prev = (t != 0) & (t != c_len) & (t != t_len - c_len)
        has_next = (t != c_len - 1) & (t != t_len - c_len - 1) & (t != t_len - 1)
        hs_ref[...] = h
        xs_ref[...] = 0.5 * (jnp.where(has_prev, up, 0.0) + jnp.where(has_next, dn, 0.0)) - h

    group = jnp.where(j < 3 * wide_tiles, j // wide_tiles, j - 3 * wide_tiles + 3)

    @pl.when((j % wide_tiles == 0) | (j >= 3 * wide_tiles))
    def _():
        h_ref[...] = (hs_ref[...] + xs_ref[...] * mu_ref[pl.ds(group, 1), :]).astype(h_ref.dtype)

    o_ref[...] = _mm(h_ref[...], w_ref[...])


def rwkv_projections(xcat, norm_g, sc, sh, csc, csh, mu, weights, t_len, c_len, tm=512):
    M, K = xcat.shape
    nb = sc.shape[0]
    assert M == nb * t_len and t_len % tm == 0 and tm % 8 == 0 and K % RW_TN == 0
    wr, wk, wv, g1, w1c, a1c = weights
    padw = lambda w: jnp.pad(w, ((0, 0), (0, RW_TN - w.shape[1])))
    w_all = jnp.concatenate([wr, wk, wv, padw(g1), padw(w1c), padw(a1c)], axis=1).astype(BF16)
    mu_g = jnp.pad(mu[jnp.array([0, 2, 3, 5, 1, 4])].astype(F32), ((0, 2), (0, 0)))
    n_out = w_all.shape[1]
    per = tm // 8
    vec = lambda t: t.reshape(1, -1).astype(F32)
    vec_spec = pl.BlockSpec((1, K), lambda i, j: (0, 0))
    mod_spec = pl.BlockSpec((1, 1, K), lambda i, j: ((i * tm) // t_len, 0, 0))
    return pl.pallas_call(
        functools.partial(_rwkv_proj_kernel, t_len=t_len, c_len=c_len, wide_tiles=K // RW_TN),
        out_shape=jax.ShapeDtypeStruct((M, n_out), F32),
        grid=(M // tm, n_out // RW_TN),
        in_specs=[pl.BlockSpec((tm, K), lambda i, j: (i, 0)),
                  pl.BlockSpec((8, K), lambda i, j: (jnp.maximum(i * per - 1, 0), 0)),
                  pl.BlockSpec((8, K), lambda i, j: (jnp.minimum((i + 1) * per, M // 8 - 1), 0)),
                  vec_spec, mod_spec, mod_spec, vec_spec, vec_spec,
                  pl.BlockSpec((8, K), lambda i, j: (0, 0)),
                  pl.BlockSpec((K, RW_TN), lambda i, j: (0, j))],
        out_specs=pl.BlockSpec((tm, RW_TN), lambda i, j: (i, j)),
        scratch_shapes=[pltpu.VMEM((tm, K), F32), pltpu.VMEM((tm, K), F32), pltpu.VMEM((tm, K), BF16)],
        compiler_params=_params("parallel", "arbitrary"),
        name="rwkv_projections",
    )(xcat, xcat, xcat, vec(norm_g), sc.reshape(nb, 1, K).astype(F32), sh.reshape(nb, 1, K).astype(F32),
      vec(csc), vec(csh), mu_g, w_all)


def _qkv_kernel(x_ref, g_ref, sc_ref, sh_ref, w_ref, gain_ref, cos_ref, sin_ref, o_ref, h_ref, *, rope):
    j = pl.program_id(1)

    @pl.when(j == 0)
    def _():
        h_ref[...] = _rms_modulate(x_ref[...].astype(F32), g_ref[...], sc_ref[0], sh_ref[0]).astype(h_ref.dtype)

    acc = _mm(h_ref[...], w_ref[...])
    tm = acc.shape[0]

    def emit(kinds):
        for hh, kind in enumerate(kinds):
            sl = slice(hh * HEAD_DIM, (hh + 1) * HEAD_DIM)
            y = acc[:, sl]
            if kind != "plain":
                y = y * lax.rsqrt(jnp.mean(y * y, axis=-1, keepdims=True) + NORM_EPS) * gain_ref[:, sl]
                if kind == "rope" and rope:
                    lane = lax.broadcasted_iota(jnp.int32, (tm, HEAD_DIM), 1)
                    first = (lane % (HEAD_DIM // 2)) < HEAD_DIM // 4
                    partner = jnp.where(first, pltpu.roll(y, HEAD_DIM - HEAD_DIM // 4, 1),
                                        pltpu.roll(y, HEAD_DIM // 4, 1))
                    y = y * cos_ref[...] + partner * sin_ref[...]
            o_ref[:, sl] = y.astype(o_ref.dtype)

    per = QKV_TN // HEAD_DIM
    kinds = (["rope"] * A_HEADS + ["rope"] * A_KV_HEADS + ["plain"] * A_KV_HEADS + ["norm"] * B_HEADS
             + ["norm"] * B_HEADS + ["plain"] * B_HEADS)
    tiles = [tuple(kinds[t * per:(t + 1) * per]) for t in range(len(kinds) // per)]
    for kind_set in sorted(set(tiles)):
        cond = None
        for t, ks in enumerate(tiles):
            if ks == kind_set:
                cond = (j == t) if cond is None else (cond | (j == t))
        pl.when(cond)(functools.partial(emit, kind_set))


def qkv_projection(x, w_in, norm_g, sc, sh, gains, rope_len, tm=512):
    M, K = x.shape
    N = w_in.shape[1]
    assert N == QKV_COLS and N % QKV_TN == 0 and M % tm == 0
    nb = sc.shape[0]
    rpb = M // nb
    assert rpb % tm == 0
    a_q, a_k, b_q, b_k = (t.astype(F32) for t in gains)
    ones = jnp.ones((HEAD_DIM,), F32)
    gain = jnp.concatenate([jnp.tile(a_q, A_HEADS), jnp.tile(a_k, A_KV_HEADS), jnp.tile(ones, A_KV_HEADS),
                            jnp.tile(b_q, B_HEADS), jnp.tile(b_k, B_HEADS), jnp.tile(ones, B_HEADS)]).reshape(1, N)
    rope = rope_len is not None
    if rope:
        assert rope_len % tm == 0
        t = jnp.arange(rope_len)
        quarter = HEAD_DIM // 4
        inv_freq = ROPE_THETA ** (-jnp.arange(quarter, dtype=F32) / quarter)
        ar = (t // GRID_W).astype(F32)[:, None] * inv_freq[None, :]
        ac = (t % GRID_W).astype(F32)[:, None] * inv_freq[None, :]
        cos_t = jnp.concatenate([jnp.cos(ar), jnp.cos(ar), jnp.cos(ac), jnp.cos(ac)], axis=1)
        sin_t = jnp.concatenate([-jnp.sin(ar), jnp.sin(ar), -jnp.sin(ac), jnp.sin(ac)], axis=1)
        nt = rope_len // tm
    else:
        cos_t = jnp.ones((tm, HEAD_DIM), F32)
        sin_t = jnp.zeros((tm, HEAD_DIM), F32)
        nt = 1
    tab_spec = pl.BlockSpec((tm, HEAD_DIM), lambda i, j: (i % nt, 0))
    return pl.pallas_call(
        functools.partial(_qkv_kernel, rope=rope),
        out_shape=jax.ShapeDtypeStruct((M, N), BF16),
        grid=(M // tm, N // QKV_TN),
        in_specs=[pl.BlockSpec((tm, K), lambda i, j: (i, 0)),
                  pl.BlockSpec((1, K), lambda i, j: (0, 0)),
                  pl.BlockSpec((1, 1, K), lambda i, j: ((i * tm) // rpb, 0, 0)),
                  pl.BlockSpec((1, 1, K), lambda i, j: ((i * tm) // rpb, 0, 0)),
                  pl.BlockSpec((K, QKV_TN), lambda i, j: (0, j)),
                  pl.BlockSpec((1, QKV_TN), lambda i, j: (0, j)),
                  tab_spec, tab_spec],
        out_specs=pl.BlockSpec((tm, QKV_TN), lambda i, j: (i, j)),
        scratch_shapes=[pltpu.VMEM((tm, K), BF16)],
        compiler_params=_params("parallel", "arbitrary"),
        name="qkv_projection",
    )(x, norm_g.reshape(1, K).astype(F32), sc.reshape(nb, 1, K).astype(F32), sh.reshape(nb, 1, K).astype(F32),
      w_in.astype(BF16), gain, cos_t, sin_t)


def _win_attn_kernel(q_ref, kp_ref, kc_ref, kn_ref, vp_ref, vc_ref, vn_ref, kx_ref, vx_ref, sink_ref,
                     o_ref, *, nblk, groups, scale):
    n = pl.program_id(2)
    blk = A_BLOCK
    q = q_ref[0]
    qs = jnp.concatenate([q[:, g * HEAD_DIM:(g + 1) * HEAD_DIM] for g in range(groups)], axis=0)
    rows = groups * blk
    keys = jnp.concatenate([kp_ref[0], kc_ref[0], kn_ref[0], kx_ref[0]], axis=0)
    vals = jnp.concatenate([vp_ref[0], vc_ref[0], vn_ref[0], vx_ref[0]], axis=0)
    nkeys = keys.shape[0]
    s = _nt(qs, keys) * scale
    qi = lax.broadcasted_iota(jnp.int32, (rows, nkeys), 0) % blk
    kj = lax.broadcasted_iota(jnp.int32, (rows, nkeys), 1)
    ok = (((kj >= blk) | ((kj >= qi) & (n > 0)))
          & ((kj < 2 * blk) | (kj >= 3 * blk) | ((kj - 2 * blk <= qi) & (n < nblk - 1))))
    s = jnp.where(ok, s, NEG_INF)
    sink = sink_ref[0][:, :1]
    m = jnp.maximum(jnp.max(s, axis=-1, keepdims=True), sink)
    p = jnp.exp(s - m)
    den = jnp.sum(p, axis=-1, keepdims=True) + jnp.exp(sink - m)
    o = _mm(p.astype(BF16), vals) / den
    for g in range(groups):
        o_ref[0, :, g * HEAD_DIM:(g + 1) * HEAD_DIM] = o[g * blk:(g + 1) * blk].astype(o_ref.dtype)


def window_attention(qkv, qkv_c, sink):
    B, L, _ = qkv.shape
    C = qkv_c.shape[1]
    groups = A_HEADS // A_KV_HEADS
    nblk = L // A_BLOCK
    gw = groups * HEAD_DIM
    sink_t = jnp.broadcast_to(
        jnp.repeat(sink.astype(F32).reshape(A_KV_HEADS, groups), A_BLOCK, axis=1)[:, :, None],
        (A_KV_HEADS, groups * A_BLOCK, LANES))
    blkspec = lambda f: pl.BlockSpec((1, A_BLOCK, HEAD_DIM), f)
    prev = lambda c0: (lambda b, h, n: (b, jnp.maximum(n - 1, 0), c0 + h))
    cur = lambda c0: (lambda b, h, n: (b, n, c0 + h))
    nxt = lambda c0: (lambda b, h, n: (b, jnp.minimum(n + 1, nblk - 1), c0 + h))
    ctx = lambda c0: pl.BlockSpec((1, C, HEAD_DIM), lambda b, h, n: (b, 0, c0 + h))
    return pl.pallas_call(
        functools.partial(_win_attn_kernel, nblk=nblk, groups=groups, scale=HEAD_DIM ** -0.5),
        out_shape=jax.ShapeDtypeStruct((B, L, A_Q_COLS), BF16),
        grid=(B, A_KV_HEADS, nblk),
        in_specs=[pl.BlockSpec((1, A_BLOCK, gw), lambda b, h, n: (b, n, h)),
                  blkspec(prev(KA_BLK)), blkspec(cur(KA_BLK)), blkspec(nxt(KA_BLK)),
                  blkspec(prev(VA_BLK)), blkspec(cur(VA_BLK)), blkspec(nxt(VA_BLK)),
                  ctx(KA_BLK), ctx(VA_BLK),
                  pl.BlockSpec((1, groups * A_BLOCK, LANES), lambda b, h, n: (h, 0, 0))],
        out_specs=pl.BlockSpec((1, A_BLOCK, gw), lambda b, h, n: (b, n, h)),
        compiler_params=_params("parallel", "parallel", "arbitrary"),
        name="window_attention",
    )(qkv, qkv, qkv, qkv, qkv, qkv, qkv, qkv_c, qkv_c, sink_t)


def _nbr_attn_kernel(q_ref, kp_ref, kc_ref, kn_ref, vp_ref, vc_ref, vn_ref, kx_ref, vx_ref, bias_ref,
                     o_ref, kbuf, vbuf, *, grid_rows, scale):
    rb = pl.program_id(2)
    tok = NB_ROWS * GRID_W
    nk = NA_ROWS * GRID_W
    nctx = kx_ref.shape[1]
    kbuf[0:tok] = kp_ref[0]
    kbuf[tok:2 * tok] = kc_ref[0]
    kbuf[2 * tok:3 * tok] = kn_ref[0]
    vbuf[0:tok] = vp_ref[0]
    vbuf[tok:2 * tok] = vc_ref[0]
    vbuf[2 * tok:3 * tok] = vn_ref[0]
    kx = kx_ref[0]
    vx = vx_ref[0]
    sx_all = _nt(q_ref[0], kx) * scale
    for rr in range(NB_ROWS):
        r = rb * NB_ROWS + rr
        r0 = jnp.clip(r - NA_ROWS // 2, 0, grid_rows - NA_ROWS)
        off = pl.multiple_of((r0 - (rb - 1) * NB_ROWS) * GRID_W, GRID_W)
        ks = kbuf[pl.ds(off, nk), :]
        vs = vbuf[pl.ds(off, nk), :]
        qr = q_ref[0, rr * GRID_W:(rr + 1) * GRID_W, :]
        s = jnp.concatenate([_nt(qr, ks) * scale + bias_ref[0, r0 - r + NA_ROWS - 1],
                             sx_all[rr * GRID_W:(rr + 1) * GRID_W]], axis=1)
        m = jnp.max(s, axis=-1, keepdims=True)
        p = jnp.exp(s - m)
        den = jnp.sum(p, axis=-1, keepdims=True)
        pb = p.astype(BF16)
        o = (_mm(pb[:, 0:nk], vs) + _mm(pb[:, nk:nk + nctx], vx)) / den
        o_ref[0, rr * GRID_W:(rr + 1) * GRID_W, :] = o.astype(o_ref.dtype)


def _nbr_bias_table(rpb):
    col = np.arange(GRID_W)
    c0 = np.clip(col - NA_COLS // 2, 0, GRID_W - NA_COLS)
    col_ok = (col[None, :] >= c0[:, None]) & (col[None, :] < c0[:, None] + NA_COLS)
    dc = np.clip(col[None, :] - col[:, None] + (NA_COLS - 1), 0, 2 * NA_COLS - 2)
    dr = np.arange(NA_ROWS)[:, None] + np.arange(NA_ROWS)[None, :]
    sel_r = np.eye(2 * NA_ROWS - 1, dtype=np.float32)[dr]
    sel_c = np.eye(2 * NA_COLS - 1, dtype=np.float32)[dc]
    t = jnp.einsum('djr,hrc->hdjc', sel_r, rpb.astype(F32), precision=lax.Precision.HIGHEST)
    t = jnp.einsum('hdjc,qkc->hdqjk', t, sel_c, precision=lax.Precision.HIGHEST)
    t = jnp.where(col_ok[None, None, :, None, :], t, NEG_INF)
    return t.reshape(rpb.shape[0], NA_ROWS, GRID_W, NA_ROWS * GRID_W)


def neighbourhood_attention(qkv, qkv_c, rpb):
    B, L, _ = qkv.shape
    C = qkv_c.shape[1]
    grid_rows = L // GRID_W
    tok = NB_ROWS * GRID_W
    assert grid_rows >= NA_ROWS and L % tok == 0 and NB_ROWS == NA_ROWS
    nrb = L // tok
    bias = _nbr_bias_table(rpb)
    spec = lambda f: pl.BlockSpec((1, tok, HEAD_DIM), f)
    prev = lambda c0: (lambda b, h, n: (b, jnp.maximum(n - 1, 0), c0 + h))
    cur = lambda c0: (lambda b, h, n: (b, n, c0 + h))
    nxt = lambda c0: (lambda b, h, n: (b, jnp.minimum(n + 1, nrb - 1), c0 + h))
    ctx = lambda c0: pl.BlockSpec((1, C, HEAD_DIM), lambda b, h, n: (b, 0, c0 + h))
    return pl.pallas_call(
        functools.partial(_nbr_attn_kernel, grid_rows=grid_rows, scale=HEAD_DIM ** -0.5),
        out_shape=jax.ShapeDtypeStruct((B, L, B_COLS), BF16),
        grid=(B, B_HEADS, nrb),
        in_specs=[spec(cur(QB_BLK)), spec(prev(KB_BLK)), spec(cur(KB_BLK)), spec(nxt(KB_BLK)),
                  spec(prev(VB_BLK)), spec(cur(VB_BLK)), spec(nxt(VB_BLK)),
                  ctx(KB_BLK), ctx(VB_BLK),
                  pl.BlockSpec((1, NA_ROWS, GRID_W, NA_ROWS * GRID_W), lambda b, h, n: (h, 0, 0, 0))],
        out_specs=spec(cur(0)),
        scratch_shapes=[pltpu.VMEM((3 * tok, HEAD_DIM), BF16), pltpu.VMEM((3 * tok, HEAD_DIM), BF16)],
        compiler_params=_params("parallel", "parallel", "arbitrary"),
        name="neighbourhood_attention",
    )(qkv, qkv, qkv, qkv, qkv, qkv, qkv, qkv_c, qkv_c, bias)


def _ctx_attn_kernel(q_ref, k_ref, v_ref, sink_ref, o_ref, *, scale):
    s = _nt(q_ref[0], k_ref[0]) * scale
    sink = sink_ref[0][:1, :1]
    m = jnp.maximum(jnp.max(s, axis=-1, keepdims=True), sink)
    p = jnp.exp(s - m)
    den = jnp.sum(p, axis=-1, keepdims=True) + jnp.exp(sink - m)
    o_ref[0] = (_mm(p.astype(BF16), v_ref[0]) / den).astype(o_ref.dtype)


def context_attention(qkv_c, a_sink):
    B, C, _ = qkv_c.shape
    grp = A_HEADS // A_KV_HEADS
    heads = A_HEADS + B_HEADS
    sink = jnp.concatenate([a_sink.astype(F32), jnp.full((B_HEADS,), NEG_INF, F32)])
    sink_t = jnp.broadcast_to(sink[:, None, None], (heads, 8, LANES))
    is_a = lambda h: h < A_HEADS
    qcol = lambda h: jnp.where(is_a(h), QA_BLK + h, QB_BLK + h - A_HEADS)
    kcol = lambda h: jnp.where(is_a(h), KA_BLK + h // grp, KB_BLK + h - A_HEADS)
    vcol = lambda h: jnp.where(is_a(h), VA_BLK + h // grp, VB_BLK + h - A_HEADS)
    return pl.pallas_call(
        functools.partial(_ctx_attn_kernel, scale=HEAD_DIM ** -0.5),
        out_shape=jax.ShapeDtypeStruct((B, C, MIX_WIDTH), BF16),
        grid=(B, heads),
        in_specs=[pl.BlockSpec((1, C, HEAD_DIM), lambda b, h: (b, 0, qcol(h))),
                  pl.BlockSpec((1, C, HEAD_DIM), lambda b, h: (b, 0, kcol(h))),
                  pl.BlockSpec((1, C, HEAD_DIM), lambda b, h: (b, 0, vcol(h))),
                  pl.BlockSpec((1, 8, LANES), lambda b, h: (h, 0, 0))],
        out_specs=pl.BlockSpec((1, C, HEAD_DIM), lambda b, h: (b, 0, h)),
        compiler_params=_params("parallel", "parallel"),
        name="context_attention",
    )(qkv_c, qkv_c, qkv_c, sink_t)


def _wkv_pairs(ins, cum, sts, strict_s, incl_s, lane_lo, bd):
    C = ins[0][0].shape[0]
    P = range(len(ins))
    steps = int(math.log2(C))
    cat = jnp.concatenate

    def bdw(x):
        return jnp.where(bd, cat([x, x], axis=0), 0.0).astype(BF16)

    tot = [jnp.sum(ins[p][1], axis=0, keepdims=True) for p in P]
    g_inv = [jnp.exp(-cum[p]) for p in P]
    lhs = [cat([ins[p][4] * jnp.exp(cum[p] - ins[p][1]), ins[p][0] * jnp.exp(cum[p])], axis=0).astype(BF16)
           for p in P]
    b_t = [ins[p][5] * g_inv[p] for p in P]
    k_t = [ins[p][2] * g_inv[p] for p in P]
    rhs = [cat([jnp.where(lane_lo, b_t[p], 0.0), jnp.where(lane_lo, 0.0, b_t[p]),
                jnp.where(lane_lo, k_t[p], 0.0), jnp.where(lane_lo, 0.0, k_t[p])], axis=0).astype(BF16) for p in P]
    prod = [_nt(lhs[p], rhs[p]) for p in P]
    sh = [_nt(lhs[p], sts[p].astype(BF16)) for p in P]
    ns = [jnp.where(strict_s, prod[p][0:C, 0:2 * C], 0.0) for p in P]
    n_k = [cat([jnp.where(strict_s, prod[p][0:C, 2 * C:4 * C], 0.0),
                jnp.where(incl_s, prod[p][C:2 * C, 2 * C:4 * C], 0.0)], axis=0).astype(BF16) for p in P]
    n_rb = [jnp.where(incl_s, prod[p][C:2 * C, 0:2 * C], 0.0).astype(BF16) for p in P]
    kv = [_mm(n_k[p], bdw(ins[p][3])) for p in P]
    x = [sh[p][0:C] + kv[p][0:C] for p in P]
    for i in range(steps):
        nsb = [ns[p].astype(BF16) for p in P]
        if i + 1 < steps:
            z = [_mm(nsb[p], cat([bdw(x[p]), bdw(ns[p])], axis=1)) for p in P]
            x = [x[p] + z[p][:, 0:LANES] for p in P]
            ns = [z[p][:, LANES:2 * LANES] for p in P]
        else:
            x = [x[p] + _mm(nsb[p], bdw(x[p])) for p in P]
    y = [sh[p][C:2 * C] + kv[p][C:2 * C] + _mm(n_rb[p], bdw(x[p])) for p in P]
    uv_t = [cat([x[p], ins[p][3]], axis=0).T.astype(BF16) for p in P]
    g_tail = [jnp.exp(tot[p] - cum[p]) for p in P]
    bk = [cat([ins[p][5] * g_tail[p], ins[p][2] * g_tail[p]], axis=0).astype(BF16) for p in P]
    st_new = [sts[p] * jnp.exp(tot[p]) + jnp.where(bd, _mm(uv_t[p], bk[p]), 0.0) for p in P]
    return y, st_new


def _cumsum_time(x, reverse):
    C = x.shape[0]
    row = lax.broadcasted_iota(jnp.int32, x.shape, 0)
    s = 1
    while s < C:
        if reverse:
            x = x + jnp.where(row < C - s, pltpu.roll(x, C - s, 0), 0.0)
        else:
            x = x + jnp.where(row >= s, pltpu.roll(x, s, 0), 0.0)
        s *= 2
    return x


def _head_sum(x, lane_lo):
    lo = jnp.sum(jnp.where(lane_lo, x, 0.0), axis=-1, keepdims=True)
    hi = jnp.sum(jnp.where(lane_lo, 0.0, x), axis=-1, keepdims=True)
    return jnp.where(lane_lo, lo, hi)


_V_W0, _V_A0, _V_KK, _V_KA, _V_A0_OTHER, _V_RK, _V_LNW, _V_LNB = range(8)


def _wkv_kernel(*refs, reverse, pairs):
    C = WKV_CHUNK
    if reverse:
        (r_ref, k_ref, v_ref, wl_ref, al_ref, w2_ref, a2_ref, vec_ref, a2o_ref, yf_ref, g_ref,
         o_ref, st_ref) = refs
    else:
        r_ref, k_ref, v_ref, wl_ref, al_ref, w2_ref, a2_ref, vec_ref, o_ref, st_ref = refs

    @pl.when(pl.program_id(2) == 0)
    def _():
        st_ref[...] = jnp.zeros_like(st_ref)

    tt = lax.broadcasted_iota(jnp.int32, (C, 2 * C), 0)
    ss = lax.broadcasted_iota(jnp.int32, (C, 2 * C), 1) % C
    incl_s = (ss >= tt) if reverse else (ss <= tt)
    strict_s = (ss > tt) if reverse else (ss < tt)
    lane_lo = lax.broadcasted_iota(jnp.int32, (C, LANES), 1) < RW_HEAD
    bd = (lax.broadcasted_iota(jnp.int32, (LANES, LANES), 0) // RW_HEAD) == (
        lax.broadcasted_iota(jnp.int32, (LANES, LANES), 1) // RW_HEAD)
    sls = [slice(p * LANES, (p + 1) * LANES) for p in range(pairs)]
    vec = lambda row, sl: vec_ref[row:row + 1, sl]

    wl = jnp.tanh(wl_ref[0]).astype(BF16)
    al = al_ref[0].astype(BF16)
    w_raw = vec_ref[_V_W0:_V_W0 + 1, :] + _mm(wl, w2_ref[...])
    lw_all = -jnp.exp(-jax.nn.softplus(-w_raw) - 0.5)
    cum_all = _cumsum_time(lw_all, reverse)
    ag_all = jax.nn.sigmoid(vec_ref[_V_A0:_V_A0 + 1, :] + _mm(al, a2_ref[...]))
    if reverse:
        ag_o_all = jax.nn.sigmoid(vec_ref[_V_A0_OTHER:_V_A0_OTHER + 1, :] + _mm(al, a2o_ref[...]))
    ins = []
    for sl in sls:
        lw, ag = lw_all[:, sl], ag_all[:, sl]
        k_raw = k_ref[0, :, sl]
        kkf = k_raw * vec(_V_KK, sl)
        kk = kkf * lax.rsqrt(_head_sum(kkf * kkf, lane_lo) + 1e-12)
        kd = k_raw * (1.0 + (ag - 1.0) * vec(_V_KA, sl))
        ins.append((r_ref[0, :, sl], lw, kd, v_ref[0, :, sl], -kk, kk * ag))
    y, st_new = _wkv_pairs(ins, [cum_all[:, sl] for sl in sls], [st_ref[p] for p in range(pairs)],
                           strict_s, incl_s, lane_lo, bd)
    for p, sl in enumerate(sls):
        st_ref[p] = st_new[p]
        if not reverse:
            o_ref[0, :, sl] = y[p]
            continue
        r, _, kd, v, _, _ = ins[p]
        k_sum = kd + k_ref[0, :, sl] * (1.0 + (ag_o_all[:, sl] - 1.0) * vec(_V_KA, sl))
        bonus = _head_sum(r * k_sum * vec(_V_RK, sl), lane_lo) * v
        ysum = yf_ref[0, :, sl] + y[p]
        mean = _head_sum(ysum, lane_lo) * (1.0 / RW_HEAD)
        dev = ysum - mean
        var = _head_sum(dev * dev, lane_lo) * (1.0 / RW_HEAD)
        yn = dev * lax.rsqrt(var + RW_LNX_EPS) * vec(_V_LNW, sl) + vec(_V_LNB, sl)
        o_ref[0, :, sl] = ((yn + bonus) * g_ref[0, :, sl]).astype(o_ref.dtype)


def wkv_scan(proj, w2, a2, vecs, *, ctx_len, reverse, a2_other=None, y_fwd=None, g=None):
    B, T, _ = proj.shape
    D = vecs.shape[1]
    C = WKV_CHUNK
    L = T - 2 * ctx_len
    assert ctx_len % C == 0 and L % C == 0 and 2 * C == LANES
    nchunk = (ctx_len + L) // C
    cc = ctx_len // C
    nl = L // C
    pairs = min(WKV_PAIRS, D // LANES)
    width = pairs * LANES
    assert D % width == 0
    nd = D // width
    if reverse:
        tchunk = lambda ci: cc + nchunk - 1 - ci
    else:
        tchunk = lambda ci: ci
    lchunk = lambda ci: jnp.clip(tchunk(ci) - cc, 0, nl - 1)
    pspec = lambda m: pl.BlockSpec((1, C, width), lambda bi, di, ci: (bi, tchunk(ci), m * nd + di))
    hid0 = 3 * D // RW_TN
    hspec = lambda m: pl.BlockSpec((1, C, RW_TN), lambda bi, di, ci: (bi, tchunk(ci), hid0 + m))
    lspec = pl.BlockSpec((1, C, width), lambda bi, di, ci: (bi, lchunk(ci), di))
    wspec = lambda rows: pl.BlockSpec((rows, width), lambda bi, di, ci: (0, di))
    args = [proj, proj, proj, proj, proj, w2, a2, vecs]
    in_specs = [pspec(0), pspec(1), pspec(2), hspec(1), hspec(2), wspec(RW_TN), wspec(RW_TN),
                wspec(vecs.shape[0])]
    if reverse:
        args += [a2_other, y_fwd, g]
        in_specs += [wspec(RW_TN), lspec,
                     pl.BlockSpec((1, C, width), lambda bi, di, ci: (bi, tchunk(ci), di))]
    return pl.pallas_call(
        functools.partial(_wkv_kernel, reverse=reverse, pairs=pairs),
        out_shape=jax.ShapeDtypeStruct((B, L, D), BF16 if reverse else F32),
        grid=(B, nd, nchunk),
        in_specs=in_specs,
        out_specs=lspec,
        scratch_shapes=[pltpu.VMEM((pairs, LANES, LANES), F32)],
        compiler_params=_params("parallel", "parallel", "arbitrary"),
        name="wkv_scan_bwd" if reverse else "wkv_scan_fwd",
    )(*args)


def _moe_ffn_kernel(be_ref, nused_ref, x_ref, w1_ref, w3_ref, w2_ref, o_ref, w1b, w3b, w2b):
    i = pl.program_id(0)
    used = i < nused_ref[0]

    @pl.when(used & ((i == 0) | (be_ref[i] != be_ref[jnp.maximum(i - 1, 0)])))
    def _():
        w1b[...] = w1_ref[0, 0].astype(BF16)
        w3b[...] = w3_ref[0, 0].astype(BF16)
        w2b[...] = w2_ref[0, 0].astype(BF16)

    @pl.when(used)
    def _():
        x = x_ref[...]
        h1 = _mm(x, w1b[...])
        h3 = _mm(x, w3b[...])
        hh = (h1 * jax.nn.sigmoid(h1)) * h3
        o_ref[...] = _mm(hh.astype(BF16), w2b[...]).astype(o_ref.dtype)

    @pl.when(i >= nused_ref[0])
    def _():
        o_ref[...] = jnp.zeros_like(o_ref)


def moe_ffn(x_rows, blk_expert, n_used, w1, w3, w2, layer):
    slots, D = x_rows.shape
    F = w1.shape[3]
    nblk = slots // MOE_ROWS
    grid_spec = pltpu.PrefetchScalarGridSpec(
        num_scalar_prefetch=2,
        grid=(nblk,),
        in_specs=[pl.BlockSpec((MOE_ROWS, D), lambda i, be, nu: (i, 0)),
                  pl.BlockSpec((1, 1, D, F), lambda i, be, nu: (layer, be[i], 0, 0)),
                  pl.BlockSpec((1, 1, D, F), lambda i, be, nu: (layer, be[i], 0, 0)),
                  pl.BlockSpec((1, 1, F, D), lambda i, be, nu: (layer, be[i], 0, 0))],
        out_specs=pl.BlockSpec((MOE_ROWS, D), lambda i, be, nu: (i, 0)),
        scratch_shapes=[pltpu.VMEM((D, F), BF16), pltpu.VMEM((D, F), BF16), pltpu.VMEM((F, D), BF16)],
    )
    return pl.pallas_call(
        _moe_ffn_kernel,
        out_shape=jax.ShapeDtypeStruct((slots, D), BF16),
        grid_spec=grid_spec,
        compiler_params=_params("arbitrary"),
        name="moe_ffn",
    )(blk_expert, n_used, x_rows, w1.astype(F32), w3.astype(F32), w2.astype(F32))


def hierarchical_moe(route, hb, w1, w3, w2, layer):
    n, D = hb.shape
    p_grp = jax.nn.softmax(route[:, :N_GROUPS], axis=-1)
    g_sel = jnp.argmax(p_grp, axis=-1)
    p_sel = jnp.take_along_axis(p_grp, g_sel[:, None], axis=-1)
    logits = route[:, N_GROUPS:N_GROUPS + N_EXPERTS].reshape(n, N_GROUPS, EXPERTS_PER_GROUP)
    logits_g = jnp.take_along_axis(logits, g_sel[:, None, None], axis=1)[:, 0]
    top_val, top_idx = lax.top_k(logits_g, TOP_K)
    gate = p_sel * jax.nn.softmax(top_val, axis=-1)
    expert = (g_sel[:, None] * EXPERTS_PER_GROUP + top_idx).astype(jnp.int32)
    flat_e = expert.reshape(-1)
    onehot = (flat_e[:, None] == jnp.arange(N_EXPERTS, dtype=jnp.int32)[None, :]).astype(jnp.int32)
    csum = jnp.cumsum(onehot, axis=0)
    rank = jnp.take_along_axis(csum, flat_e[:, None], axis=1)[:, 0] - 1
    counts = csum[-1]
    padded = (counts + MOE_ROWS - 1) // MOE_ROWS * MOE_ROWS
    pad_end = jnp.cumsum(padded)
    dest = (pad_end - padded)[flat_e] + rank
    nblk = (n * TOP_K + N_EXPERTS * (MOE_ROWS - 1) + MOE_ROWS - 1) // MOE_ROWS
    slots = nblk * MOE_ROWS
    flat_tok = jnp.repeat(jnp.arange(n, dtype=jnp.int32), TOP_K)
    slot_tok = jnp.zeros((slots,), jnp.int32).at[dest].set(flat_tok)
    blk_start = jnp.arange(nblk, dtype=jnp.int32) * MOE_ROWS
    blk_expert = jnp.minimum(jnp.sum((pad_end[None, :] <= blk_start[:, None]).astype(jnp.int32), axis=1),
                             N_EXPERTS - 1).astype(jnp.int32)
    n_used = (pad_end[-1] // MOE_ROWS).astype(jnp.int32).reshape(1)
    x_rows = hb[slot_tok]
    y = moe_ffn(x_rows, blk_expert, n_used, w1, w3, w2, layer)
    d2 = dest.reshape(n, TOP_K)
    return y[d2[:, 0]], y[d2[:, 1]], gate


def _moe_combine_kernel(ya_ref, yb_ref, gate_ref, res_ref, gt_ref, o_ref):
    gate = gate_ref[...]
    f = ya_ref[...].astype(F32) * gate[:, 0:1] + yb_ref[...].astype(F32) * gate[:, 1:2]
    o_ref[...] = res_ref[...] + gt_ref[0] * f


def moe_combine(ya, yb, gate, res, gt, row0, tm=512):
    M, D = res.shape
    tm = min(tm, M)
    nb = gt.shape[0]
    rpb = M // nb
    assert M % tm == 0 and row0 % tm == 0 and rpb % tm == 0
    r0 = row0 // tm
    row_spec = lambda w: pl.BlockSpec((tm, w), lambda i: (r0 + i, 0))
    return pl.pallas_call(
        _moe_combine_kernel,
        out_shape=jax.ShapeDtypeStruct((M, D), F32),
        grid=(M // tm,),
        in_specs=[row_spec(D), row_spec(D), row_spec(TOP_K),
                  pl.BlockSpec((tm, D), lambda i: (i, 0)),
                  pl.BlockSpec((1, 1, D), lambda i: ((i * tm) // rpb, 0, 0))],
        out_specs=pl.BlockSpec((tm, D), lambda i: (i, 0)),
        compiler_params=_params("parallel"),
        name="moe_combine",
    )(ya, yb, gate.astype(F32), res, gt.reshape(nb, 1, D).astype(F32))


def moe_router(x, norm_g, sc, sh, w_grp, w_exp):
    w_route = jnp.concatenate([w_grp, w_exp], axis=1).astype(F32)
    w_route = jnp.pad(w_route, ((0, 0), (0, LANES - w_route.shape[1])))
    return linear(x, w_route, norm=(norm_g, sc, sh), precise=True, emit_h=True)


def attention_layer(x, xc, norm_g, sc, sh, csc, csh, w_in, w_out, a_q_gain, a_k_gain, a_sink, b_q_gain, b_k_gain,
                    b_rpb, gt, cgt):
    B, L, D = x.shape
    C = xc.shape[1]
    gains = (a_q_gain, a_k_gain, b_q_gain, b_k_gain)
    qkv = qkv_projection(x.reshape(B * L, D), w_in, norm_g, sc, sh, gains, L).reshape(B, L, -1)
    qkv_c = qkv_projection(xc.reshape(B * C, D), w_in, norm_g, csc, csh, gains, None,
                           tm=min(512, B * C)).reshape(B, C, -1)
    ya = window_attention(qkv, qkv_c, a_sink)
    yb = neighbourhood_attention(qkv, qkv_c, b_rpb)
    x_new = linear(ya.reshape(B * L, -1), w_out, x2=yb.reshape(B * L, -1),
                   gated=(x.reshape(B * L, D), gt)).reshape(B, L, D)
    mix_c = context_attention(qkv_c, a_sink)
    xc_new = linear(mix_c.reshape(B * C, -1), w_out, gated=(xc.reshape(B * C, D), cgt)).reshape(B, C, D)
    return x_new, xc_new


def rwkv_layer(x, xc, norm_g, sc, sh, csc, csh, gt, mu, wr, wk, wv, wo, w0, w1, w2, a0, a1, a2, g1, g2, k_k, k_a,
               r_k, lnx_w, lnx_b):
    B, L, D = x.shape
    C = xc.shape[1]
    T = L + 2 * C

    xcat = jnp.concatenate([xc, x, xc], axis=1).reshape(B * T, D)
    nl, na = w1.shape[2], a1.shape[2]
    assert 2 * nl <= RW_TN and 2 * na <= RW_TN and g1.shape[1] <= RW_TN
    proj = rwkv_projections(xcat, norm_g, sc, sh, csc, csh, mu,
                            (wr, wk, wv, g1, jnp.concatenate([w1[0], w1[1]], axis=1),
                             jnp.concatenate([a1[0], a1[1]], axis=1)), T, C)
    rows = lambda w, lo: jnp.pad(w, ((lo, RW_TN - lo - w.shape[0]), (0, 0))).astype(BF16)
    g = linear(proj, rows(g2, 0), act="sigmoid", x_block=(3 * D // RW_TN, RW_TN)).reshape(B, T, D)
    w2p = [rows(w2[0], 0), rows(w2[1], nl)]
    a2p = [rows(a2[0], 0), rows(a2[1], na)]
    vecs = lambda d: jnp.stack([w0[d], a0[d], k_k, k_a, a0[1 - d], r_k.reshape(D), lnx_w, lnx_b]).astype(F32)
    proj = proj.reshape(B, T, -1)
    y_fwd = wkv_scan(proj, w2p[0], a2p[0], vecs(0), ctx_len=C, reverse=False)
    z = wkv_scan(proj, w2p[1], a2p[1], vecs(1), ctx_len=C, reverse=True, a2_other=a2p[0], y_fwd=y_fwd, g=g)
    return linear(z.reshape(B * L, D), wo, gated=(x.reshape(B * L, D), gt)).reshape(B, L, D)


def kernel(x, c, ctx, c_ctx, ada_w, ada_b, norm_mix_g, norm_ffn_g, attn_w_in, attn_w_out, a_q_gain, a_k_gain, a_sink, b_q_gain, b_k_gain, b_rpb, rw_mu, rw_wr, rw_wk, rw_wv, rw_wo, rw_w0, rw_w1, rw_w2, rw_a0, rw_a1, rw_a2, rw_g1, rw_g2, rw_k_k, rw_k_a, rw_r_k, rw_lnx_w, rw_lnx_b, moe_w_grp, moe_w_exp, moe_w1, moe_w3, moe_w2):
    B, L, D = x.shape
    C = ctx.shape[1]
    depth = ada_w.shape[0]
    assert depth == 2, "layer 0 is the attention mixer, layer 1 (last) the RWKV-7 mixer"
    pad = (-(B + 1)) % 8
    cond = jnp.concatenate([jax.nn.silu(c), jax.nn.silu(c_ctx)[None, :], jnp.zeros((pad, D), F32)], axis=0)
    xc = ctx
    for i in range(depth):
        last = i == depth - 1
        j = i // 2
        mod = linear(cond, ada_w, precise=True, layer=i) + ada_b[i]
        sh1, sc1, gt1, sh2, sc2, gt2 = jnp.split(mod[:B], 6, axis=-1)
        csh1, csc1, cgt1, csh2, csc2, cgt2 = jnp.split(mod[B:B + 1], 6, axis=-1)
        if i % 2 == 0:
            x, xc = attention_layer(x, xc, norm_mix_g[i], sc1, sh1, csc1, csh1, attn_w_in[j], attn_w_out[j],
                                    a_q_gain[j], a_k_gain[j], a_sink[j], b_q_gain[j], b_k_gain[j], b_rpb[j],
                                    gt1, cgt1)
        else:
            x = rwkv_layer(x, xc, norm_mix_g[i], sc1, sh1, csc1, csh1, gt1, rw_mu[j], rw_wr[j], rw_wk[j], rw_wv[j],
                           rw_wo[j], rw_w0[j], rw_w1[j], rw_w2[j], rw_a0[j], rw_a1[j], rw_a2[j], rw_g1[j], rw_g2[j],
                           rw_k_k[j], rw_k_a[j], rw_r_k[j], rw_lnx_w[j], rw_lnx_b[j])
        route, hb = moe_router(x.reshape(B * L, D), norm_ffn_g[i], sc2, sh2, moe_w_grp[i], moe_w_exp[i])
        if last:
            ya, yb, gate = hierarchical_moe(route, hb, moe_w1, moe_w3, moe_w2, i)
            x = moe_combine(ya, yb, gate, x.reshape(B * L, D), gt2, 0).reshape(B, L, D)
        else:
            route_c, hb_c = moe_router(xc.reshape(B * C, D), norm_ffn_g[i], csc2, csh2, moe_w_grp[i], moe_w_exp[i])
            ya, yb, gate = hierarchical_moe(jnp.concatenate([route, route_c], axis=0),
                                            jnp.concatenate([hb, hb_c], axis=0), moe_w1, moe_w3, moe_w2, i)
            x = moe_combine(ya, yb, gate, x.reshape(B * L, D), gt2, 0).reshape(B, L, D)
            xc = moe_combine(ya, yb, gate, xc.reshape(B * C, D), cgt2, B * L).reshape(B, C, D)
    return x
```

```python
import functools
import math

import numpy as np
import jax
import jax.numpy as jnp
from jax import lax
from jax.experimental import pallas as pl
from jax.experimental.pallas import tpu as pltpu

GRID_W = 64
HEAD_DIM = 128
A_HEADS = 8
A_KV_HEADS = 2
A_BLOCK = 128
B_HEADS = 8
NA_ROWS = 8
NA_COLS = 16
ROPE_THETA = 10000.0
A_Q_COLS = A_HEADS * HEAD_DIM
A_KV_COLS = A_KV_HEADS * HEAD_DIM
B_COLS = B_HEADS * HEAD_DIM
QKV_COLS = A_Q_COLS + 2 * A_KV_COLS + 3 * B_COLS
MIX_WIDTH = A_Q_COLS + B_COLS
RW_HEAD = 64
RW_LNX_EPS = 64e-5
N_GROUPS = 4
EXPERTS_PER_GROUP = 8
N_EXPERTS = N_GROUPS * EXPERTS_PER_GROUP
TOP_K = 2
NORM_EPS = 1e-6
NEG_INF = -1e30

LANES = 128
VMEM_LIMIT_BYTES = 48 * 1024 * 1024
MOE_ROWS = 256
NB_ROWS = 8
WKV_CHUNK = 64
WKV_PAIRS = 16
QKV_TN = 4 * HEAD_DIM
ROW_TILE = 1024

QA_BLK = 0
KA_BLK = A_Q_COLS // HEAD_DIM
VA_BLK = KA_BLK + A_KV_HEADS
QB_BLK = VA_BLK + A_KV_HEADS
KB_BLK = QB_BLK + B_HEADS
VB_BLK = KB_BLK + B_HEADS

BF16 = jnp.bfloat16
F32 = jnp.float32


def _params(*sem):
    return pltpu.CompilerParams(dimension_semantics=sem, vmem_limit_bytes=VMEM_LIMIT_BYTES)


def _nt(a, b):
    return lax.dot_general(a, b, (((1,), (1,)), ((), ())), preferred_element_type=F32)


def _mm(a, b):
    return jnp.dot(a, b, preferred_element_type=F32)


def _row_tile(want, group_sizes):
    t = want
    while t > 8 and any(g % t for g in group_sizes):
        t //= 2
    assert all(g % t == 0 for g in group_sizes)
    return t


def _rms_modulate(x, g, sc, sh):
    y = x * lax.rsqrt(jnp.mean(x * x, axis=-1, keepdims=True) + NORM_EPS)
    return (y * g) * (1.0 + sc) + sh


def _linear_kernel(*refs, mode, gated, precise, emit_h, layered):
    it = iter(refs)
    x_ref = next(it)
    if mode == "norm":
        g_ref, sc_ref, sh_ref = next(it), next(it), next(it)
    elif mode == "cat2":
        x2_ref = next(it)
    w_ref = next(it)
    if gated:
        res_ref, gate_ref = next(it), next(it)
    o_ref = next(it)
    if emit_h:
        hb_ref = next(it)
    h_ref = next(it)

    @pl.when(pl.program_id(1) == 0)
    def _():
        if mode == "cat2":
            k1 = x_ref.shape[1]
            h_ref[:, 0:k1] = x_ref[...].astype(h_ref.dtype)
            h_ref[:, k1:] = x2_ref[...].astype(h_ref.dtype)
            return
        x = x_ref[...].astype(F32)
        if mode == "norm":
            x = _rms_modulate(x, g_ref[...], sc_ref[0], sh_ref[0])
        elif mode == "sigmoid":
            x = jax.nn.sigmoid(x)
        h_ref[...] = x.astype(h_ref.dtype)
        if emit_h:
            hb_ref[...] = x.astype(hb_ref.dtype)

    w = w_ref[0] if layered else w_ref[...]
    if precise:
        acc = jnp.dot(h_ref[...], w, preferred_element_type=F32, precision=lax.Precision.HIGHEST)
    else:
        acc = _mm(h_ref[...], w)
    if gated:
        acc = res_ref[...] + gate_ref[0] * acc
    o_ref[...] = acc.astype(o_ref.dtype)


def linear(x, w, *, norm=None, x2=None, act=None, gated=None, out_dtype=F32, precise=False, emit_h=False,
           layer=None, x_block=None, tm=ROW_TILE, tn=512):
    M = x.shape[0]
    K = x.shape[1] if x_block is None else x_block[1]
    xcol = 0 if x_block is None else x_block[0]
    N = w.shape[-1]
    groups = [M] + ([M // norm[1].shape[0]] if norm is not None else []) + (
        [M // gated[1].shape[0]] if gated is not None else [])
    tm = _row_tile(tm, groups)
    tn = tn if N % tn == 0 else N
    assert N % tn == 0
    mode = "norm" if norm is not None else "cat2" if x2 is not None else (act or "plain")
    wdt = F32 if precise else BF16
    args = [x]
    in_specs = [pl.BlockSpec((tm, K), lambda i, j: (i, xcol))]
    vec = lambda t: t.reshape(1, -1).astype(F32)
    vec_spec = lambda k: pl.BlockSpec((1, k), lambda i, j: (0, 0))
    if mode == "norm":
        g, sc, sh = norm
        nb = sc.shape[0]
        rpb = M // nb
        assert rpb % tm == 0
        args += [vec(g), sc.reshape(nb, 1, K).astype(F32), sh.reshape(nb, 1, K).astype(F32)]
        in_specs += [vec_spec(K),
                     pl.BlockSpec((1, 1, K), lambda i, j: ((i * tm) // rpb, 0, 0)),
                     pl.BlockSpec((1, 1, K), lambda i, j: ((i * tm) // rpb, 0, 0))]
    elif mode == "cat2":
        args.append(x2)
        in_specs.append(pl.BlockSpec((tm, x2.shape[1]), lambda i, j: (i, 0)))
        K = K + x2.shape[1]
    assert w.shape[-2] == K
    args.append(w.astype(wdt))
    if layer is None:
        in_specs.append(pl.BlockSpec((K, tn), lambda i, j: (0, j)))
    else:
        in_specs.append(pl.BlockSpec((1, K, tn), lambda i, j: (layer, 0, j)))
    if gated is not None:
        res, gate = gated
        nbg = gate.shape[0]
        rpg = M // nbg
        assert rpg % tm == 0
        args += [res, gate.reshape(nbg, 1, N).astype(F32)]
        in_specs += [pl.BlockSpec((tm, tn), lambda i, j: (i, j)),
                     pl.BlockSpec((1, 1, tn), lambda i, j: ((i * tm) // rpg, 0, j))]
    out_shape = jax.ShapeDtypeStruct((M, N), out_dtype)
    out_specs = pl.BlockSpec((tm, tn), lambda i, j: (i, j))
    if emit_h:
        out_shape = (out_shape, jax.ShapeDtypeStruct((M, K), BF16))
        out_specs = (out_specs, pl.BlockSpec((tm, K), lambda i, j: (i, 0)))
    return pl.pallas_call(
        functools.partial(_linear_kernel, mode=mode, gated=gated is not None, precise=precise, emit_h=emit_h,
                          layered=layer is not None),
        out_shape=out_shape,
        grid=(M // tm, N // tn),
        in_specs=in_specs,
        out_specs=out_specs,
        scratch_shapes=[pltpu.VMEM((tm, K), wdt)],
        compiler_params=_params("parallel", "arbitrary"),
        name="linear_" + mode,
    )(*args)


RW_TN = 512


def _rwkv_proj_kernel(x_ref, prev_ref, next_ref, g_ref, sc_ref, sh_ref, csc_ref, csh_ref, mu_ref, w_ref,
                      o_ref, hs_ref, xs_ref, h_ref, *, t_len, c_len, wide_tiles):
    j = pl.program_id(1)
    tm = x_ref.shape[0]

    @pl.when(j == 0)
    def _():
        t0 = (pl.program_id(0) * tm) % t_len

        def hmod(rows, t):
            is_ctx = (t < c_len) | (t >= t_len - c_len)
            return _rms_modulate(rows, g_ref[...], jnp.where(is_ctx, csc_ref[...], sc_ref[0]),
                                 jnp.where(is_ctx, csh_ref[...], sh_ref[0]))

        row = lax.broadcasted_iota(jnp.int32, (tm, 1), 0)
        row8 = lax.broadcasted_iota(jnp.int32, (8, 1), 0)
        t = t0 + row
        h = hmod(x_ref[...].astype(F32), t)
        h_before = hmod(prev_ref[...].astype(F32), t0 - 8 + row8)[7:8]
        h_after = hmod(next_ref[...].astype(F32), t0 + tm + row8)[0:1]
        up = jnp.where(row == 0, h_before, pltpu.roll(h, 1, 0))
        dn = jnp.where(row == tm - 1, h_after, pltpu.roll(h, tm - 1, 0))
        has_prev = (t != 0) & (t != c_len) & (t != t_len - c_len)
        has_next = (t != c_len - 1) & (t != t_len - c_len - 1) & (t != t_len - 1)
        hs_ref[...] = h
        xs_ref[...] = 0.5 * (jnp.where(has_prev, up, 0.0) + jnp.where(has_next, dn, 0.0)) - h

    group = jnp.where(j < 3 * wide_tiles, j // wide_tiles, j - 3 * wide_tiles + 3)

    @pl.when((j % wide_tiles == 0) | (j >= 3 * wide_tiles))
    def _():
        h_ref[...] = (hs_ref[...] + xs_ref[...] * mu_ref[pl.ds(group, 1), :]).astype(h_ref.dtype)

    o_ref[...] = _mm(h_ref[...], w_ref[...])


def rwkv_projections(xcat, norm_g, sc, sh, csc, csh, mu, weights, t_len, c_len, tm=512):
    M, K = xcat.shape
    nb = sc.shape[0]
    assert M == nb * t_len and t_len % tm == 0 and tm % 8 == 0 and K % RW_TN == 0
    wr, wk, wv, g1, w1c, a1c = weights
    padw = lambda w: jnp.pad(w, ((0, 0), (0, RW_TN - w.shape[1])))
    w_all = jnp.concatenate([wr, wk, wv, padw(g1), padw(w1c), padw(a1c)], axis=1).astype(BF16)
    mu_g = jnp.pad(mu[jnp.array([0, 2, 3, 5, 1, 4])].astype(F32), ((0, 2), (0, 0)))
    n_out = w_all.shape[1]
    per = tm // 8
    vec = lambda t: t.reshape(1, -1).astype(F32)
    vec_spec = pl.BlockSpec((1, K), lambda i, j: (0, 0))
    mod_spec = pl.BlockSpec((1, 1, K), lambda i, j: ((i * tm) // t_len, 0, 0))
    return pl.pallas_call(
        functools.partial(_rwkv_proj_kernel, t_len=t_len, c_len=c_len, wide_tiles=K // RW_TN),
        out_shape=jax.ShapeDtypeStruct((M, n_out), F32),
        grid=(M // tm, n_out // RW_TN),
        in_specs=[pl.BlockSpec((tm, K), lambda i, j: (i, 0)),
                  pl.BlockSpec((8, K), lambda i, j: (jnp.maximum(i * per - 1, 0), 0)),
                  pl.BlockSpec((8, K), lambda i, j: (jnp.minimum((i + 1) * per, M // 8 - 1), 0)),
                  vec_spec, mod_spec, mod_spec, vec_spec, vec_spec,
                  pl.BlockSpec((8, K), lambda i, j: (0, 0)),
                  pl.BlockSpec((K, RW_TN), lambda i, j: (0, j))],
        out_specs=pl.BlockSpec((tm, RW_TN), lambda i, j: (i, j)),
        scratch_shapes=[pltpu.VMEM((tm, K), F32), pltpu.VMEM((tm, K), F32), pltpu.VMEM((tm, K), BF16)],
        compiler_params=_params("parallel", "arbitrary"),
        name="rwkv_projections",
    )(xcat, xcat, xcat, vec(norm_g), sc.reshape(nb, 1, K).astype(F32), sh.reshape(nb, 1, K).astype(F32),
      vec(csc), vec(csh), mu_g, w_all)


def _qkv_kernel(x_ref, g_ref, sc_ref, sh_ref, w_ref, gain_ref, cos_ref, sin_ref, o_ref, h_ref, *, rope):
    j = pl.program_id(1)

    @pl.when(j == 0)
    def _():
        h_ref[...] = _rms_modulate(x_ref[...].astype(F32), g_ref[...], sc_ref[0], sh_ref[0]).astype(h_ref.dtype)

    acc = _mm(h_ref[...], w_ref[...])
    tm = acc.shape[0]

    def emit(kinds):
        for hh, kind in enumerate(kinds):
            sl = slice(hh * HEAD_DIM, (hh + 1) * HEAD_DIM)
            y = acc[:, sl]
            if kind != "plain":
                y = y * lax.rsqrt(jnp.mean(y * y, axis=-1, keepdims=True) + NORM_EPS) * gain_ref[:, sl]
                if kind == "rope" and rope:
                    lane = lax.broadcasted_iota(jnp.int32, (tm, HEAD_DIM), 1)
                    first = (lane % (HEAD_DIM // 2)) < HEAD_DIM // 4
                    partner = jnp.where(first, pltpu.roll(y, HEAD_DIM - HEAD_DIM // 4, 1),
                                        pltpu.roll(y, HEAD_DIM // 4, 1))
                    y = y * cos_ref[...] + partner * sin_ref[...]
            o_ref[:, sl] = y.astype(o_ref.dtype)

    per = QKV_TN // HEAD_DIM
    kinds = (["rope"] * A_HEADS + ["rope"] * A_KV_HEADS + ["plain"] * A_KV_HEADS + ["norm"] * B_HEADS
             + ["norm"] * B_HEADS + ["plain"] * B_HEADS)
    tiles = [tuple(kinds[t * per:(t + 1) * per]) for t in range(len(kinds) // per)]
    for kind_set in sorted(set(tiles)):
        cond = None
        for t, ks in enumerate(tiles):
            if ks == kind_set:
                cond = (j == t) if cond is None else (cond | (j == t))
        pl.when(cond)(functools.partial(emit, kind_set))


def qkv_projection(x, w_in, norm_g, sc, sh, gains, rope_len):
    M, K = x.shape
    N = w_in.shape[1]
    assert N == QKV_COLS and N % QKV_TN == 0
    nb = sc.shape[0]
    rpb = M // nb
    tm = _row_tile(ROW_TILE, [rpb] + ([rope_len] if rope_len is not None else []))
    a_q, a_k, b_q, b_k = (t.astype(F32) for t in gains)
    ones = jnp.ones((HEAD_DIM,), F32)
    gain = jnp.concatenate([jnp.tile(a_q, A_HEADS), jnp.tile(a_k, A_KV_HEADS), jnp.tile(ones, A_KV_HEADS),
                            jnp.tile(b_q, B_HEADS), jnp.tile(b_k, B_HEADS), jnp.tile(ones, B_HEADS)]).reshape(1, N)
    rope = rope_len is not None
    if rope:
        assert rope_len % tm == 0
        t = jnp.arange(rope_len)
        quarter = HEAD_DIM // 4
        inv_freq = ROPE_THETA ** (-jnp.arange(quarter, dtype=F32) / quarter)
        ar = (t // GRID_W).astype(F32)[:, None] * inv_freq[None, :]
        ac = (t % GRID_W).astype(F32)[:, None] * inv_freq[None, :]
        cos_t = jnp.concatenate([jnp.cos(ar), jnp.cos(ar), jnp.cos(ac), jnp.cos(ac)], axis=1)
        sin_t = jnp.concatenate([-jnp.sin(ar), jnp.sin(ar), -jnp.sin(ac), jnp.sin(ac)], axis=1)
        nt = rope_len // tm
    else:
        cos_t = jnp.ones((tm, HEAD_DIM), F32)
        sin_t = jnp.zeros((tm, HEAD_DIM), F32)
        nt = 1
    tab_spec = pl.BlockSpec((tm, HEAD_DIM), lambda i, j: (i % nt, 0))
    return pl.pallas_call(
        functools.partial(_qkv_kernel, rope=rope),
        out_shape=jax.ShapeDtypeStruct((M, N), BF16),
        grid=(M // tm, N // QKV_TN),
        in_specs=[pl.BlockSpec((tm, K), lambda i, j: (i, 0)),
                  pl.BlockSpec((1, K), lambda i, j: (0, 0)),
                  pl.BlockSpec((1, 1, K), lambda i, j: ((i * tm) // rpb, 0, 0)),
                  pl.BlockSpec((1, 1, K), lambda i, j: ((i * tm) // rpb, 0, 0)),
                  pl.BlockSpec((K, QKV_TN), lambda i, j: (0, j)),
                  pl.BlockSpec((1, QKV_TN), lambda i, j: (0, j)),
                  tab_spec, tab_spec],
        out_specs=pl.BlockSpec((tm, QKV_TN), lambda i, j: (i, j)),
        scratch_shapes=[pltpu.VMEM((tm, K), BF16)],
        compiler_params=_params("parallel", "arbitrary"),
        name="qkv_projection",
    )(x, norm_g.reshape(1, K).astype(F32), sc.reshape(nb, 1, K).astype(F32), sh.reshape(nb, 1, K).astype(F32),
      w_in.astype(BF16), gain, cos_t, sin_t)


def _win_attn_kernel(q_ref, kp_ref, kc_ref, kn_ref, vp_ref, vc_ref, vn_ref, kx_ref, vx_ref, sink_ref,
                     o_ref, *, nblk, groups, scale):
    n = pl.program_id(2)
    blk = A_BLOCK
    q = q_ref[0]
    qs = jnp.concatenate([q[:, g * HEAD_DIM:(g + 1) * HEAD_DIM] for g in range(groups)], axis=0)
    rows = groups * blk
    keys = jnp.concatenate([kp_ref[0], kc_ref[0], kn_ref[0], kx_ref[0]], axis=0)
    vals = jnp.concatenate([vp_ref[0], vc_ref[0], vn_ref[0], vx_ref[0]], axis=0)
    nkeys = keys.shape[0]
    s = _nt(qs, keys) * scale
    qi = lax.broadcasted_iota(jnp.int32, (rows, nkeys), 0) % blk
    kj = lax.broadcasted_iota(jnp.int32, (rows, nkeys), 1)
    ok = (((kj >= blk) | ((kj >= qi) & (n > 0)))
          & ((kj < 2 * blk) | (kj >= 3 * blk) | ((kj - 2 * blk <= qi) & (n < nblk - 1))))
    s = jnp.where(ok, s, NEG_INF)
    sink = sink_ref[0][:, :1]
    m = jnp.maximum(jnp.max(s, axis=-1, keepdims=True), sink)
    p = jnp.exp(s - m)
    den = jnp.sum(p, axis=-1, keepdims=True) + jnp.exp(sink - m)
    o = _mm(p.astype(BF16), vals) / den
    for g in range(groups):
        o_ref[0, :, g * HEAD_DIM:(g + 1) * HEAD_DIM] = o[g * blk:(g + 1) * blk].astype(o_ref.dtype)


def window_attention(qkv, qkv_c, sink):
    B, L, _ = qkv.shape
    C = qkv_c.shape[1]
    groups = A_HEADS // A_KV_HEADS
    nblk = L // A_BLOCK
    gw = groups * HEAD_DIM
    sink_t = jnp.broadcast_to(
        jnp.repeat(sink.astype(F32).reshape(A_KV_HEADS, groups), A_BLOCK, axis=1)[:, :, None],
        (A_KV_HEADS, groups * A_BLOCK, LANES))
    blkspec = lambda f: pl.BlockSpec((1, A_BLOCK, HEAD_DIM), f)
    prev = lambda c0: (lambda b, h, n: (b, jnp.maximum(n - 1, 0), c0 + h))
    cur = lambda c0: (lambda b, h, n: (b, n, c0 + h))
    nxt = lambda c0: (lambda b, h, n: (b, jnp.minimum(n + 1, nblk - 1), c0 + h))
    ctx = lambda c0: pl.BlockSpec((1, C, HEAD_DIM), lambda b, h, n: (b, 0, c0 + h))
    return pl.pallas_call(
        functools.partial(_win_attn_kernel, nblk=nblk, groups=groups, scale=HEAD_DIM ** -0.5),
        out_shape=jax.ShapeDtypeStruct((B, L, A_Q_COLS), BF16),
        grid=(B, A_KV_HEADS, nblk),
        in_specs=[pl.BlockSpec((1, A_BLOCK, gw), lambda b, h, n: (b, n, h)),
                  blkspec(prev(KA_BLK)), blkspec(cur(KA_BLK)), blkspec(nxt(KA_BLK)),
                  blkspec(prev(VA_BLK)), blkspec(cur(VA_BLK)), blkspec(nxt(VA_BLK)),
                  ctx(KA_BLK), ctx(VA_BLK),
                  pl.BlockSpec((1, groups * A_BLOCK, LANES), lambda b, h, n: (h, 0, 0))],
        out_specs=pl.BlockSpec((1, A_BLOCK, gw), lambda b, h, n: (b, n, h)),
        compiler_params=_params("parallel", "parallel", "arbitrary"),
        name="window_attention",
    )(qkv, qkv, qkv, qkv, qkv, qkv, qkv, qkv_c, qkv_c, sink_t)


def _nbr_attn_kernel(q_ref, kp_ref, kc_ref, kn_ref, vp_ref, vc_ref, vn_ref, kx_ref, vx_ref, bias_ref,
                     o_ref, kbuf, vbuf, *, grid_rows, scale):
    rb = pl.program_id(2)
    tok = NB_ROWS * GRID_W
    nk = NA_ROWS * GRID_W
    nctx = kx_ref.shape[1]
    kbuf[0:tok] = kp_ref[0]
    kbuf[tok:2 * tok] = kc_ref[0]
    kbuf[2 * tok:3 * tok] = kn_ref[0]
    vbuf[0:tok] = vp_ref[0]
    vbuf[tok:2 * tok] = vc_ref[0]
    vbuf[2 * tok:3 * tok] = vn_ref[0]
    kx = kx_ref[0]
    vx = vx_ref[0]
    sx_all = _nt(q_ref[0], kx) * scale
    R = range(NB_ROWS)
    rsl = [slice(rr * GRID_W, (rr + 1) * GRID_W) for rr in R]
    r = [rb * NB_ROWS + rr for rr in R]
    r0 = [jnp.clip(r[rr] - NA_ROWS // 2, 0, grid_rows - NA_ROWS) for rr in R]
    off = [pl.multiple_of((r0[rr] - (rb - 1) * NB_ROWS) * GRID_W, GRID_W) for rr in R]
    s = [jnp.concatenate([_nt(q_ref[0, rsl[rr], :], kbuf[pl.ds(off[rr], nk), :]) * scale
                          + bias_ref[0, r0[rr] - r[rr] + NA_ROWS - 1], sx_all[rsl[rr]]], axis=1)
         for rr in R]
    m = [jnp.max(s[rr], axis=-1, keepdims=True) for rr in R]
    p = [jnp.exp(s[rr] - m[rr]) for rr in R]
    den = [jnp.sum(p[rr], axis=-1, keepdims=True) for rr in R]
    pb = [p[rr].astype(BF16) for rr in R]
    o = [(_mm(pb[rr][:, 0:nk], vbuf[pl.ds(off[rr], nk), :]) + _mm(pb[rr][:, nk:nk + nctx], vx)) / den[rr]
         for rr in R]
    for rr in R:
        o_ref[0, rsl[rr], :] = o[rr].astype(o_ref.dtype)


def _nbr_bias_table(rpb):
    col = np.arange(GRID_W)
    c0 = np.clip(col - NA_COLS // 2, 0, GRID_W - NA_COLS)
    col_ok = (col[None, :] >= c0[:, None]) & (col[None, :] < c0[:, None] + NA_COLS)
    dc = np.clip(col[None, :] - col[:, None] + (NA_COLS - 1), 0, 2 * NA_COLS - 2)
    dr = np.arange(NA_ROWS)[:, None] + np.arange(NA_ROWS)[None, :]
    sel_r = np.eye(2 * NA_ROWS - 1, dtype=np.float32)[dr]
    sel_c = np.eye(2 * NA_COLS - 1, dtype=np.float32)[dc]
    t = jnp.einsum('djr,hrc->hdjc', sel_r, rpb.astype(F32), precision=lax.Precision.HIGHEST)
    t = jnp.einsum('hdjc,qkc->hdqjk', t, sel_c, precision=lax.Precision.HIGHEST)
    t = jnp.where(col_ok[None, None, :, None, :], t, NEG_INF)
    return t.reshape(rpb.shape[0], NA_ROWS, GRID_W, NA_ROWS * GRID_W)


def neighbourhood_attention(qkv, qkv_c, rpb):
    B, L, _ = qkv.shape
    C = qkv_c.shape[1]
    grid_rows = L // GRID_W
    tok = NB_ROWS * GRID_W
    assert grid_rows >= NA_ROWS and L % tok == 0 and NB_ROWS == NA_ROWS
    nrb = L // tok
    bias = _nbr_bias_table(rpb)
    spec = lambda f: pl.BlockSpec((1, tok, HEAD_DIM), f)
    prev = lambda c0: (lambda b, h, n: (b, jnp.maximum(n - 1, 0), c0 + h))
    cur = lambda c0: (lambda b, h, n: (b, n, c0 + h))
    nxt = lambda c0: (lambda b, h, n: (b, jnp.minimum(n + 1, nrb - 1), c0 + h))
    ctx = lambda c0: pl.BlockSpec((1, C, HEAD_DIM), lambda b, h, n: (b, 0, c0 + h))
    return pl.pallas_call(
        functools.partial(_nbr_attn_kernel, grid_rows=grid_rows, scale=HEAD_DIM ** -0.5),
        out_shape=jax.ShapeDtypeStruct((B, L, B_COLS), BF16),
        grid=(B, B_HEADS, nrb),
        in_specs=[spec(cur(QB_BLK)), spec(prev(KB_BLK)), spec(cur(KB_BLK)), spec(nxt(KB_BLK)),
                  spec(prev(VB_BLK)), spec(cur(VB_BLK)), spec(nxt(VB_BLK)),
                  ctx(KB_BLK), ctx(VB_BLK),
                  pl.BlockSpec((1, NA_ROWS, GRID_W, NA_ROWS * GRID_W), lambda b, h, n: (h, 0, 0, 0))],
        out_specs=spec(cur(0)),
        scratch_shapes=[pltpu.VMEM((3 * tok, HEAD_DIM), BF16), pltpu.VMEM((3 * tok, HEAD_DIM), BF16)],
        compiler_params=_params("parallel", "parallel", "arbitrary"),
        name="neighbourhood_attention",
    )(qkv, qkv, qkv, qkv, qkv, qkv, qkv, qkv_c, qkv_c, bias)


def _ctx_attn_kernel(q_ref, k_ref, v_ref, sink_ref, o_ref, *, scale):
    s = _nt(q_ref[0], k_ref[0]) * scale
    sink = sink_ref[0][:1, :1]
    m = jnp.maximum(jnp.max(s, axis=-1, keepdims=True), sink)
    p = jnp.exp(s - m)
    den = jnp.sum(p, axis=-1, keepdims=True) + jnp.exp(sink - m)
    o_ref[0] = (_mm(p.astype(BF16), v_ref[0]) / den).astype(o_ref.dtype)


def context_attention(qkv_c, a_sink):
    B, C, _ = qkv_c.shape
    grp = A_HEADS // A_KV_HEADS
    heads = A_HEADS + B_HEADS
    sink = jnp.concatenate([a_sink.astype(F32), jnp.full((B_HEADS,), NEG_INF, F32)])
    sink_t = jnp.broadcast_to(sink[:, None, None], (heads, 8, LANES))
    is_a = lambda h: h < A_HEADS
    qcol = lambda h: jnp.where(is_a(h), QA_BLK + h, QB_BLK + h - A_HEADS)
    kcol = lambda h: jnp.where(is_a(h), KA_BLK + h // grp, KB_BLK + h - A_HEADS)
    vcol = lambda h: jnp.where(is_a(h), VA_BLK + h // grp, VB_BLK + h - A_HEADS)
    return pl.pallas_call(
        functools.partial(_ctx_attn_kernel, scale=HEAD_DIM ** -0.5),
        out_shape=jax.ShapeDtypeStruct((B, C, MIX_WIDTH), BF16),
        grid=(B, heads),
        in_specs=[pl.BlockSpec((1, C, HEAD_DIM), lambda b, h: (b, 0, qcol(h))),
                  pl.BlockSpec((1, C, HEAD_DIM), lambda b, h: (b, 0, kcol(h))),
                  pl.BlockSpec((1, C, HEAD_DIM), lambda b, h: (b, 0, vcol(h))),
                  pl.BlockSpec((1, 8, LANES), lambda b, h: (h, 0, 0))],
        out_specs=pl.BlockSpec((1, C, HEAD_DIM), lambda b, h: (b, 0, h)),
        compiler_params=_params("parallel", "parallel"),
        name="context_attention",
    )(qkv_c, qkv_c, qkv_c, sink_t)


def _wkv_pairs(ins, cum, sts, strict_s, incl_s, lane_lo, bd):
    C = ins[0][0].shape[0]
    P = range(len(ins))
    steps = int(math.log2(C))
    cat = jnp.concatenate

    def bdw(x):
        return jnp.where(bd, cat([x, x], axis=0), 0.0).astype(BF16)

    tot = [jnp.sum(ins[p][1], axis=0, keepdims=True) for p in P]
    g_inv = [jnp.exp(-cum[p]) for p in P]
    lhs = [cat([ins[p][4] * jnp.exp(cum[p] - ins[p][1]), ins[p][0] * jnp.exp(cum[p])], axis=0).astype(BF16)
           for p in P]
    b_t = [ins[p][5] * g_inv[p] for p in P]
    k_t = [ins[p][2] * g_inv[p] for p in P]
    rhs = [cat([jnp.where(lane_lo, b_t[p], 0.0), jnp.where(lane_lo, 0.0, b_t[p]),
                jnp.where(lane_lo, k_t[p], 0.0), jnp.where(lane_lo, 0.0, k_t[p])], axis=0).astype(BF16) for p in P]
    prod = [_nt(lhs[p], rhs[p]) for p in P]
    sh = [_nt(lhs[p], sts[p].astype(BF16)) for p in P]
    ns = [jnp.where(strict_s, prod[p][0:C, 0:2 * C], 0.0) for p in P]
    n_k = [cat([jnp.where(strict_s, prod[p][0:C, 2 * C:4 * C], 0.0),
                jnp.where(incl_s, prod[p][C:2 * C, 2 * C:4 * C], 0.0)], axis=0).astype(BF16) for p in P]
    n_rb = [jnp.where(incl_s, prod[p][C:2 * C, 0:2 * C], 0.0).astype(BF16) for p in P]
    kv = [_mm(n_k[p], bdw(ins[p][3])) for p in P]
    x = [sh[p][0:C] + kv[p][0:C] for p in P]
    for i in range(steps):
        nsb = [ns[p].astype(BF16) for p in P]
        if i + 1 < steps:
            z = [_mm(nsb[p], cat([bdw(x[p]), bdw(ns[p])], axis=1)) for p in P]
            x = [x[p] + z[p][:, 0:LANES] for p in P]
            ns = [z[p][:, LANES:2 * LANES] for p in P]
        else:
            x = [x[p] + _mm(nsb[p], bdw(x[p])) for p in P]
    y = [sh[p][C:2 * C] + kv[p][C:2 * C] + _mm(n_rb[p], bdw(x[p])) for p in P]
    uv_t = [cat([x[p], ins[p][3]], axis=0).T.astype(BF16) for p in P]
    g_tail = [jnp.exp(tot[p] - cum[p]) for p in P]
    bk = [cat([ins[p][5] * g_tail[p], ins[p][2] * g_tail[p]], axis=0).astype(BF16) for p in P]
    st_new = [sts[p] * jnp.exp(tot[p]) + jnp.where(bd, _mm(uv_t[p], bk[p]), 0.0) for p in P]
    return y, st_new


def _cumsum_time(x, reverse):
    C = x.shape[0]
    ti = lax.broadcasted_iota(jnp.int32, (C, C), 0)
    si = lax.broadcasted_iota(jnp.int32, (C, C), 1)
    tri = jnp.where((si >= ti) if reverse else (si <= ti), 1.0, 0.0).astype(BF16)
    hi = x.astype(BF16)
    rem = x - hi.astype(F32)
    mid = rem.astype(BF16)
    lo = (rem - mid.astype(F32)).astype(BF16)
    return _mm(tri, hi) + _mm(tri, mid) + _mm(tri, lo)


def _head_sum(x, lane_lo):
    lo = jnp.sum(jnp.where(lane_lo, x, 0.0), axis=-1, keepdims=True)
    hi = jnp.sum(jnp.where(lane_lo, 0.0, x), axis=-1, keepdims=True)
    return jnp.where(lane_lo, lo, hi)


_V_W0, _V_A0, _V_KK, _V_KA, _V_A0_OTHER, _V_RK, _V_LNW, _V_LNB = range(8)


def _wkv_kernel(*refs, reverse, pairs):
    C = WKV_CHUNK
    if reverse:
        (r_ref, k_ref, v_ref, wl_ref, al_ref, w2_ref, a2_ref, vec_ref, a2o_ref, yf_ref, g_ref,
         o_ref, st_ref) = refs
    else:
        r_ref, k_ref, v_ref, wl_ref, al_ref, w2_ref, a2_ref, vec_ref, o_ref, st_ref = refs

    @pl.when(pl.program_id(2) == 0)
    def _():
        st_ref[...] = jnp.zeros_like(st_ref)

    tt = lax.broadcasted_iota(jnp.int32, (C, 2 * C), 0)
    ss = lax.broadcasted_iota(jnp.int32, (C, 2 * C), 1) % C
    incl_s = (ss >= tt) if reverse else (ss <= tt)
    strict_s = (ss > tt) if reverse else (ss < tt)
    lane_lo = lax.broadcasted_iota(jnp.int32, (C, LANES), 1) < RW_HEAD
    bd = (lax.broadcasted_iota(jnp.int32, (LANES, LANES), 0) // RW_HEAD) == (
        lax.broadcasted_iota(jnp.int32, (LANES, LANES), 1) // RW_HEAD)
    sls = [slice(p * LANES, (p + 1) * LANES) for p in range(pairs)]
    vec = lambda row, sl: vec_ref[row:row + 1, sl]

    wl = jnp.tanh(wl_ref[0]).astype(BF16)
    al = al_ref[0].astype(BF16)
    w_raw = vec_ref[_V_W0:_V_W0 + 1, :] + _mm(wl, w2_ref[...])
    lw_all = -math.exp(-0.5) * jax.nn.sigmoid(w_raw)
    cum_all = _cumsum_time(lw_all, reverse)
    ag_all = jax.nn.sigmoid(vec_ref[_V_A0:_V_A0 + 1, :] + _mm(al, a2_ref[...]))
    if reverse:
        ag_o_all = jax.nn.sigmoid(vec_ref[_V_A0_OTHER:_V_A0_OTHER + 1, :] + _mm(al, a2o_ref[...]))
    ins = []
    for sl in sls:
        lw, ag = lw_all[:, sl], ag_all[:, sl]
        k_raw = k_ref[0, :, sl]
        kkf = k_raw * vec(_V_KK, sl)
        kk = kkf * lax.rsqrt(_head_sum(kkf * kkf, lane_lo) + 1e-12)
        kd = k_raw * (1.0 + (ag - 1.0) * vec(_V_KA, sl))
        ins.append((r_ref[0, :, sl], lw, kd, v_ref[0, :, sl], -kk, kk * ag))
    y, st_new = _wkv_pairs(ins, [cum_all[:, sl] for sl in sls], [st_ref[p] for p in range(pairs)],
                           strict_s, incl_s, lane_lo, bd)
    for p, sl in enumerate(sls):
        st_ref[p] = st_new[p]
        if not reverse:
            o_ref[0, :, sl] = y[p]
            continue
        r, _, kd, v, _, _ = ins[p]
        k_sum = kd + k_ref[0, :, sl] * (1.0 + (ag_o_all[:, sl] - 1.0) * vec(_V_KA, sl))
        bonus = _head_sum(r * k_sum * vec(_V_RK, sl), lane_lo) * v
        ysum = yf_ref[0, :, sl] + y[p]
        mean = _head_sum(ysum, lane_lo) * (1.0 / RW_HEAD)
        dev = ysum - mean
        var = _head_sum(dev * dev, lane_lo) * (1.0 / RW_HEAD)
        yn = dev * lax.rsqrt(var + RW_LNX_EPS) * vec(_V_LNW, sl) + vec(_V_LNB, sl)
        o_ref[0, :, sl] = ((yn + bonus) * g_ref[0, :, sl]).astype(o_ref.dtype)


def wkv_scan(proj, w2, a2, vecs, *, ctx_len, reverse, a2_other=None, y_fwd=None, g=None):
    B, T, _ = proj.shape
    D = vecs.shape[1]
    C = WKV_CHUNK
    L = T - 2 * ctx_len
    assert ctx_len % C == 0 and L % C == 0 and 2 * C == LANES
    nchunk = (ctx_len + L) // C
    cc = ctx_len // C
    nl = L // C
    pairs = min(WKV_PAIRS, D // LANES)
    width = pairs * LANES
    assert D % width == 0
    nd = D // width
    if reverse:
        tchunk = lambda ci: cc + nchunk - 1 - ci
    else:
        tchunk = lambda ci: ci
    lchunk = lambda ci: jnp.clip(tchunk(ci) - cc, 0, nl - 1)
    pspec = lambda m: pl.BlockSpec((1, C, width), lambda bi, di, ci: (bi, tchunk(ci), m * nd + di))
    hid0 = 3 * D // RW_TN
    hspec = lambda m: pl.BlockSpec((1, C, RW_TN), lambda bi, di, ci: (bi, tchunk(ci), hid0 + m))
    lspec = pl.BlockSpec((1, C, width), lambda bi, di, ci: (bi, lchunk(ci), di))
    wspec = lambda rows: pl.BlockSpec((rows, width), lambda bi, di, ci: (0, di))
    args = [proj, proj, proj, proj, proj, w2, a2, vecs]
    in_specs = [pspec(0), pspec(1), pspec(2), hspec(1), hspec(2), wspec(RW_TN), wspec(RW_TN),
                wspec(vecs.shape[0])]
    if reverse:
        args += [a2_other, y_fwd, g]
        in_specs += [wspec(RW_TN), lspec,
                     pl.BlockSpec((1, C, width), lambda bi, di, ci: (bi, tchunk(ci), di))]
    return pl.pallas_call(
        functools.partial(_wkv_kernel, reverse=reverse, pairs=pairs),
        out_shape=jax.ShapeDtypeStruct((B, L, D), BF16 if reverse else F32),
        grid=(B, nd, nchunk),
        in_specs=in_specs,
        out_specs=lspec,
        scratch_shapes=[pltpu.VMEM((pairs, LANES, LANES), F32)],
        compiler_params=_params("parallel", "parallel", "arbitrary"),
        name="wkv_scan_bwd" if reverse else "wkv_scan_fwd",
    )(*args)


def _moe_ffn_kernel(be_ref, nused_ref, x_ref, w1_ref, w3_ref, w2_ref, o_ref, w1b, w3b, w2b):
    i = pl.program_id(0)
    used = i < nused_ref[0]

    @pl.when(used & ((i == 0) | (be_ref[i] != be_ref[jnp.maximum(i - 1, 0)])))
    def _():
        w1b[...] = w1_ref[0, 0].astype(BF16)
        w3b[...] = w3_ref[0, 0].astype(BF16)
        w2b[...] = w2_ref[0, 0].astype(BF16)

    @pl.when(used)
    def _():
        x = x_ref[...]
        h1 = _mm(x, w1b[...])
        h3 = _mm(x, w3b[...])
        hh = (h1 * jax.nn.sigmoid(h1)) * h3
        o_ref[...] = _mm(hh.astype(BF16), w2b[...]).astype(o_ref.dtype)

    @pl.when(i >= nused_ref[0])
    def _():
        o_ref[...] = jnp.zeros_like(o_ref)


def moe_ffn(x_rows, blk_expert, n_used, w1, w3, w2, layer):
    slots, D = x_rows.shape
    F = w1.shape[3]
    nblk = slots // MOE_ROWS
    grid_spec = pltpu.PrefetchScalarGridSpec(
        num_scalar_prefetch=2,
        grid=(nblk,),
        in_specs=[pl.BlockSpec((MOE_ROWS, D), lambda i, be, nu: (i, 0)),
                  pl.BlockSpec((1, 1, D, F), lambda i, be, nu: (layer, be[i], 0, 0)),
                  pl.BlockSpec((1, 1, D, F), lambda i, be, nu: (layer, be[i], 0, 0)),
                  pl.BlockSpec((1, 1, F, D), lambda i, be, nu: (layer, be[i], 0, 0))],
        out_specs=pl.BlockSpec((MOE_ROWS, D), lambda i, be, nu: (i, 0)),
        scratch_shapes=[pltpu.VMEM((D, F), BF16), pltpu.VMEM((D, F), BF16), pltpu.VMEM((F, D), BF16)],
    )
    return pl.pallas_call(
        _moe_ffn_kernel,
        out_shape=jax.ShapeDtypeStruct((slots, D), BF16),
        grid_spec=grid_spec,
        compiler_params=_params("arbitrary"),
        name="moe_ffn",
    )(blk_expert, n_used, x_rows, w1.astype(F32), w3.astype(F32), w2.astype(F32))


def hierarchical_moe(route, hb, w1, w3, w2, layer):
    n, D = hb.shape
    p_grp = jax.nn.softmax(route[:, :N_GROUPS], axis=-1)
    g_sel = jnp.argmax(p_grp, axis=-1)
    p_sel = jnp.take_along_axis(p_grp, g_sel[:, None], axis=-1)
    logits = route[:, N_GROUPS:N_GROUPS + N_EXPERTS].reshape(n, N_GROUPS, EXPERTS_PER_GROUP)
    logits_g = jnp.take_along_axis(logits, g_sel[:, None, None], axis=1)[:, 0]
    top_val, top_idx = lax.top_k(logits_g, TOP_K)
    gate = p_sel * jax.nn.softmax(top_val, axis=-1)
    expert = (g_sel[:, None] * EXPERTS_PER_GROUP + top_idx).astype(jnp.int32)
    flat_e = expert.reshape(-1)
    onehot = (flat_e[:, None] == jnp.arange(N_EXPERTS, dtype=jnp.int32)[None, :]).astype(jnp.int32)
    csum = jnp.cumsum(onehot, axis=0)
    rank = jnp.take_along_axis(csum, flat_e[:, None], axis=1)[:, 0] - 1
    counts = csum[-1]
    padded = (counts + MOE_ROWS - 1) // MOE_ROWS * MOE_ROWS
    pad_end = jnp.cumsum(padded)
    dest = (pad_end - padded)[flat_e] + rank
    nblk = (n * TOP_K + N_EXPERTS * (MOE_ROWS - 1) + MOE_ROWS - 1) // MOE_ROWS
    slots = nblk * MOE_ROWS
    flat_tok = jnp.repeat(jnp.arange(n, dtype=jnp.int32), TOP_K)
    slot_tok = jnp.zeros((slots,), jnp.int32).at[dest].set(flat_tok)
    blk_start = jnp.arange(nblk, dtype=jnp.int32) * MOE_ROWS
    blk_expert = jnp.minimum(jnp.sum((pad_end[None, :] <= blk_start[:, None]).astype(jnp.int32), axis=1),
                             N_EXPERTS - 1).astype(jnp.int32)
    n_used = (pad_end[-1] // MOE_ROWS).astype(jnp.int32).reshape(1)
    x_rows = hb[slot_tok]
    y = moe_ffn(x_rows, blk_expert, n_used, w1, w3, w2, layer)
    d2 = dest.reshape(n, TOP_K)
    return y[d2[:, 0]], y[d2[:, 1]], gate


def _moe_combine_kernel(ya_ref, yb_ref, gate_ref, res_ref, gt_ref, o_ref):
    gate = gate_ref[...]
    f = ya_ref[...].astype(F32) * gate[:, 0:1] + yb_ref[...].astype(F32) * gate[:, 1:2]
    o_ref[...] = res_ref[...] + gt_ref[0] * f


def moe_combine(ya, yb, gate, res, gt, row0, tm=512):
    M, D = res.shape
    tm = min(tm, M)
    nb = gt.shape[0]
    rpb = M // nb
    assert M % tm == 0 and row0 % tm == 0 and rpb % tm == 0
    r0 = row0 // tm
    row_spec = lambda w: pl.BlockSpec((tm, w), lambda i: (r0 + i, 0))
    return pl.pallas_call(
        _moe_combine_kernel,
        out_shape=jax.ShapeDtypeStruct((M, D), F32),
        grid=(M // tm,),
        in_specs=[row_spec(D), row_spec(D), row_spec(TOP_K),
                  pl.BlockSpec((tm, D), lambda i: (i, 0)),
                  pl.BlockSpec((1, 1, D), lambda i: ((i * tm) // rpb, 0, 0))],
        out_specs=pl.BlockSpec((tm, D), lambda i: (i, 0)),
        compiler_params=_params("parallel"),
        name="moe_combine",
    )(ya, yb, gate.astype(F32), res, gt.reshape(nb, 1, D).astype(F32))


def moe_router(x, norm_g, sc, sh, w_grp, w_exp):
    w_route = jnp.concatenate([w_grp, w_exp], axis=1).astype(F32)
    w_route = jnp.pad(w_route, ((0, 0), (0, LANES - w_route.shape[1])))
    return linear(x, w_route, norm=(norm_g, sc, sh), precise=True, emit_h=True)


def attention_layer(x, xc, norm_g, sc, sh, csc, csh, w_in, w_out, a_q_gain, a_k_gain, a_sink, b_q_gain, b_k_gain,
                    b_rpb, gt, cgt):
    B, L, D = x.shape
    C = xc.shape[1]
    gains = (a_q_gain, a_k_gain, b_q_gain, b_k_gain)
    qkv = qkv_projection(x.reshape(B * L, D), w_in, norm_g, sc, sh, gains, L).reshape(B, L, -1)
    qkv_c = qkv_projection(xc.reshape(B * C, D), w_in, norm_g, csc, csh, gains, None).reshape(B, C, -1)
    ya = window_attention(qkv, qkv_c, a_sink)
    yb = neighbourhood_attention(qkv, qkv_c, b_rpb)
    x_new = linear(ya.reshape(B * L, -1), w_out, x2=yb.reshape(B * L, -1),
                   gated=(x.reshape(B * L, D), gt)).reshape(B, L, D)
    mix_c = context_attention(qkv_c, a_sink)
    xc_new = linear(mix_c.reshape(B * C, -1), w_out, gated=(xc.reshape(B * C, D), cgt)).reshape(B, C, D)
    return x_new, xc_new


def rwkv_layer(x, xc, norm_g, sc, sh, csc, csh, gt, mu, wr, wk, wv, wo, w0, w1, w2, a0, a1, a2, g1, g2, k_k, k_a,
               r_k, lnx_w, lnx_b):
    B, L, D = x.shape
    C = xc.shape[1]
    T = L + 2 * C

    xcat = jnp.concatenate([xc, x, xc], axis=1).reshape(B * T, D)
    nl, na = w1.shape[2], a1.shape[2]
    assert 2 * nl <= RW_TN and 2 * na <= RW_TN and g1.shape[1] <= RW_TN
    proj = rwkv_projections(xcat, norm_g, sc, sh, csc, csh, mu,
                            (wr, wk, wv, g1, jnp.concatenate([w1[0], w1[1]], axis=1),
                             jnp.concatenate([a1[0], a1[1]], axis=1)), T, C)
    rows = lambda w, lo: jnp.pad(w, ((lo, RW_TN - lo - w.shape[0]), (0, 0))).astype(BF16)
    g = linear(proj, rows(g2, 0), act="sigmoid", x_block=(3 * D // RW_TN, RW_TN)).reshape(B, T, D)
    w2p = [rows(w2[0], 0), rows(w2[1], nl)]
    a2p = [rows(a2[0], 0), rows(a2[1], na)]
    vecs = lambda d: jnp.stack([w0[d], a0[d], k_k, k_a, a0[1 - d], r_k.reshape(D), lnx_w, lnx_b]).astype(F32)
    proj = proj.reshape(B, T, -1)
    y_fwd = wkv_scan(proj, w2p[0], a2p[0], vecs(0), ctx_len=C, reverse=False)
    z = wkv_scan(proj, w2p[1], a2p[1], vecs(1), ctx_len=C, reverse=True, a2_other=a2p[0], y_fwd=y_fwd, g=g)
    return linear(z.reshape(B * L, D), wo, gated=(x.reshape(B * L, D), gt)).reshape(B, L, D)


def kernel(x, c, ctx, c_ctx, ada_w, ada_b, norm_mix_g, norm_ffn_g, attn_w_in, attn_w_out, a_q_gain, a_k_gain, a_sink, b_q_gain, b_k_gain, b_rpb, rw_mu, rw_wr, rw_wk, rw_wv, rw_wo, rw_w0, rw_w1, rw_w2, rw_a0, rw_a1, rw_a2, rw_g1, rw_g2, rw_k_k, rw_k_a, rw_r_k, rw_lnx_w, rw_lnx_b, moe_w_grp, moe_w_exp, moe_w1, moe_w3, moe_w2):
    B, L, D = x.shape
    C = ctx.shape[1]
    depth = ada_w.shape[0]
    assert depth == 2, "layer 0 is the attention mixer, layer 1 (last) the RWKV-7 mixer"
    pad = (-(B + 1)) % 8
    cond = jnp.concatenate([jax.nn.silu(c), jax.nn.silu(c_ctx)[None, :], jnp.zeros((pad, D), F32)], axis=0)
    xc = ctx
    for i in range(depth):
        last = i == depth - 1
        j = i // 2
        mod = linear(cond, ada_w, precise=True, layer=i) + ada_b[i]
        sh1, sc1, gt1, sh2, sc2, gt2 = jnp.split(mod[:B], 6, axis=-1)
        csh1, csc1, cgt1, csh2, csc2, cgt2 = jnp.split(mod[B:B + 1], 6, axis=-1)
        if i % 2 == 0:
            x, xc = attention_layer(x, xc, norm_mix_g[i], sc1, sh1, csc1, csh1, attn_w_in[j], attn_w_out[j],
                                    a_q_gain[j], a_k_gain[j], a_sink[j], b_q_gain[j], b_k_gain[j], b_rpb[j],
                                    gt1, cgt1)
        else:
            x = rwkv_layer(x, xc, norm_mix_g[i], sc1, sh1, csc1, csh1, gt1, rw_mu[j], rw_wr[j], rw_wk[j], rw_wv[j],
                           rw_wo[j], rw_w0[j], rw_w1[j], rw_w2[j], rw_a0[j], rw_a1[j], rw_a2[j], rw_g1[j], rw_g2[j],
                           rw_k_k[j], rw_k_a[j], rw_r_k[j], rw_lnx_w[j], rw_lnx_b[j])
        route, hb = moe_router(x.reshape(B * L, D), norm_ffn_g[i], sc2, sh2, moe_w_grp[i], moe_w_exp[i])
        if last:
            ya, yb, gate = hierarchical_moe(route, hb, moe_w1, moe_w3, moe_w2, i)
            x = moe_combine(ya, yb, gate, x.reshape(B * L, D), gt2, 0).reshape(B, L, D)
        else:
            route_c, hb_c = moe_router(xc.reshape(B * C, D), norm_ffn_g[i], csc2, csh2, moe_w_grp[i], moe_w_exp[i])
            ya, yb, gate = hierarchical_moe(jnp.concatenate([route, route_c], axis=0),
                                            jnp.concatenate([hb, hb_c], axis=0), moe_w1, moe_w3, moe_w2, i)
            x = moe_combine(ya, yb, gate, x.reshape(B * L, D), gt2, 0).reshape(B, L, D)
            xc = moe_combine(ya, yb, gate, xc.reshape(B * C, D), cgt2, B * L).reshape(B, C, D)
    return x
```

```python
import functools
import math

import numpy as np
import jax
import jax.numpy as jnp
from jax import lax
from jax.experimental import pallas as pl
from jax.experimental.pallas import tpu as pltpu

GRID_W = 64
HEAD_DIM = 128
A_HEADS = 8
A_KV_HEADS = 2
A_BLOCK = 128
B_HEADS = 8
NA_ROWS = 8
NA_COLS = 16
ROPE_THETA = 10000.0
A_Q_COLS = A_HEADS * HEAD_DIM
A_KV_COLS = A_KV_HEADS * HEAD_DIM
B_COLS = B_HEADS * HEAD_DIM
QKV_COLS = A_Q_COLS + 2 * A_KV_COLS + 3 * B_COLS
MIX_WIDTH = A_Q_COLS + B_COLS
RW_HEAD = 64
RW_LNX_EPS = 64e-5
N_GROUPS = 4
EXPERTS_PER_GROUP = 8
N_EXPERTS = N_GROUPS * EXPERTS_PER_GROUP
TOP_K = 2
NORM_EPS = 1e-6
NEG_INF = -1e30

LANES = 128
VMEM_LIMIT_BYTES = 48 * 1024 * 1024
MOE_ROWS = 256
MOE_SEGMENTS = 4
NB_ROWS = 8
WKV_CHUNK = 64
WKV_PAIRS = 16
QKV_TN = 4 * HEAD_DIM
ROW_TILE = 1024

QA_BLK = 0
KA_BLK = A_Q_COLS // HEAD_DIM
VA_BLK = KA_BLK + A_KV_HEADS
QB_BLK = VA_BLK + A_KV_HEADS
KB_BLK = QB_BLK + B_HEADS
VB_BLK = KB_BLK + B_HEADS

BF16 = jnp.bfloat16
F32 = jnp.float32


def _params(*sem):
    return pltpu.CompilerParams(dimension_semantics=sem, vmem_limit_bytes=VMEM_LIMIT_BYTES)


def _nt(a, b):
    return lax.dot_general(a, b, (((1,), (1,)), ((), ())), preferred_element_type=F32)


def _mm(a, b):
    return jnp.dot(a, b, preferred_element_type=F32)


def _row_tile(want, group_sizes):
    t = want
    while t > 8 and any(g % t for g in group_sizes):
        t //= 2
    assert all(g % t == 0 for g in group_sizes)
    return t


def _rms_modulate(x, g, sc, sh):
    y = x * lax.rsqrt(jnp.mean(x * x, axis=-1, keepdims=True) + NORM_EPS)
    return (y * g) * (1.0 + sc) + sh


def _linear_kernel(*refs, mode, gated, precise, emit_h, layered):
    it = iter(refs)
    x_ref = next(it)
    if mode == "norm":
        g_ref, sc_ref, sh_ref = next(it), next(it), next(it)
    elif mode == "cat2":
        x2_ref = next(it)
    w_ref = next(it)
    if gated:
        res_ref, gate_ref = next(it), next(it)
    o_ref = next(it)
    if emit_h:
        hb_ref = next(it)
    h_ref = next(it)

    @pl.when(pl.program_id(1) == 0)
    def _():
        if mode == "cat2":
            k1 = x_ref.shape[1]
            h_ref[:, 0:k1] = x_ref[...].astype(h_ref.dtype)
            h_ref[:, k1:] = x2_ref[...].astype(h_ref.dtype)
            return
        x = x_ref[...].astype(F32)
        if mode == "norm":
            x = _rms_modulate(x, g_ref[...], sc_ref[0], sh_ref[0])
        elif mode == "sigmoid":
            x = jax.nn.sigmoid(x)
        h_ref[...] = x.astype(h_ref.dtype)
        if emit_h:
            hb_ref[...] = x.astype(hb_ref.dtype)

    w = w_ref[0] if layered else w_ref[...]
    if precise:
        acc = jnp.dot(h_ref[...], w, preferred_element_type=F32, precision=lax.Precision.HIGHEST)
    else:
        acc = _mm(h_ref[...], w)
    if gated:
        acc = res_ref[...] + gate_ref[0] * acc
    o_ref[...] = acc.astype(o_ref.dtype)


def linear(x, w, *, norm=None, x2=None, act=None, gated=None, out_dtype=F32, precise=False, emit_h=False,
           layer=None, x_block=None, tm=ROW_TILE, tn=512):
    M = x.shape[0]
    K = x.shape[1] if x_block is None else x_block[1]
    xcol = 0 if x_block is None else x_block[0]
    N = w.shape[-1]
    groups = [M] + ([M // norm[1].shape[0]] if norm is not None else []) + (
        [M // gated[1].shape[0]] if gated is not None else [])
    tm = _row_tile(tm, groups)
    tn = tn if N % tn == 0 else N
    assert N % tn == 0
    mode = "norm" if norm is not None else "cat2" if x2 is not None else (act or "plain")
    wdt = F32 if precise else BF16
    args = [x]
    in_specs = [pl.BlockSpec((tm, K), lambda i, j: (i, xcol))]
    vec = lambda t: t.reshape(1, -1).astype(F32)
    vec_spec = lambda k: pl.BlockSpec((1, k), lambda i, j: (0, 0))
    if mode == "norm":
        g, sc, sh = norm
        nb = sc.shape[0]
        rpb = M // nb
        assert rpb % tm == 0
        args += [vec(g), sc.reshape(nb, 1, K).astype(F32), sh.reshape(nb, 1, K).astype(F32)]
        in_specs += [vec_spec(K),
                     pl.BlockSpec((1, 1, K), lambda i, j: ((i * tm) // rpb, 0, 0)),
                     pl.BlockSpec((1, 1, K), lambda i, j: ((i * tm) // rpb, 0, 0))]
    elif mode == "cat2":
        args.append(x2)
        in_specs.append(pl.BlockSpec((tm, x2.shape[1]), lambda i, j: (i, 0)))
        K = K + x2.shape[1]
    assert w.shape[-2] == K
    args.append(w.astype(wdt))
    if layer is None:
        in_specs.append(pl.BlockSpec((K, tn), lambda i, j: (0, j)))
    else:
        in_specs.append(pl.BlockSpec((1, K, tn), lambda i, j: (layer, 0, j)))
    if gated is not None:
        res, gate = gated
        nbg = gate.shape[0]
        rpg = M // nbg
        assert rpg % tm == 0
        args += [res, gate.reshape(nbg, 1, N).astype(F32)]
        in_specs += [pl.BlockSpec((tm, tn), lambda i, j: (i, j)),
                     pl.BlockSpec((1, 1, tn), lambda i, j: ((i * tm) // rpg, 0, j))]
    out_shape = jax.ShapeDtypeStruct((M, N), out_dtype)
    out_specs = pl.BlockSpec((tm, tn), lambda i, j: (i, j))
    if emit_h:
        out_shape = (out_shape, jax.ShapeDtypeStruct((M, K), BF16))
        out_specs = (out_specs, pl.BlockSpec((tm, K), lambda i, j: (i, 0)))
    return pl.pallas_call(
        functools.partial(_linear_kernel, mode=mode, gated=gated is not None, precise=precise, emit_h=emit_h,
                          layered=layer is not None),
        out_shape=out_shape,
        grid=(M // tm, N // tn),
        in_specs=in_specs,
        out_specs=out_specs,
        scratch_shapes=[pltpu.VMEM((tm, K), wdt)],
        compiler_params=_params("parallel", "arbitrary"),
        name="linear_" + mode,
    )(*args)


RW_TN = 512


def _rwkv_proj_kernel(x_ref, prev_ref, next_ref, g_ref, sc_ref, sh_ref, csc_ref, csh_ref, mu_ref, w_ref,
                      o_ref, hs_ref, xs_ref, h_ref, *, t_len, c_len, wide_tiles):
    j = pl.program_id(1)
    tm = x_ref.shape[0]

    @pl.when(j == 0)
    def _():
        t0 = (pl.program_id(0) * tm) % t_len

        def hmod(rows, t):
            is_ctx = (t < c_len) | (t >= t_len - c_len)
            return _rms_modulate(rows, g_ref[...], jnp.where(is_ctx, csc_ref[...], sc_ref[0]),
                                 jnp.where(is_ctx, csh_ref[...], sh_ref[0]))

        row = lax.broadcasted_iota(jnp.int32, (tm, 1), 0)
        row8 = lax.broadcasted_iota(jnp.int32, (8, 1), 0)
        t = t0 + row
        h = hmod(x_ref[...].astype(F32), t)
        h_before = hmod(prev_ref[...].astype(F32), t0 - 8 + row8)[7:8]
        h_after = hmod(next_ref[...].astype(F32), t0 + tm + row8)[0:1]
        up = jnp.where(row == 0, h_before, pltpu.roll(h, 1, 0))
        dn = jnp.where(row == tm - 1, h_after, pltpu.roll(h, tm - 1, 0))
        has_prev = (t != 0) & (t != c_len) & (t != t_len - c_len)
        has_next = (t != c_len - 1) & (t != t_len - c_len - 1) & (t != t_len - 1)
        hs_ref[...] = h
        xs_ref[...] = 0.5 * (jnp.where(has_prev, up, 0.0) + jnp.where(has_next, dn, 0.0)) - h

    group = jnp.where(j < 3 * wide_tiles, j // wide_tiles, j - 3 * wide_tiles + 3)

    @pl.when((j % wide_tiles == 0) | (j >= 3 * wide_tiles))
    def _():
        h_ref[...] = (hs_ref[...] + xs_ref[...] * mu_ref[pl.ds(group, 1), :]).astype(h_ref.dtype)

    o_ref[...] = _mm(h_ref[...], w_ref[...])


def rwkv_projections(xcat, norm_g, sc, sh, csc, csh, mu, weights, t_len, c_len, tm=512):
    M, K = xcat.shape
    nb = sc.shape[0]
    assert M == nb * t_len and t_len % tm == 0 and tm % 8 == 0 and K % RW_TN == 0
    wr, wk, wv, g1, w1c, a1c = weights
    padw = lambda w: jnp.pad(w, ((0, 0), (0, RW_TN - w.shape[1])))
    w_all = jnp.concatenate([wr, wk, wv, padw(g1), padw(w1c), padw(a1c)], axis=1).astype(BF16)
    mu_g = jnp.pad(mu[jnp.array([0, 2, 3, 5, 1, 4])].astype(F32), ((0, 2), (0, 0)))
    n_out = w_all.shape[1]
    per = tm // 8
    vec = lambda t: t.reshape(1, -1).astype(F32)
    vec_spec = pl.BlockSpec((1, K), lambda i, j: (0, 0))
    mod_spec = pl.BlockSpec((1, 1, K), lambda i, j: ((i * tm) // t_len, 0, 0))
    return pl.pallas_call(
        functools.partial(_rwkv_proj_kernel, t_len=t_len, c_len=c_len, wide_tiles=K // RW_TN),
        out_shape=jax.ShapeDtypeStruct((M, n_out), F32),
        grid=(M // tm, n_out // RW_TN),
        in_specs=[pl.BlockSpec((tm, K), lambda i, j: (i, 0)),
                  pl.BlockSpec((8, K), lambda i, j: (jnp.maximum(i * per - 1, 0), 0)),
                  pl.BlockSpec((8, K), lambda i, j: (jnp.minimum((i + 1) * per, M // 8 - 1), 0)),
                  vec_spec, mod_spec, mod_spec, vec_spec, vec_spec,
                  pl.BlockSpec((8, K), lambda i, j: (0, 0)),
                  pl.BlockSpec((K, RW_TN), lambda i, j: (0, j))],
        out_specs=pl.BlockSpec((tm, RW_TN), lambda i, j: (i, j)),
        scratch_shapes=[pltpu.VMEM((tm, K), F32), pltpu.VMEM((tm, K), F32), pltpu.VMEM((tm, K), BF16)],
        compiler_params=_params("parallel", "arbitrary"),
        name="rwkv_projections",
    )(xcat, xcat, xcat, vec(norm_g), sc.reshape(nb, 1, K).astype(F32), sh.reshape(nb, 1, K).astype(F32),
      vec(csc), vec(csh), mu_g, w_all)


def _qkv_kernel(x_ref, g_ref, sc_ref, sh_ref, w_ref, gain_ref, cos_ref, sin_ref, o_ref, h_ref, *, rope):
    j = pl.program_id(1)

    @pl.when(j == 0)
    def _():
        h_ref[...] = _rms_modulate(x_ref[...].astype(F32), g_ref[...], sc_ref[0], sh_ref[0]).astype(h_ref.dtype)

    acc = _mm(h_ref[...], w_ref[...])
    tm = acc.shape[0]

    def emit(kinds):
        for hh, kind in enumerate(kinds):
            sl = slice(hh * HEAD_DIM, (hh + 1) * HEAD_DIM)
            y = acc[:, sl]
            if kind != "plain":
                y = y * lax.rsqrt(jnp.mean(y * y, axis=-1, keepdims=True) + NORM_EPS) * gain_ref[:, sl]
                if kind == "rope" and rope:
                    lane = lax.broadcasted_iota(jnp.int32, (tm, HEAD_DIM), 1)
                    first = (lane % (HEAD_DIM // 2)) < HEAD_DIM // 4
                    partner = jnp.where(first, pltpu.roll(y, HEAD_DIM - HEAD_DIM // 4, 1),
                                        pltpu.roll(y, HEAD_DIM // 4, 1))
                    y = y * cos_ref[...] + partner * sin_ref[...]
            o_ref[:, sl] = y.astype(o_ref.dtype)

    per = QKV_TN // HEAD_DIM
    kinds = (["rope"] * A_HEADS + ["rope"] * A_KV_HEADS + ["plain"] * A_KV_HEADS + ["norm"] * B_HEADS
             + ["norm"] * B_HEADS + ["plain"] * B_HEADS)
    tiles = [tuple(kinds[t * per:(t + 1) * per]) for t in range(len(kinds) // per)]
    for kind_set in sorted(set(tiles)):
        cond = None
        for t, ks in enumerate(tiles):
            if ks == kind_set:
                cond = (j == t) if cond is None else (cond | (j == t))
        pl.when(cond)(functools.partial(emit, kind_set))


def qkv_projection(x, w_in, norm_g, sc, sh, gains, rope_len):
    M, K = x.shape
    N = w_in.shape[1]
    assert N == QKV_COLS and N % QKV_TN == 0
    nb = sc.shape[0]
    rpb = M // nb
    tm = _row_tile(ROW_TILE, [rpb] + ([rope_len] if rope_len is not None else []))
    a_q, a_k, b_q, b_k = (t.astype(F32) for t in gains)
    ones = jnp.ones((HEAD_DIM,), F32)
    gain = jnp.concatenate([jnp.tile(a_q, A_HEADS), jnp.tile(a_k, A_KV_HEADS), jnp.tile(ones, A_KV_HEADS),
                            jnp.tile(b_q, B_HEADS), jnp.tile(b_k, B_HEADS), jnp.tile(ones, B_HEADS)]).reshape(1, N)
    rope = rope_len is not None
    if rope:
        assert rope_len % tm == 0
        t = jnp.arange(rope_len)
        quarter = HEAD_DIM // 4
        inv_freq = ROPE_THETA ** (-jnp.arange(quarter, dtype=F32) / quarter)
        ar = (t // GRID_W).astype(F32)[:, None] * inv_freq[None, :]
        ac = (t % GRID_W).astype(F32)[:, None] * inv_freq[None, :]
        cos_t = jnp.concatenate([jnp.cos(ar), jnp.cos(ar), jnp.cos(ac), jnp.cos(ac)], axis=1)
        sin_t = jnp.concatenate([-jnp.sin(ar), jnp.sin(ar), -jnp.sin(ac), jnp.sin(ac)], axis=1)
        nt = rope_len // tm
    else:
        cos_t = jnp.ones((tm, HEAD_DIM), F32)
        sin_t = jnp.zeros((tm, HEAD_DIM), F32)
        nt = 1
    tab_spec = pl.BlockSpec((tm, HEAD_DIM), lambda i, j: (i % nt, 0))
    return pl.pallas_call(
        functools.partial(_qkv_kernel, rope=rope),
        out_shape=jax.ShapeDtypeStruct((M, N), BF16),
        grid=(M // tm, N // QKV_TN),
        in_specs=[pl.BlockSpec((tm, K), lambda i, j: (i, 0)),
                  pl.BlockSpec((1, K), lambda i, j: (0, 0)),
                  pl.BlockSpec((1, 1, K), lambda i, j: ((i * tm) // rpb, 0, 0)),
                  pl.BlockSpec((1, 1, K), lambda i, j: ((i * tm) // rpb, 0, 0)),
                  pl.BlockSpec((K, QKV_TN), lambda i, j: (0, j)),
                  pl.BlockSpec((1, QKV_TN), lambda i, j: (0, j)),
                  tab_spec, tab_spec],
        out_specs=pl.BlockSpec((tm, QKV_TN), lambda i, j: (i, j)),
        scratch_shapes=[pltpu.VMEM((tm, K), BF16)],
        compiler_params=_params("parallel", "arbitrary"),
        name="qkv_projection",
    )(x, norm_g.reshape(1, K).astype(F32), sc.reshape(nb, 1, K).astype(F32), sh.reshape(nb, 1, K).astype(F32),
      w_in.astype(BF16), gain, cos_t, sin_t)


def _win_attn_kernel(q_ref, kp_ref, kc_ref, kn_ref, vp_ref, vc_ref, vn_ref, kx_ref, vx_ref, sink_ref,
                     o_ref, *, nblk, groups, scale):
    n = pl.program_id(2)
    blk = A_BLOCK
    q = q_ref[0]
    qs = jnp.concatenate([q[:, g * HEAD_DIM:(g + 1) * HEAD_DIM] for g in range(groups)], axis=0)
    rows = groups * blk
    keys = jnp.concatenate([kp_ref[0], kc_ref[0], kn_ref[0], kx_ref[0]], axis=0)
    vals = jnp.concatenate([vp_ref[0], vc_ref[0], vn_ref[0], vx_ref[0]], axis=0)
    nkeys = keys.shape[0]
    s = _nt(qs, keys) * scale
    qi = lax.broadcasted_iota(jnp.int32, (rows, nkeys), 0) % blk
    kj = lax.broadcasted_iota(jnp.int32, (rows, nkeys), 1)
    ok = (((kj >= blk) | ((kj >= qi) & (n > 0)))
          & ((kj < 2 * blk) | (kj >= 3 * blk) | ((kj - 2 * blk <= qi) & (n < nblk - 1))))
    s = jnp.where(ok, s, NEG_INF)
    sink = sink_ref[0][:, :1]
    m = jnp.maximum(jnp.max(s, axis=-1, keepdims=True), sink)
    p = jnp.exp(s - m)
    den = jnp.sum(p, axis=-1, keepdims=True) + jnp.exp(sink - m)
    o = _mm(p.astype(BF16), vals) / den
    for g in range(groups):
        o_ref[0, :, g * HEAD_DIM:(g + 1) * HEAD_DIM] = o[g * blk:(g + 1) * blk].astype(o_ref.dtype)


def window_attention(qkv, qkv_c, sink):
    B, L, _ = qkv.shape
    C = qkv_c.shape[1]
    groups = A_HEADS // A_KV_HEADS
    nblk = L // A_BLOCK
    gw = groups * HEAD_DIM
    sink_t = jnp.broadcast_to(
        jnp.repeat(sink.astype(F32).reshape(A_KV_HEADS, groups), A_BLOCK, axis=1)[:, :, None],
        (A_KV_HEADS, groups * A_BLOCK, LANES))
    blkspec = lambda f: pl.BlockSpec((1, A_BLOCK, HEAD_DIM), f)
    prev = lambda c0: (lambda b, h, n: (b, jnp.maximum(n - 1, 0), c0 + h))
    cur = lambda c0: (lambda b, h, n: (b, n, c0 + h))
    nxt = lambda c0: (lambda b, h, n: (b, jnp.minimum(n + 1, nblk - 1), c0 + h))
    ctx = lambda c0: pl.BlockSpec((1, C, HEAD_DIM), lambda b, h, n: (b, 0, c0 + h))
    return pl.pallas_call(
        functools.partial(_win_attn_kernel, nblk=nblk, groups=groups, scale=HEAD_DIM ** -0.5),
        out_shape=jax.ShapeDtypeStruct((B, L, A_Q_COLS), BF16),
        grid=(B, A_KV_HEADS, nblk),
        in_specs=[pl.BlockSpec((1, A_BLOCK, gw), lambda b, h, n: (b, n, h)),
                  blkspec(prev(KA_BLK)), blkspec(cur(KA_BLK)), blkspec(nxt(KA_BLK)),
                  blkspec(prev(VA_BLK)), blkspec(cur(VA_BLK)), blkspec(nxt(VA_BLK)),
                  ctx(KA_BLK), ctx(VA_BLK),
                  pl.BlockSpec((1, groups * A_BLOCK, LANES), lambda b, h, n: (h, 0, 0))],
        out_specs=pl.BlockSpec((1, A_BLOCK, gw), lambda b, h, n: (b, n, h)),
        compiler_params=_params("parallel", "parallel", "arbitrary"),
        name="window_attention",
    )(qkv, qkv, qkv, qkv, qkv, qkv, qkv, qkv_c, qkv_c, sink_t)


def _nbr_attn_kernel(q_ref, kp_ref, kc_ref, kn_ref, vp_ref, vc_ref, vn_ref, kx_ref, vx_ref, bias_ref,
                     o_ref, kbuf, vbuf, *, grid_rows, scale):
    rb = pl.program_id(2)
    tok = NB_ROWS * GRID_W
    nk = NA_ROWS * GRID_W
    nctx = kx_ref.shape[1]
    kbuf[0:tok] = kp_ref[0]
    kbuf[tok:2 * tok] = kc_ref[0]
    kbuf[2 * tok:3 * tok] = kn_ref[0]
    vbuf[0:tok] = vp_ref[0]
    vbuf[tok:2 * tok] = vc_ref[0]
    vbuf[2 * tok:3 * tok] = vn_ref[0]
    kx = kx_ref[0]
    vx = vx_ref[0]
    sx_all = _nt(q_ref[0], kx) * scale
    R = range(NB_ROWS)
    rsl = [slice(rr * GRID_W, (rr + 1) * GRID_W) for rr in R]
    r = [rb * NB_ROWS + rr for rr in R]
    r0 = [jnp.clip(r[rr] - NA_ROWS // 2, 0, grid_rows - NA_ROWS) for rr in R]
    off = [pl.multiple_of((r0[rr] - (rb - 1) * NB_ROWS) * GRID_W, GRID_W) for rr in R]
    s = [jnp.concatenate([_nt(q_ref[0, rsl[rr], :], kbuf[pl.ds(off[rr], nk), :]) * scale
                          + bias_ref[0, r0[rr] - r[rr] + NA_ROWS - 1], sx_all[rsl[rr]]], axis=1)
         for rr in R]
    m = [jnp.max(s[rr], axis=-1, keepdims=True) for rr in R]
    p = [jnp.exp(s[rr] - m[rr]) for rr in R]
    den = [jnp.sum(p[rr], axis=-1, keepdims=True) for rr in R]
    pb = [p[rr].astype(BF16) for rr in R]
    o = [(_mm(pb[rr][:, 0:nk], vbuf[pl.ds(off[rr], nk), :]) + _mm(pb[rr][:, nk:nk + nctx], vx)) / den[rr]
         for rr in R]
    for rr in R:
        o_ref[0, rsl[rr], :] = o[rr].astype(o_ref.dtype)


def _nbr_bias_table(rpb):
    col = np.arange(GRID_W)
    c0 = np.clip(col - NA_COLS // 2, 0, GRID_W - NA_COLS)
    col_ok = (col[None, :] >= c0[:, None]) & (col[None, :] < c0[:, None] + NA_COLS)
    dc = np.clip(col[None, :] - col[:, None] + (NA_COLS - 1), 0, 2 * NA_COLS - 2)
    dr = np.arange(NA_ROWS)[:, None] + np.arange(NA_ROWS)[None, :]
    sel_r = np.eye(2 * NA_ROWS - 1, dtype=np.float32)[dr]
    sel_c = np.eye(2 * NA_COLS - 1, dtype=np.float32)[dc]
    t = jnp.einsum('djr,hrc->hdjc', sel_r, rpb.astype(F32), precision=lax.Precision.HIGHEST)
    t = jnp.einsum('hdjc,qkc->hdqjk', t, sel_c, precision=lax.Precision.HIGHEST)
    t = jnp.where(col_ok[None, None, :, None, :], t, NEG_INF)
    return t.reshape(rpb.shape[0], NA_ROWS, GRID_W, NA_ROWS * GRID_W)


def neighbourhood_attention(qkv, qkv_c, rpb):
    B, L, _ = qkv.shape
    C = qkv_c.shape[1]
    grid_rows = L // GRID_W
    tok = NB_ROWS * GRID_W
    assert grid_rows >= NA_ROWS and L % tok == 0 and NB_ROWS == NA_ROWS
    nrb = L // tok
    bias = _nbr_bias_table(rpb)
    spec = lambda f: pl.BlockSpec((1, tok, HEAD_DIM), f)
    prev = lambda c0: (lambda b, h, n: (b, jnp.maximum(n - 1, 0), c0 + h))
    cur = lambda c0: (lambda b, h, n: (b, n, c0 + h))
    nxt = lambda c0: (lambda b, h, n: (b, jnp.minimum(n + 1, nrb - 1), c0 + h))
    ctx = lambda c0: pl.BlockSpec((1, C, HEAD_DIM), lambda b, h, n: (b, 0, c0 + h))
    return pl.pallas_call(
        functools.partial(_nbr_attn_kernel, grid_rows=grid_rows, scale=HEAD_DIM ** -0.5),
        out_shape=jax.ShapeDtypeStruct((B, L, B_COLS), BF16),
        grid=(B, B_HEADS, nrb),
        in_specs=[spec(cur(QB_BLK)), spec(prev(KB_BLK)), spec(cur(KB_BLK)), spec(nxt(KB_BLK)),
                  spec(prev(VB_BLK)), spec(cur(VB_BLK)), spec(nxt(VB_BLK)),
                  ctx(KB_BLK), ctx(VB_BLK),
                  pl.BlockSpec((1, NA_ROWS, GRID_W, NA_ROWS * GRID_W), lambda b, h, n: (h, 0, 0, 0))],
        out_specs=spec(cur(0)),
        scratch_shapes=[pltpu.VMEM((3 * tok, HEAD_DIM), BF16), pltpu.VMEM((3 * tok, HEAD_DIM), BF16)],
        compiler_params=_params("parallel", "parallel", "arbitrary"),
        name="neighbourhood_attention",
    )(qkv, qkv, qkv, qkv, qkv, qkv, qkv, qkv_c, qkv_c, bias)


def _ctx_attn_kernel(q_ref, k_ref, v_ref, sink_ref, o_ref, *, scale):
    s = _nt(q_ref[0], k_ref[0]) * scale
    sink = sink_ref[0][:1, :1]
    m = jnp.maximum(jnp.max(s, axis=-1, keepdims=True), sink)
    p = jnp.exp(s - m)
    den = jnp.sum(p, axis=-1, keepdims=True) + jnp.exp(sink - m)
    o_ref[0] = (_mm(p.astype(BF16), v_ref[0]) / den).astype(o_ref.dtype)


def context_attention(qkv_c, a_sink):
    B, C, _ = qkv_c.shape
    grp = A_HEADS // A_KV_HEADS
    heads = A_HEADS + B_HEADS
    sink = jnp.concatenate([a_sink.astype(F32), jnp.full((B_HEADS,), NEG_INF, F32)])
    sink_t = jnp.broadcast_to(sink[:, None, None], (heads, 8, LANES))
    is_a = lambda h: h < A_HEADS
    qcol = lambda h: jnp.where(is_a(h), QA_BLK + h, QB_BLK + h - A_HEADS)
    kcol = lambda h: jnp.where(is_a(h), KA_BLK + h // grp, KB_BLK + h - A_HEADS)
    vcol = lambda h: jnp.where(is_a(h), VA_BLK + h // grp, VB_BLK + h - A_HEADS)
    return pl.pallas_call(
        functools.partial(_ctx_attn_kernel, scale=HEAD_DIM ** -0.5),
        out_shape=jax.ShapeDtypeStruct((B, C, MIX_WIDTH), BF16),
        grid=(B, heads),
        in_specs=[pl.BlockSpec((1, C, HEAD_DIM), lambda b, h: (b, 0, qcol(h))),
                  pl.BlockSpec((1, C, HEAD_DIM), lambda b, h: (b, 0, kcol(h))),
                  pl.BlockSpec((1, C, HEAD_DIM), lambda b, h: (b, 0, vcol(h))),
                  pl.BlockSpec((1, 8, LANES), lambda b, h: (h, 0, 0))],
        out_specs=pl.BlockSpec((1, C, HEAD_DIM), lambda b, h: (b, 0, h)),
        compiler_params=_params("parallel", "parallel"),
        name="context_attention",
    )(qkv_c, qkv_c, qkv_c, sink_t)


def _wkv_pairs(ins, cum, sts, strict_s, incl_s, lane_lo, bd):
    C = ins[0][0].shape[0]
    P = range(len(ins))
    steps = int(math.log2(C))
    cat = jnp.concatenate

    def bdw(x):
        return jnp.where(bd, cat([x, x], axis=0), 0.0).astype(BF16)

    tot = [jnp.sum(ins[p][1], axis=0, keepdims=True) for p in P]
    g_inv = [jnp.exp(-cum[p]) for p in P]
    lhs = [cat([ins[p][4] * jnp.exp(cum[p] - ins[p][1]), ins[p][0] * jnp.exp(cum[p])], axis=0).astype(BF16)
           for p in P]
    b_t = [ins[p][5] * g_inv[p] for p in P]
    k_t = [ins[p][2] * g_inv[p] for p in P]
    rhs = [cat([jnp.where(lane_lo, b_t[p], 0.0), jnp.where(lane_lo, 0.0, b_t[p]),
                jnp.where(lane_lo, k_t[p], 0.0), jnp.where(lane_lo, 0.0, k_t[p])], axis=0).astype(BF16) for p in P]
    prod = [_nt(lhs[p], rhs[p]) for p in P]
    sh = [_nt(lhs[p], sts[p].astype(BF16)) for p in P]
    ns = [jnp.where(strict_s, prod[p][0:C, 0:2 * C], 0.0) for p in P]
    n_k = [cat([jnp.where(strict_s, prod[p][0:C, 2 * C:4 * C], 0.0),
                jnp.where(incl_s, prod[p][C:2 * C, 2 * C:4 * C], 0.0)], axis=0).astype(BF16) for p in P]
    n_rb = [jnp.where(incl_s, prod[p][C:2 * C, 0:2 * C], 0.0).astype(BF16) for p in P]
    kv = [_mm(n_k[p], bdw(ins[p][3])) for p in P]
    x = [sh[p][0:C] + kv[p][0:C] for p in P]
    for i in range(steps):
        nsb = [ns[p].astype(BF16) for p in P]
        if i + 1 < steps:
            z = [_mm(nsb[p], cat([bdw(x[p]), bdw(ns[p])], axis=1)) for p in P]
            x = [x[p] + z[p][:, 0:LANES] for p in P]
            ns = [z[p][:, LANES:2 * LANES] for p in P]
        else:
            x = [x[p] + _mm(nsb[p], bdw(x[p])) for p in P]
    y = [sh[p][C:2 * C] + kv[p][C:2 * C] + _mm(n_rb[p], bdw(x[p])) for p in P]
    uv_t = [cat([x[p], ins[p][3]], axis=0).T.astype(BF16) for p in P]
    g_tail = [jnp.exp(tot[p] - cum[p]) for p in P]
    bk = [cat([ins[p][5] * g_tail[p], ins[p][2] * g_tail[p]], axis=0).astype(BF16) for p in P]
    st_new = [sts[p] * jnp.exp(tot[p]) + jnp.where(bd, _mm(uv_t[p], bk[p]), 0.0) for p in P]
    return y, st_new


def _cumsum_time(x, reverse):
    C = x.shape[0]
    ti = lax.broadcasted_iota(jnp.int32, (C, C), 0)
    si = lax.broadcasted_iota(jnp.int32, (C, C), 1)
    tri = jnp.where((si >= ti) if reverse else (si <= ti), 1.0, 0.0).astype(BF16)
    hi = x.astype(BF16)
    rem = x - hi.astype(F32)
    mid = rem.astype(BF16)
    lo = (rem - mid.astype(F32)).astype(BF16)
    return _mm(tri, hi) + _mm(tri, mid) + _mm(tri, lo)


def _head_sum(x, lane_lo):
    lo = jnp.sum(jnp.where(lane_lo, x, 0.0), axis=-1, keepdims=True)
    hi = jnp.sum(jnp.where(lane_lo, 0.0, x), axis=-1, keepdims=True)
    return jnp.where(lane_lo, lo, hi)


_V_W0, _V_A0, _V_KK, _V_KA, _V_A0_OTHER, _V_RK, _V_LNW, _V_LNB = range(8)


def _wkv_kernel(*refs, reverse, pairs):
    C = WKV_CHUNK
    if reverse:
        (r_ref, k_ref, v_ref, wl_ref, al_ref, w2_ref, a2_ref, vec_ref, a2o_ref, yf_ref, g_ref,
         o_ref, st_ref) = refs
    else:
        r_ref, k_ref, v_ref, wl_ref, al_ref, w2_ref, a2_ref, vec_ref, o_ref, st_ref = refs

    @pl.when(pl.program_id(2) == 0)
    def _():
        st_ref[...] = jnp.zeros_like(st_ref)

    tt = lax.broadcasted_iota(jnp.int32, (C, 2 * C), 0)
    ss = lax.broadcasted_iota(jnp.int32, (C, 2 * C), 1) % C
    incl_s = (ss >= tt) if reverse else (ss <= tt)
    strict_s = (ss > tt) if reverse else (ss < tt)
    lane_lo = lax.broadcasted_iota(jnp.int32, (C, LANES), 1) < RW_HEAD
    bd = (lax.broadcasted_iota(jnp.int32, (LANES, LANES), 0) // RW_HEAD) == (
        lax.broadcasted_iota(jnp.int32, (LANES, LANES), 1) // RW_HEAD)
    sls = [slice(p * LANES, (p + 1) * LANES) for p in range(pairs)]
    vec = lambda row, sl: vec_ref[row:row + 1, sl]

    wl = jnp.tanh(wl_ref[0]).astype(BF16)
    al = al_ref[0].astype(BF16)
    w_raw = vec_ref[_V_W0:_V_W0 + 1, :] + _mm(wl, w2_ref[...])
    lw_all = -math.exp(-0.5) * jax.nn.sigmoid(w_raw)
    cum_all = _cumsum_time(lw_all, reverse)
    ag_all = jax.nn.sigmoid(vec_ref[_V_A0:_V_A0 + 1, :] + _mm(al, a2_ref[...]))
    if reverse:
        ag_o_all = jax.nn.sigmoid(vec_ref[_V_A0_OTHER:_V_A0_OTHER + 1, :] + _mm(al, a2o_ref[...]))
    ins = []
    for sl in sls:
        lw, ag = lw_all[:, sl], ag_all[:, sl]
        k_raw = k_ref[0, :, sl]
        kkf = k_raw * vec(_V_KK, sl)
        kk = kkf * lax.rsqrt(_head_sum(kkf * kkf, lane_lo) + 1e-12)
        kd = k_raw * (1.0 + (ag - 1.0) * vec(_V_KA, sl))
        ins.append((r_ref[0, :, sl], lw, kd, v_ref[0, :, sl], -kk, kk * ag))
    y, st_new = _wkv_pairs(ins, [cum_all[:, sl] for sl in sls], [st_ref[p] for p in range(pairs)],
                           strict_s, incl_s, lane_lo, bd)
    for p, sl in enumerate(sls):
        st_ref[p] = st_new[p]
        if not reverse:
            o_ref[0, :, sl] = y[p]
            continue
        r, _, kd, v, _, _ = ins[p]
        k_sum = kd + k_ref[0, :, sl] * (1.0 + (ag_o_all[:, sl] - 1.0) * vec(_V_KA, sl))
        bonus = _head_sum(r * k_sum * vec(_V_RK, sl), lane_lo) * v
        ysum = yf_ref[0, :, sl] + y[p]
        mean = _head_sum(ysum, lane_lo) * (1.0 / RW_HEAD)
        dev = ysum - mean
        var = _head_sum(dev * dev, lane_lo) * (1.0 / RW_HEAD)
        yn = dev * lax.rsqrt(var + RW_LNX_EPS) * vec(_V_LNW, sl) + vec(_V_LNB, sl)
        o_ref[0, :, sl] = ((yn + bonus) * g_ref[0, :, sl]).astype(o_ref.dtype)


def wkv_scan(proj, w2, a2, vecs, *, ctx_len, reverse, a2_other=None, y_fwd=None, g=None):
    B, T, _ = proj.shape
    D = vecs.shape[1]
    C = WKV_CHUNK
    L = T - 2 * ctx_len
    assert ctx_len % C == 0 and L % C == 0 and 2 * C == LANES
    nchunk = (ctx_len + L) // C
    cc = ctx_len // C
    nl = L // C
    pairs = min(WKV_PAIRS, D // LANES)
    width = pairs * LANES
    assert D % width == 0
    nd = D // width
    if reverse:
        tchunk = lambda ci: cc + nchunk - 1 - ci
    else:
        tchunk = lambda ci: ci
    lchunk = lambda ci: jnp.clip(tchunk(ci) - cc, 0, nl - 1)
    pspec = lambda m: pl.BlockSpec((1, C, width), lambda bi, di, ci: (bi, tchunk(ci), m * nd + di))
    hid0 = 3 * D // RW_TN
    hspec = lambda m: pl.BlockSpec((1, C, RW_TN), lambda bi, di, ci: (bi, tchunk(ci), hid0 + m))
    lspec = pl.BlockSpec((1, C, width), lambda bi, di, ci: (bi, lchunk(ci), di))
    wspec = lambda rows: pl.BlockSpec((rows, width), lambda bi, di, ci: (0, di))
    args = [proj, proj, proj, proj, proj, w2, a2, vecs]
    in_specs = [pspec(0), pspec(1), pspec(2), hspec(1), hspec(2), wspec(RW_TN), wspec(RW_TN),
                wspec(vecs.shape[0])]
    if reverse:
        args += [a2_other, y_fwd, g]
        in_specs += [wspec(RW_TN), lspec,
                     pl.BlockSpec((1, C, width), lambda bi, di, ci: (bi, tchunk(ci), di))]
    return pl.pallas_call(
        functools.partial(_wkv_kernel, reverse=reverse, pairs=pairs),
        out_shape=jax.ShapeDtypeStruct((B, L, D), BF16 if reverse else F32),
        grid=(B, nd, nchunk),
        in_specs=in_specs,
        out_specs=lspec,
        scratch_shapes=[pltpu.VMEM((pairs, LANES, LANES), F32)],
        compiler_params=_params("parallel", "parallel", "arbitrary"),
        name="wkv_scan_bwd" if reverse else "wkv_scan_fwd",
    )(*args)


def _moe_ffn_kernel(be_ref, nused_ref, x_ref, w1_ref, w3_ref, w2_ref, *rest, block0, aliased):
    o_ref, w1b, w3b, w2b = rest[1:] if aliased else rest
    i = pl.program_id(0)
    blk = block0 + i
    used = blk < nused_ref[0]

    @pl.when(used & ((i == 0) | (be_ref[blk] != be_ref[jnp.maximum(blk - 1, 0)])))
    def _():
        w1b[...] = w1_ref[0, 0].astype(BF16)
        w3b[...] = w3_ref[0, 0].astype(BF16)
        w2b[...] = w2_ref[0, 0].astype(BF16)

    @pl.when(used)
    def _():
        half = MOE_ROWS // 2
        sls = [slice(0, half), slice(half, MOE_ROWS)]
        x = [x_ref[sl, :] for sl in sls]
        h1 = [_mm(xh, w1b[...]) for xh in x]
        h3 = [_mm(xh, w3b[...]) for xh in x]
        hh = [((a * jax.nn.sigmoid(a)) * b).astype(BF16) for a, b in zip(h1, h3)]
        y = [_mm(h, w2b[...]) for h in hh]
        for sl, yh in zip(sls, y):
            o_ref[sl, :] = yh.astype(o_ref.dtype)

    @pl.when(jnp.logical_not(used))
    def _():
        o_ref[...] = jnp.zeros_like(o_ref)


def moe_ffn(x_rows, blk_expert, n_used, w1, w3, w2, layer, block0, nblk_total, y_prev=None):
    rows, D = x_rows.shape
    F = w1.shape[3]
    nblk = rows // MOE_ROWS
    aliased = y_prev is not None
    wspec = lambda shape: pl.BlockSpec((1, 1) + shape, lambda i, be, nu: (layer, be[block0 + i], 0, 0))
    in_specs = [pl.BlockSpec((MOE_ROWS, D), lambda i, be, nu: (i, 0)), wspec((D, F)), wspec((D, F)), wspec((F, D))]
    args = [blk_expert, n_used, x_rows, w1.astype(F32), w3.astype(F32), w2.astype(F32)]
    if aliased:
        in_specs.append(pl.BlockSpec(memory_space=pl.ANY))
        args.append(y_prev)
    grid_spec = pltpu.PrefetchScalarGridSpec(
        num_scalar_prefetch=2,
        grid=(nblk,),
        in_specs=in_specs,
        out_specs=pl.BlockSpec((MOE_ROWS, D), lambda i, be, nu: (block0 + i, 0)),
        scratch_shapes=[pltpu.VMEM((D, F), BF16), pltpu.VMEM((D, F), BF16), pltpu.VMEM((F, D), BF16)],
    )
    return pl.pallas_call(
        functools.partial(_moe_ffn_kernel, block0=block0, aliased=aliased),
        out_shape=jax.ShapeDtypeStruct((nblk_total * MOE_ROWS, D), BF16),
        grid_spec=grid_spec,
        input_output_aliases={len(args) - 1: 0} if aliased else {},
        compiler_params=_params("arbitrary"),
        name="moe_ffn",
    )(*args)


def hierarchical_moe(route, hb, w1, w3, w2, layer):
    n, D = hb.shape
    p_grp = jax.nn.softmax(route[:, :N_GROUPS], axis=-1)
    g_sel = jnp.argmax(p_grp, axis=-1)
    p_sel = jnp.take_along_axis(p_grp, g_sel[:, None], axis=-1)
    logits = route[:, N_GROUPS:N_GROUPS + N_EXPERTS].reshape(n, N_GROUPS, EXPERTS_PER_GROUP)
    logits_g = jnp.take_along_axis(logits, g_sel[:, None, None], axis=1)[:, 0]
    top_val, top_idx = lax.top_k(logits_g, TOP_K)
    gate = p_sel * jax.nn.softmax(top_val, axis=-1)
    expert = (g_sel[:, None] * EXPERTS_PER_GROUP + top_idx).astype(jnp.int32)
    flat_e = expert.reshape(-1)
    onehot = (flat_e[:, None] == jnp.arange(N_EXPERTS, dtype=jnp.int32)[None, :]).astype(jnp.int32)
    csum = jnp.cumsum(onehot, axis=0)
    rank = jnp.take_along_axis(csum, flat_e[:, None], axis=1)[:, 0] - 1
    counts = csum[-1]
    padded = (counts + MOE_ROWS - 1) // MOE_ROWS * MOE_ROWS
    pad_end = jnp.cumsum(padded)
    dest = (pad_end - padded)[flat_e] + rank
    nblk = (n * TOP_K + N_EXPERTS * (MOE_ROWS - 1) + MOE_ROWS - 1) // MOE_ROWS
    nblk = -(-nblk // MOE_SEGMENTS) * MOE_SEGMENTS
    slots = nblk * MOE_ROWS
    flat_tok = jnp.repeat(jnp.arange(n, dtype=jnp.int32), TOP_K)
    slot_tok = jnp.zeros((slots,), jnp.int32).at[dest].set(flat_tok)
    blk_start = jnp.arange(nblk, dtype=jnp.int32) * MOE_ROWS
    blk_expert = jnp.minimum(jnp.sum((pad_end[None, :] <= blk_start[:, None]).astype(jnp.int32), axis=1),
                             N_EXPERTS - 1).astype(jnp.int32)
    n_used = (pad_end[-1] // MOE_ROWS).astype(jnp.int32).reshape(1)
    seg_blk = nblk // MOE_SEGMENTS
    y = None
    for s in range(MOE_SEGMENTS):
        x_rows = hb[slot_tok[s * seg_blk * MOE_ROWS:(s + 1) * seg_blk * MOE_ROWS]]
        y = moe_ffn(x_rows, blk_expert, n_used, w1, w3, w2, layer, s * seg_blk, nblk, y)
    d2 = dest.reshape(n, TOP_K)
    return y[d2[:, 0]], y[d2[:, 1]], gate


def _moe_combine_kernel(ya_ref, yb_ref, gate_ref, res_ref, gt_ref, o_ref):
    gate = gate_ref[...]
    f = ya_ref[...].astype(F32) * gate[:, 0:1] + yb_ref[...].astype(F32) * gate[:, 1:2]
    o_ref[...] = res_ref[...] + gt_ref[0] * f


def moe_combine(ya, yb, gate, res, gt, row0, tm=512):
    M, D = res.shape
    tm = min(tm, M)
    nb = gt.shape[0]
    rpb = M // nb
    assert M % tm == 0 and row0 % tm == 0 and rpb % tm == 0
    r0 = row0 // tm
    row_spec = lambda w: pl.BlockSpec((tm, w), lambda i: (r0 + i, 0))
    return pl.pallas_call(
        _moe_combine_kernel,
        out_shape=jax.ShapeDtypeStruct((M, D), F32),
        grid=(M // tm,),
        in_specs=[row_spec(D), row_spec(D), row_spec(TOP_K),
                  pl.BlockSpec((tm, D), lambda i: (i, 0)),
                  pl.BlockSpec((1, 1, D), lambda i: ((i * tm) // rpb, 0, 0))],
        out_specs=pl.BlockSpec((tm, D), lambda i: (i, 0)),
        compiler_params=_params("parallel"),
        name="moe_combine",
    )(ya, yb, gate.astype(F32), res, gt.reshape(nb, 1, D).astype(F32))


def moe_router(x, norm_g, sc, sh, w_grp, w_exp):
    w_route = jnp.concatenate([w_grp, w_exp], axis=1).astype(F32)
    w_route = jnp.pad(w_route, ((0, 0), (0, LANES - w_route.shape[1])))
    return linear(x, w_route, norm=(norm_g, sc, sh), precise=True, emit_h=True)


def attention_layer(x, xc, norm_g, sc, sh, csc, csh, w_in, w_out, a_q_gain, a_k_gain, a_sink, b_q_gain, b_k_gain,
                    b_rpb, gt, cgt):
    B, L, D = x.shape
    C = xc.shape[1]
    gains = (a_q_gain, a_k_gain, b_q_gain, b_k_gain)
    qkv = qkv_projection(x.reshape(B * L, D), w_in, norm_g, sc, sh, gains, L).reshape(B, L, -1)
    qkv_c = qkv_projection(xc.reshape(B * C, D), w_in, norm_g, csc, csh, gains, None).reshape(B, C, -1)
    ya = window_attention(qkv, qkv_c, a_sink)
    yb = neighbourhood_attention(qkv, qkv_c, b_rpb)
    x_new = linear(ya.reshape(B * L, -1), w_out, x2=yb.reshape(B * L, -1),
                   gated=(x.reshape(B * L, D), gt)).reshape(B, L, D)
    mix_c = context_attention(qkv_c, a_sink)
    xc_new = linear(mix_c.reshape(B * C, -1), w_out, gated=(xc.reshape(B * C, D), cgt)).reshape(B, C, D)
    return x_new, xc_new


def rwkv_layer(x, xc, norm_g, sc, sh, csc, csh, gt, mu, wr, wk, wv, wo, w0, w1, w2, a0, a1, a2, g1, g2, k_k, k_a,
               r_k, lnx_w, lnx_b):
    B, L, D = x.shape
    C = xc.shape[1]
    T = L + 2 * C

    xcat = jnp.concatenate([xc, x, xc], axis=1).reshape(B * T, D)
    nl, na = w1.shape[2], a1.shape[2]
    assert 2 * nl <= RW_TN and 2 * na <= RW_TN and g1.shape[1] <= RW_TN
    proj = rwkv_projections(xcat, norm_g, sc, sh, csc, csh, mu,
                            (wr, wk, wv, g1, jnp.concatenate([w1[0], w1[1]], axis=1),
                             jnp.concatenate([a1[0], a1[1]], axis=1)), T, C)
    rows = lambda w, lo: jnp.pad(w, ((lo, RW_TN - lo - w.shape[0]), (0, 0))).astype(BF16)
    g = linear(proj, rows(g2, 0), act="sigmoid", x_block=(3 * D // RW_TN, RW_TN)).reshape(B, T, D)
    w2p = [rows(w2[0], 0), rows(w2[1], nl)]
    a2p = [rows(a2[0], 0), rows(a2[1], na)]
    vecs = lambda d: jnp.stack([w0[d], a0[d], k_k, k_a, a0[1 - d], r_k.reshape(D), lnx_w, lnx_b]).astype(F32)
    proj = proj.reshape(B, T, -1)
    y_fwd = wkv_scan(proj, w2p[0], a2p[0], vecs(0), ctx_len=C, reverse=False)
    z = wkv_scan(proj, w2p[1], a2p[1], vecs(1), ctx_len=C, reverse=True, a2_other=a2p[0], y_fwd=y_fwd, g=g)
    return linear(z.reshape(B * L, D), wo, gated=(x.reshape(B * L, D), gt)).reshape(B, L, D)


def kernel(x, c, ctx, c_ctx, ada_w, ada_b, norm_mix_g, norm_ffn_g, attn_w_in, attn_w_out, a_q_gain, a_k_gain, a_sink, b_q_gain, b_k_gain, b_rpb, rw_mu, rw_wr, rw_wk, rw_wv, rw_wo, rw_w0, rw_w1, rw_w2, rw_a0, rw_a1, rw_a2, rw_g1, rw_g2, rw_k_k, rw_k_a, rw_r_k, rw_lnx_w, rw_lnx_b, moe_w_grp, moe_w_exp, moe_w1, moe_w3, moe_w2):
    B, L, D = x.shape
    C = ctx.shape[1]
    depth = ada_w.shape[0]
    assert depth == 2, "layer 0 is the attention mixer, layer 1 (last) the RWKV-7 mixer"
    pad = (-(B + 1)) % 8
    cond = jnp.concatenate([jax.nn.silu(c), jax.nn.silu(c_ctx)[None, :], jnp.zeros((pad, D), F32)], axis=0)
    xc = ctx
    for i in range(depth):
        last = i == depth - 1
        j = i // 2
        mod = linear(cond, ada_w, precise=True, layer=i) + ada_b[i]
        sh1, sc1, gt1, sh2, sc2, gt2 = jnp.split(mod[:B], 6, axis=-1)
        csh1, csc1, cgt1, csh2, csc2, cgt2 = jnp.split(mod[B:B + 1], 6, axis=-1)
        if i % 2 == 0:
            x, xc = attention_layer(x, xc, norm_mix_g[i], sc1, sh1, csc1, csh1, attn_w_in[j], attn_w_out[j],
                                    a_q_gain[j], a_k_gain[j], a_sink[j], b_q_gain[j], b_k_gain[j], b_rpb[j],
                                    gt1, cgt1)
        else:
            x = rwkv_layer(x, xc, norm_mix_g[i], sc1, sh1, csc1, csh1, gt1, rw_mu[j], rw_wr[j], rw_wk[j], rw_wv[j],
                           rw_wo[j], rw_w0[j], rw_w1[j], rw_w2[j], rw_a0[j], rw_a1[j], rw_a2[j], rw_g1[j], rw_g2[j],
                           rw_k_k[j], rw_k_a[j], rw_r_k[j], rw_lnx_w[j], rw_lnx_b[j])
        route, hb = moe_router(x.reshape(B * L, D), norm_ffn_g[i], sc2, sh2, moe_w_grp[i], moe_w_exp[i])
        if last:
            ya, yb, gate = hierarchical_moe(route, hb, moe_w1, moe_w3, moe_w2, i)
            x = moe_combine(ya, yb, gate, x.reshape(B * L, D), gt2, 0).reshape(B, L, D)
        else:
            route_c, hb_c = moe_router(xc.reshape(B * C, D), norm_ffn_g[i], csc2, csh2, moe_w_grp[i], moe_w_exp[i])
            ya, yb, gate = hierarchical_moe(jnp.concatenate([route, route_c], axis=0),
                                            jnp.concatenate([hb, hb_c], axis=0), moe_w1, moe_w3, moe_w2, i)
            x = moe_combine(ya, yb, gate, x.reshape(B * L, D), gt2, 0).reshape(B, L, D)
            xc = moe_combine(ya, yb, gate, xc.reshape(B * C, D), cgt2, B * L).reshape(B, C, D)
    return x
```

```python
import functools
import math

import numpy as np
import jax
import jax.numpy as jnp
from jax import lax
from jax.experimental import pallas as pl
from jax.experimental.pallas import tpu as pltpu

GRID_W = 64
HEAD_DIM = 128
A_HEADS = 8
A_KV_HEADS = 2
A_BLOCK = 128
B_HEADS = 8
NA_ROWS = 8
NA_COLS = 16
ROPE_THETA = 10000.0
A_Q_COLS = A_HEADS * HEAD_DIM
A_KV_COLS = A_KV_HEADS * HEAD_DIM
B_COLS = B_HEADS * HEAD_DIM
QKV_COLS = A_Q_COLS + 2 * A_KV_COLS + 3 * B_COLS
MIX_WIDTH = A_Q_COLS + B_COLS
RW_HEAD = 64
RW_LNX_EPS = 64e-5
N_GROUPS = 4
EXPERTS_PER_GROUP = 8
N_EXPERTS = N_GROUPS * EXPERTS_PER_GROUP
TOP_K = 2
NORM_EPS = 1e-6
NEG_INF = -1e30

LANES = 128
VMEM_LIMIT_BYTES = 48 * 1024 * 1024
MOE_ROWS = 256
MOE_SEGMENTS = 4
NB_ROWS = 8
WKV_CHUNK = 64
WKV_PAIRS = 16
QKV_TN = 4 * HEAD_DIM
ROW_TILE = 1024

QA_BLK = 0
KA_BLK = A_Q_COLS // HEAD_DIM
VA_BLK = KA_BLK + A_KV_HEADS
QB_BLK = VA_BLK + A_KV_HEADS
KB_BLK = QB_BLK + B_HEADS
VB_BLK = KB_BLK + B_HEADS

BF16 = jnp.bfloat16
F32 = jnp.float32


def _params(*sem):
    return pltpu.CompilerParams(dimension_semantics=sem, vmem_limit_bytes=VMEM_LIMIT_BYTES)


def _nt(a, b):
    return lax.dot_general(a, b, (((1,), (1,)), ((), ())), preferred_element_type=F32)


def _mm(a, b):
    return jnp.dot(a, b, preferred_element_type=F32)


def _row_tile(want, group_sizes):
    t = want
    while t > 8 and any(g % t for g in group_sizes):
        t //= 2
    assert all(g % t == 0 for g in group_sizes)
    return t


def _rms_modulate(x, g, sc, sh):
    y = x * lax.rsqrt(jnp.mean(x * x, axis=-1, keepdims=True) + NORM_EPS)
    return (y * g) * (1.0 + sc) + sh


def _linear_kernel(*refs, mode, gated, precise, emit_h, layered):
    it = iter(refs)
    x_ref = next(it)
    if mode == "norm":
        g_ref, sc_ref, sh_ref = next(it), next(it), next(it)
    elif mode == "cat2":
        x2_ref = next(it)
    w_ref = next(it)
    if gated:
        res_ref, gate_ref = next(it), next(it)
    o_ref = next(it)
    if emit_h:
        hb_ref = next(it)
    h_ref = next(it)

    @pl.when(pl.program_id(1) == 0)
    def _():
        if mode == "cat2":
            k1 = x_ref.shape[1]
            h_ref[:, 0:k1] = x_ref[...].astype(h_ref.dtype)
            h_ref[:, k1:] = x2_ref[...].astype(h_ref.dtype)
            return
        x = x_ref[...].astype(F32)
        if mode == "norm":
            x = _rms_modulate(x, g_ref[...], sc_ref[0], sh_ref[0])
        h_ref[...] = x.astype(h_ref.dtype)
        if emit_h:
            hb_ref[...] = x.astype(hb_ref.dtype)

    w = w_ref[0] if layered else w_ref[...]
    if precise:
        acc = jnp.dot(h_ref[...], w, preferred_element_type=F32, precision=lax.Precision.HIGHEST)
    else:
        acc = _mm(h_ref[...], w)
    if gated:
        acc = res_ref[...] + gate_ref[0] * acc
    o_ref[...] = acc.astype(o_ref.dtype)


def linear(x, w, *, norm=None, x2=None, gated=None, out_dtype=F32, precise=False, emit_h=False, layer=None,
           tm=ROW_TILE, tn=512):
    M, K = x.shape
    N = w.shape[-1]
    groups = [M] + ([M // norm[1].shape[0]] if norm is not None else []) + (
        [M // gated[1].shape[0]] if gated is not None else [])
    tm = _row_tile(tm, groups)
    tn = tn if N % tn == 0 else N
    assert N % tn == 0
    mode = "norm" if norm is not None else "cat2" if x2 is not None else "plain"
    wdt = F32 if precise else BF16
    args = [x]
    in_specs = [pl.BlockSpec((tm, K), lambda i, j: (i, 0))]
    vec = lambda t: t.reshape(1, -1).astype(F32)
    vec_spec = lambda k: pl.BlockSpec((1, k), lambda i, j: (0, 0))
    if mode == "norm":
        g, sc, sh = norm
        nb = sc.shape[0]
        rpb = M // nb
        assert rpb % tm == 0
        args += [vec(g), sc.reshape(nb, 1, K).astype(F32), sh.reshape(nb, 1, K).astype(F32)]
        in_specs += [vec_spec(K),
                     pl.BlockSpec((1, 1, K), lambda i, j: ((i * tm) // rpb, 0, 0)),
                     pl.BlockSpec((1, 1, K), lambda i, j: ((i * tm) // rpb, 0, 0))]
    elif mode == "cat2":
        args.append(x2)
        in_specs.append(pl.BlockSpec((tm, x2.shape[1]), lambda i, j: (i, 0)))
        K = K + x2.shape[1]
    assert w.shape[-2] == K
    args.append(w.astype(wdt))
    if layer is None:
        in_specs.append(pl.BlockSpec((K, tn), lambda i, j: (0, j)))
    else:
        in_specs.append(pl.BlockSpec((1, K, tn), lambda i, j: (layer, 0, j)))
    if gated is not None:
        res, gate = gated
        nbg = gate.shape[0]
        rpg = M // nbg
        assert rpg % tm == 0
        args += [res, gate.reshape(nbg, 1, N).astype(F32)]
        in_specs += [pl.BlockSpec((tm, tn), lambda i, j: (i, j)),
                     pl.BlockSpec((1, 1, tn), lambda i, j: ((i * tm) // rpg, 0, j))]
    out_shape = jax.ShapeDtypeStruct((M, N), out_dtype)
    out_specs = pl.BlockSpec((tm, tn), lambda i, j: (i, j))
    if emit_h:
        out_shape = (out_shape, jax.ShapeDtypeStruct((M, K), BF16))
        out_specs = (out_specs, pl.BlockSpec((tm, K), lambda i, j: (i, 0)))
    return pl.pallas_call(
        functools.partial(_linear_kernel, mode=mode, gated=gated is not None, precise=precise, emit_h=emit_h,
                          layered=layer is not None),
        out_shape=out_shape,
        grid=(M // tm, N // tn),
        in_specs=in_specs,
        out_specs=out_specs,
        scratch_shapes=[pltpu.VMEM((tm, K), wdt)],
        compiler_params=_params("parallel", "arbitrary"),
        name="linear_" + mode,
    )(*args)


RW_TN = 512
RW_ROW_TILES = (768, 512, 256, 128, 64, 32, 16, 8)


def _rwkv_proj_kernel(x_ref, prev_ref, next_ref, g_ref, sc_ref, sh_ref, csc_ref, csh_ref, mu_ref, w_ref,
                      o_ref, hs_ref, xs_ref, h_ref, *, t_len, c_len, wide_tiles):
    j = pl.program_id(1)
    tm = x_ref.shape[0]

    @pl.when(j == 0)
    def _():
        t0 = (pl.program_id(0) * tm) % t_len

        def hmod(rows, t):
            is_ctx = (t < c_len) | (t >= t_len - c_len)
            return _rms_modulate(rows, g_ref[...], jnp.where(is_ctx, csc_ref[...], sc_ref[0]),
                                 jnp.where(is_ctx, csh_ref[...], sh_ref[0]))

        row = lax.broadcasted_iota(jnp.int32, (tm, 1), 0)
        row8 = lax.broadcasted_iota(jnp.int32, (8, 1), 0)
        t = t0 + row
        h = hmod(x_ref[...].astype(F32), t)
        h_before = hmod(prev_ref[...].astype(F32), t0 - 8 + row8)[7:8]
        h_after = hmod(next_ref[...].astype(F32), t0 + tm + row8)[0:1]
        up = jnp.where(row == 0, h_before, pltpu.roll(h, 1, 0))
        dn = jnp.where(row == tm - 1, h_after, pltpu.roll(h, tm - 1, 0))
        has_prev = (t != 0) & (t != c_len) & (t != t_len - c_len)
        has_next = (t != c_len - 1) & (t != t_len - c_len - 1) & (t != t_len - 1)
        hs_ref[...] = h
        xs_ref[...] = 0.5 * (jnp.where(has_prev, up, 0.0) + jnp.where(has_next, dn, 0.0)) - h

    group = jnp.where(j < 3 * wide_tiles, j // wide_tiles, j - 3 * wide_tiles + 3)

    @pl.when((j % wide_tiles == 0) | (j >= 3 * wide_tiles))
    def _():
        h_ref[...] = (hs_ref[...] + xs_ref[...] * mu_ref[pl.ds(group, 1), :]).astype(h_ref.dtype)

    o_ref[...] = _mm(h_ref[...], w_ref[...])


def rwkv_projections(xcat, norm_g, sc, sh, csc, csh, mu, weights, t_len, c_len):
    M, K = xcat.shape
    nb = sc.shape[0]
    tm = next(t for t in RW_ROW_TILES if t_len % t == 0)
    assert M == nb * t_len and K % RW_TN == 0
    wr, wk, wv, g1, w1c, a1c = weights
    padw = lambda w: jnp.pad(w, ((0, 0), (0, RW_TN - w.shape[1])))
    w_all = jnp.concatenate([wr, wk, wv, padw(g1), padw(w1c), padw(a1c)], axis=1).astype(BF16)
    mu_g = jnp.pad(mu[jnp.array([0, 2, 3, 5, 1, 4])].astype(F32), ((0, 2), (0, 0)))
    n_out = w_all.shape[1]
    per = tm // 8
    vec = lambda t: t.reshape(1, -1).astype(F32)
    vec_spec = pl.BlockSpec((1, K), lambda i, j: (0, 0))
    mod_spec = pl.BlockSpec((1, 1, K), lambda i, j: ((i * tm) // t_len, 0, 0))
    return pl.pallas_call(
        functools.partial(_rwkv_proj_kernel, t_len=t_len, c_len=c_len, wide_tiles=K // RW_TN),
        out_shape=jax.ShapeDtypeStruct((M, n_out), F32),
        grid=(M // tm, n_out // RW_TN),
        in_specs=[pl.BlockSpec((tm, K), lambda i, j: (i, 0)),
                  pl.BlockSpec((8, K), lambda i, j: (jnp.maximum(i * per - 1, 0), 0)),
                  pl.BlockSpec((8, K), lambda i, j: (jnp.minimum((i + 1) * per, M // 8 - 1), 0)),
                  vec_spec, mod_spec, mod_spec, vec_spec, vec_spec,
                  pl.BlockSpec((8, K), lambda i, j: (0, 0)),
                  pl.BlockSpec((K, RW_TN), lambda i, j: (0, j))],
        out_specs=pl.BlockSpec((tm, RW_TN), lambda i, j: (i, j)),
        scratch_shapes=[pltpu.VMEM((tm, K), F32), pltpu.VMEM((tm, K), F32), pltpu.VMEM((tm, K), BF16)],
        compiler_params=_params("parallel", "arbitrary"),
        name="rwkv_projections",
    )(xcat, xcat, xcat, vec(norm_g), sc.reshape(nb, 1, K).astype(F32), sh.reshape(nb, 1, K).astype(F32),
      vec(csc), vec(csh), mu_g, w_all)


def _qkv_kernel(x_ref, g_ref, sc_ref, sh_ref, w_ref, gain_ref, cos_ref, sin_ref, o_ref, h_ref, *, rope):
    j = pl.program_id(1)

    @pl.when(j == 0)
    def _():
        h_ref[...] = _rms_modulate(x_ref[...].astype(F32), g_ref[...], sc_ref[0], sh_ref[0]).astype(h_ref.dtype)

    acc = _mm(h_ref[...], w_ref[...])
    tm = acc.shape[0]

    def emit(kinds):
        for hh, kind in enumerate(kinds):
            sl = slice(hh * HEAD_DIM, (hh + 1) * HEAD_DIM)
            y = acc[:, sl]
            if kind != "plain":
                y = y * lax.rsqrt(jnp.mean(y * y, axis=-1, keepdims=True) + NORM_EPS) * gain_ref[:, sl]
                if kind == "rope" and rope:
                    lane = lax.broadcasted_iota(jnp.int32, (tm, HEAD_DIM), 1)
                    first = (lane % (HEAD_DIM // 2)) < HEAD_DIM // 4
                    partner = jnp.where(first, pltpu.roll(y, HEAD_DIM - HEAD_DIM // 4, 1),
                                        pltpu.roll(y, HEAD_DIM // 4, 1))
                    y = y * cos_ref[...] + partner * sin_ref[...]
            o_ref[:, sl] = y.astype(o_ref.dtype)

    per = QKV_TN // HEAD_DIM
    kinds = (["rope"] * A_HEADS + ["rope"] * A_KV_HEADS + ["plain"] * A_KV_HEADS + ["norm"] * B_HEADS
             + ["norm"] * B_HEADS + ["plain"] * B_HEADS)
    tiles = [tuple(kinds[t * per:(t + 1) * per]) for t in range(len(kinds) // per)]
    for kind_set in sorted(set(tiles)):
        cond = None
        for t, ks in enumerate(tiles):
            if ks == kind_set:
                cond = (j == t) if cond is None else (cond | (j == t))
        pl.when(cond)(functools.partial(emit, kind_set))


def qkv_projection(x, w_in, norm_g, sc, sh, gains, rope_len):
    M, K = x.shape
    N = w_in.shape[1]
    assert N == QKV_COLS and N % QKV_TN == 0
    nb = sc.shape[0]
    rpb = M // nb
    tm = _row_tile(ROW_TILE, [rpb] + ([rope_len] if rope_len is not None else []))
    a_q, a_k, b_q, b_k = (t.astype(F32) for t in gains)
    ones = jnp.ones((HEAD_DIM,), F32)
    gain = jnp.concatenate([jnp.tile(a_q, A_HEADS), jnp.tile(a_k, A_KV_HEADS), jnp.tile(ones, A_KV_HEADS),
                            jnp.tile(b_q, B_HEADS), jnp.tile(b_k, B_HEADS), jnp.tile(ones, B_HEADS)]).reshape(1, N)
    rope = rope_len is not None
    if rope:
        assert rope_len % tm == 0
        t = jnp.arange(rope_len)
        quarter = HEAD_DIM // 4
        inv_freq = ROPE_THETA ** (-jnp.arange(quarter, dtype=F32) / quarter)
        ar = (t // GRID_W).astype(F32)[:, None] * inv_freq[None, :]
        ac = (t % GRID_W).astype(F32)[:, None] * inv_freq[None, :]
        cos_t = jnp.concatenate([jnp.cos(ar), jnp.cos(ar), jnp.cos(ac), jnp.cos(ac)], axis=1)
        sin_t = jnp.concatenate([-jnp.sin(ar), jnp.sin(ar), -jnp.sin(ac), jnp.sin(ac)], axis=1)
        nt = rope_len // tm
    else:
        cos_t = jnp.ones((tm, HEAD_DIM), F32)
        sin_t = jnp.zeros((tm, HEAD_DIM), F32)
        nt = 1
    tab_spec = pl.BlockSpec((tm, HEAD_DIM), lambda i, j: (i % nt, 0))
    return pl.pallas_call(
        functools.partial(_qkv_kernel, rope=rope),
        out_shape=jax.ShapeDtypeStruct((M, N), BF16),
        grid=(M // tm, N // QKV_TN),
        in_specs=[pl.BlockSpec((tm, K), lambda i, j: (i, 0)),
                  pl.BlockSpec((1, K), lambda i, j: (0, 0)),
                  pl.BlockSpec((1, 1, K), lambda i, j: ((i * tm) // rpb, 0, 0)),
                  pl.BlockSpec((1, 1, K), lambda i, j: ((i * tm) // rpb, 0, 0)),
                  pl.BlockSpec((K, QKV_TN), lambda i, j: (0, j)),
                  pl.BlockSpec((1, QKV_TN), lambda i, j: (0, j)),
                  tab_spec, tab_spec],
        out_specs=pl.BlockSpec((tm, QKV_TN), lambda i, j: (i, j)),
        scratch_shapes=[pltpu.VMEM((tm, K), BF16)],
        compiler_params=_params("parallel", "arbitrary"),
        name="qkv_projection",
    )(x, norm_g.reshape(1, K).astype(F32), sc.reshape(nb, 1, K).astype(F32), sh.reshape(nb, 1, K).astype(F32),
      w_in.astype(BF16), gain, cos_t, sin_t)


def _win_attn_kernel(q_ref, kp_ref, kc_ref, kn_ref, vp_ref, vc_ref, vn_ref, kx_ref, vx_ref, sink_ref,
                     o_ref, *, nblk, groups, scale):
    n = pl.program_id(2)
    blk = A_BLOCK
    keys = jnp.concatenate([kp_ref[0], kc_ref[0], kn_ref[0], kx_ref[0]], axis=0)
    vals = jnp.concatenate([vp_ref[0], vc_ref[0], vn_ref[0], vx_ref[0]], axis=0)
    nkeys = keys.shape[0]
    qi = lax.broadcasted_iota(jnp.int32, (blk, nkeys), 0)
    kj = lax.broadcasted_iota(jnp.int32, (blk, nkeys), 1)
    ok = (((kj >= blk) | ((kj >= qi) & (n > 0)))
          & ((kj < 2 * blk) | (kj >= 3 * blk) | ((kj - 2 * blk <= qi) & (n < nblk - 1))))
    G = range(groups)
    hsl = [slice(g * HEAD_DIM, (g + 1) * HEAD_DIM) for g in G]
    sink = [sink_ref[0, g * blk:(g + 1) * blk, 0:1] for g in G]
    s = [jnp.where(ok, _nt(q_ref[0, :, hsl[g]], keys) * scale, NEG_INF) for g in G]
    m = [jnp.maximum(jnp.max(s[g], axis=-1, keepdims=True), sink[g]) for g in G]
    p = [jnp.exp(s[g] - m[g]) for g in G]
    den = [jnp.sum(p[g], axis=-1, keepdims=True) + jnp.exp(sink[g] - m[g]) for g in G]
    o = [_mm(p[g].astype(BF16), vals) / den[g] for g in G]
    for g in G:
        o_ref[0, :, hsl[g]] = o[g].astype(o_ref.dtype)


def window_attention(qkv, qkv_c, sink):
    B, L, _ = qkv.shape
    C = qkv_c.shape[1]
    groups = A_HEADS // A_KV_HEADS
    nblk = L // A_BLOCK
    gw = groups * HEAD_DIM
    sink_t = jnp.broadcast_to(
        jnp.repeat(sink.astype(F32).reshape(A_KV_HEADS, groups), A_BLOCK, axis=1)[:, :, None],
        (A_KV_HEADS, groups * A_BLOCK, LANES))
    blkspec = lambda f: pl.BlockSpec((1, A_BLOCK, HEAD_DIM), f)
    prev = lambda c0: (lambda b, h, n: (b, jnp.maximum(n - 1, 0), c0 + h))
    cur = lambda c0: (lambda b, h, n: (b, n, c0 + h))
    nxt = lambda c0: (lambda b, h, n: (b, jnp.minimum(n + 1, nblk - 1), c0 + h))
    ctx = lambda c0: pl.BlockSpec((1, C, HEAD_DIM), lambda b, h, n: (b, 0, c0 + h))
    return pl.pallas_call(
        functools.partial(_win_attn_kernel, nblk=nblk, groups=groups, scale=HEAD_DIM ** -0.5),
        out_shape=jax.ShapeDtypeStruct((B, L, A_Q_COLS), BF16),
        grid=(B, A_KV_HEADS, nblk),
        in_specs=[pl.BlockSpec((1, A_BLOCK, gw), lambda b, h, n: (b, n, h)),
                  blkspec(prev(KA_BLK)), blkspec(cur(KA_BLK)), blkspec(nxt(KA_BLK)),
                  blkspec(prev(VA_BLK)), blkspec(cur(VA_BLK)), blkspec(nxt(VA_BLK)),
                  ctx(KA_BLK), ctx(VA_BLK),
                  pl.BlockSpec((1, groups * A_BLOCK, LANES), lambda b, h, n: (h, 0, 0))],
        out_specs=pl.BlockSpec((1, A_BLOCK, gw), lambda b, h, n: (b, n, h)),
        compiler_params=_params("parallel", "parallel", "arbitrary"),
        name="window_attention",
    )(qkv, qkv, qkv, qkv, qkv, qkv, qkv, qkv_c, qkv_c, sink_t)


def _nbr_attn_kernel(q_ref, kp_ref, kc_ref, kn_ref, vp_ref, vc_ref, vn_ref, kx_ref, vx_ref, bias_ref,
                     o_ref, kbuf, vbuf, *, grid_rows, scale):
    rb = pl.program_id(2)
    tok = NB_ROWS * GRID_W
    nk = NA_ROWS * GRID_W
    nctx = kx_ref.shape[1]
    kbuf[0:tok] = kp_ref[0]
    kbuf[tok:2 * tok] = kc_ref[0]
    kbuf[2 * tok:3 * tok] = kn_ref[0]
    vbuf[0:tok] = vp_ref[0]
    vbuf[tok:2 * tok] = vc_ref[0]
    vbuf[2 * tok:3 * tok] = vn_ref[0]
    kx = kx_ref[0]
    vx = vx_ref[0]
    sx_all = _nt(q_ref[0], kx) * scale
    R = range(NB_ROWS)
    rsl = [slice(rr * GRID_W, (rr + 1) * GRID_W) for rr in R]
    r = [rb * NB_ROWS + rr for rr in R]
    r0 = [jnp.clip(r[rr] - NA_ROWS // 2, 0, grid_rows - NA_ROWS) for rr in R]
    off = [pl.multiple_of((r0[rr] - (rb - 1) * NB_ROWS) * GRID_W, GRID_W) for rr in R]
    s = [jnp.concatenate([_nt(q_ref[0, rsl[rr], :], kbuf[pl.ds(off[rr], nk), :]) * scale
                          + bias_ref[0, r0[rr] - r[rr] + NA_ROWS - 1], sx_all[rsl[rr]]], axis=1)
         for rr in R]
    m = [jnp.max(s[rr], axis=-1, keepdims=True) for rr in R]
    p = [jnp.exp(s[rr] - m[rr]) for rr in R]
    den = [jnp.sum(p[rr], axis=-1, keepdims=True) for rr in R]
    pb = [p[rr].astype(BF16) for rr in R]
    o = [(_mm(pb[rr][:, 0:nk], vbuf[pl.ds(off[rr], nk), :]) + _mm(pb[rr][:, nk:nk + nctx], vx)) / den[rr]
         for rr in R]
    for rr in R:
        o_ref[0, rsl[rr], :] = o[rr].astype(o_ref.dtype)


def _nbr_bias_table(rpb):
    col = np.arange(GRID_W)
    c0 = np.clip(col - NA_COLS // 2, 0, GRID_W - NA_COLS)
    col_ok = (col[None, :] >= c0[:, None]) & (col[None, :] < c0[:, None] + NA_COLS)
    dc = np.clip(col[None, :] - col[:, None] + (NA_COLS - 1), 0, 2 * NA_COLS - 2)
    dr = np.arange(NA_ROWS)[:, None] + np.arange(NA_ROWS)[None, :]
    sel_r = np.eye(2 * NA_ROWS - 1, dtype=np.float32)[dr]
    sel_c = np.eye(2 * NA_COLS - 1, dtype=np.float32)[dc]
    t = jnp.einsum('djr,hrc->hdjc', sel_r, rpb.astype(F32), precision=lax.Precision.HIGHEST)
    t = jnp.einsum('hdjc,qkc->hdqjk', t, sel_c, precision=lax.Precision.HIGHEST)
    t = jnp.where(col_ok[None, None, :, None, :], t, NEG_INF)
    return t.reshape(rpb.shape[0], NA_ROWS, GRID_W, NA_ROWS * GRID_W)


def neighbourhood_attention(qkv, qkv_c, rpb):
    B, L, _ = qkv.shape
    C = qkv_c.shape[1]
    grid_rows = L // GRID_W
    tok = NB_ROWS * GRID_W
    assert grid_rows >= NA_ROWS and L % tok == 0 and NB_ROWS == NA_ROWS
    nrb = L // tok
    bias = _nbr_bias_table(rpb)
    spec = lambda f: pl.BlockSpec((1, tok, HEAD_DIM), f)
    prev = lambda c0: (lambda b, h, n: (b, jnp.maximum(n - 1, 0), c0 + h))
    cur = lambda c0: (lambda b, h, n: (b, n, c0 + h))
    nxt = lambda c0: (lambda b, h, n: (b, jnp.minimum(n + 1, nrb - 1), c0 + h))
    ctx = lambda c0: pl.BlockSpec((1, C, HEAD_DIM), lambda b, h, n: (b, 0, c0 + h))
    return pl.pallas_call(
        functools.partial(_nbr_attn_kernel, grid_rows=grid_rows, scale=HEAD_DIM ** -0.5),
        out_shape=jax.ShapeDtypeStruct((B, L, B_COLS), BF16),
        grid=(B, B_HEADS, nrb),
        in_specs=[spec(cur(QB_BLK)), spec(prev(KB_BLK)), spec(cur(KB_BLK)), spec(nxt(KB_BLK)),
                  spec(prev(VB_BLK)), spec(cur(VB_BLK)), spec(nxt(VB_BLK)),
                  ctx(KB_BLK), ctx(VB_BLK),
                  pl.BlockSpec((1, NA_ROWS, GRID_W, NA_ROWS * GRID_W), lambda b, h, n: (h, 0, 0, 0))],
        out_specs=spec(cur(0)),
        scratch_shapes=[pltpu.VMEM((3 * tok, HEAD_DIM), BF16), pltpu.VMEM((3 * tok, HEAD_DIM), BF16)],
        compiler_params=_params("parallel", "parallel", "arbitrary"),
        name="neighbourhood_attention",
    )(qkv, qkv, qkv, qkv, qkv, qkv, qkv, qkv_c, qkv_c, bias)


def _ctx_attn_kernel(q_ref, k_ref, v_ref, sink_ref, o_ref, *, scale):
    s = _nt(q_ref[0], k_ref[0]) * scale
    sink = sink_ref[0][:1, :1]
    m = jnp.maximum(jnp.max(s, axis=-1, keepdims=True), sink)
    p = jnp.exp(s - m)
    den = jnp.sum(p, axis=-1, keepdims=True) + jnp.exp(sink - m)
    o_ref[0] = (_mm(p.astype(BF16), v_ref[0]) / den).astype(o_ref.dtype)


def context_attention(qkv_c, a_sink):
    B, C, _ = qkv_c.shape
    grp = A_HEADS // A_KV_HEADS
    heads = A_HEADS + B_HEADS
    sink = jnp.concatenate([a_sink.astype(F32), jnp.full((B_HEADS,), NEG_INF, F32)])
    sink_t = jnp.broadcast_to(sink[:, None, None], (heads, 8, LANES))
    is_a = lambda h: h < A_HEADS
    qcol = lambda h: jnp.where(is_a(h), QA_BLK + h, QB_BLK + h - A_HEADS)
    kcol = lambda h: jnp.where(is_a(h), KA_BLK + h // grp, KB_BLK + h - A_HEADS)
    vcol = lambda h: jnp.where(is_a(h), VA_BLK + h // grp, VB_BLK + h - A_HEADS)
    return pl.pallas_call(
        functools.partial(_ctx_attn_kernel, scale=HEAD_DIM ** -0.5),
        out_shape=jax.ShapeDtypeStruct((B, C, MIX_WIDTH), BF16),
        grid=(B, heads),
        in_specs=[pl.BlockSpec((1, C, HEAD_DIM), lambda b, h: (b, 0, qcol(h))),
                  pl.BlockSpec((1, C, HEAD_DIM), lambda b, h: (b, 0, kcol(h))),
                  pl.BlockSpec((1, C, HEAD_DIM), lambda b, h: (b, 0, vcol(h))),
                  pl.BlockSpec((1, 8, LANES), lambda b, h: (h, 0, 0))],
        out_specs=pl.BlockSpec((1, C, HEAD_DIM), lambda b, h: (b, 0, h)),
        compiler_params=_params("parallel", "parallel"),
        name="context_attention",
    )(qkv_c, qkv_c, qkv_c, sink_t)


def _wkv_pairs(ins, cum, sts, strict_s, incl_s, lane_lo, bd):
    C = ins[0][0].shape[0]
    P = range(len(ins))
    steps = int(math.log2(C))
    cat = jnp.concatenate

    def bdw(x):
        return jnp.where(bd, cat([x, x], axis=0), 0.0).astype(BF16)

    tot = [jnp.sum(ins[p][1], axis=0, keepdims=True) for p in P]
    g_inv = [jnp.exp(-cum[p]) for p in P]
    lhs = [cat([ins[p][4] * jnp.exp(cum[p] - ins[p][1]), ins[p][0] * jnp.exp(cum[p])], axis=0).astype(BF16)
           for p in P]
    b_t = [ins[p][5] * g_inv[p] for p in P]
    k_t = [ins[p][2] * g_inv[p] for p in P]
    rhs = [cat([jnp.where(lane_lo, b_t[p], 0.0), jnp.where(lane_lo, 0.0, b_t[p]),
                jnp.where(lane_lo, k_t[p], 0.0), jnp.where(lane_lo, 0.0, k_t[p])], axis=0).astype(BF16) for p in P]
    prod = [_nt(lhs[p], rhs[p]) for p in P]
    sh = [_nt(lhs[p], sts[p].astype(BF16)) for p in P]
    ns = [jnp.where(strict_s, prod[p][0:C, 0:2 * C], 0.0) for p in P]
    n_k = [cat([jnp.where(strict_s, prod[p][0:C, 2 * C:4 * C], 0.0),
                jnp.where(incl_s, prod[p][C:2 * C, 2 * C:4 * C], 0.0)], axis=0).astype(BF16) for p in P]
    n_rb = [jnp.where(incl_s, prod[p][C:2 * C, 0:2 * C], 0.0).astype(BF16) for p in P]
    kv = [_mm(n_k[p], bdw(ins[p][3])) for p in P]
    x = [sh[p][0:C] + kv[p][0:C] for p in P]
    for i in range(steps):
        nsb = [ns[p].astype(BF16) for p in P]
        if i + 1 < steps:
            z = [_mm(nsb[p], cat([bdw(x[p]), bdw(ns[p])], axis=1)) for p in P]
            x = [x[p] + z[p][:, 0:LANES] for p in P]
            ns = [z[p][:, LANES:2 * LANES] for p in P]
        else:
            x = [x[p] + _mm(nsb[p], bdw(x[p])) for p in P]
    y = [sh[p][C:2 * C] + kv[p][C:2 * C] + _mm(n_rb[p], bdw(x[p])) for p in P]
    uv_t = [cat([x[p], ins[p][3]], axis=0).T.astype(BF16) for p in P]
    g_tail = [jnp.exp(tot[p] - cum[p]) for p in P]
    bk = [cat([ins[p][5] * g_tail[p], ins[p][2] * g_tail[p]], axis=0).astype(BF16) for p in P]
    st_new = [sts[p] * jnp.exp(tot[p]) + jnp.where(bd, _mm(uv_t[p], bk[p]), 0.0) for p in P]
    return y, st_new


def _cumsum_time(x, reverse):
    C = x.shape[0]
    ti = lax.broadcasted_iota(jnp.int32, (C, C), 0)
    si = lax.broadcasted_iota(jnp.int32, (C, C), 1)
    tri = jnp.where((si >= ti) if reverse else (si <= ti), 1.0, 0.0).astype(BF16)
    hi = x.astype(BF16)
    rem = x - hi.astype(F32)
    mid = rem.astype(BF16)
    lo = (rem - mid.astype(F32)).astype(BF16)
    return _mm(tri, hi) + _mm(tri, mid) + _mm(tri, lo)


def _head_sum(x, lane_lo):
    lo = jnp.sum(jnp.where(lane_lo, x, 0.0), axis=-1, keepdims=True)
    hi = jnp.sum(jnp.where(lane_lo, 0.0, x), axis=-1, keepdims=True)
    return jnp.where(lane_lo, lo, hi)


_V_W0, _V_A0, _V_KK, _V_KA, _V_A0_OTHER, _V_RK, _V_LNW, _V_LNB = range(8)


def _wkv_kernel(*refs, reverse, pairs):
    C = WKV_CHUNK
    if reverse:
        (r_ref, k_ref, v_ref, wl_ref, al_ref, w2_ref, a2_ref, vec_ref, a2o_ref, yf_ref, gh_ref, g2_ref,
         o_ref, st_ref) = refs
    else:
        r_ref, k_ref, v_ref, wl_ref, al_ref, w2_ref, a2_ref, vec_ref, o_ref, st_ref = refs

    @pl.when(pl.program_id(2) == 0)
    def _():
        st_ref[...] = jnp.zeros_like(st_ref)

    tt = lax.broadcasted_iota(jnp.int32, (C, 2 * C), 0)
    ss = lax.broadcasted_iota(jnp.int32, (C, 2 * C), 1) % C
    incl_s = (ss >= tt) if reverse else (ss <= tt)
    strict_s = (ss > tt) if reverse else (ss < tt)
    lane_lo = lax.broadcasted_iota(jnp.int32, (C, LANES), 1) < RW_HEAD
    bd = (lax.broadcasted_iota(jnp.int32, (LANES, LANES), 0) // RW_HEAD) == (
        lax.broadcasted_iota(jnp.int32, (LANES, LANES), 1) // RW_HEAD)
    sls = [slice(p * LANES, (p + 1) * LANES) for p in range(pairs)]
    vec = lambda row, sl: vec_ref[row:row + 1, sl]

    wl = jnp.tanh(wl_ref[0]).astype(BF16)
    al = al_ref[0].astype(BF16)
    w_raw = vec_ref[_V_W0:_V_W0 + 1, :] + _mm(wl, w2_ref[...])
    lw_all = -math.exp(-0.5) * jax.nn.sigmoid(w_raw)
    cum_all = _cumsum_time(lw_all, reverse)
    ag_all = jax.nn.sigmoid(vec_ref[_V_A0:_V_A0 + 1, :] + _mm(al, a2_ref[...]))
    if reverse:
        ag_o_all = jax.nn.sigmoid(vec_ref[_V_A0_OTHER:_V_A0_OTHER + 1, :] + _mm(al, a2o_ref[...]))
        gate_all = _mm(jax.nn.sigmoid(gh_ref[0]).astype(BF16), g2_ref[...])
    ins = []
    for sl in sls:
        lw, ag = lw_all[:, sl], ag_all[:, sl]
        k_raw = k_ref[0, :, sl]
        kkf = k_raw * vec(_V_KK, sl)
        kk = kkf * lax.rsqrt(_head_sum(kkf * kkf, lane_lo) + 1e-12)
        kd = k_raw * (1.0 + (ag - 1.0) * vec(_V_KA, sl))
        ins.append((r_ref[0, :, sl], lw, kd, v_ref[0, :, sl], -kk, kk * ag))
    y, st_new = _wkv_pairs(ins, [cum_all[:, sl] for sl in sls], [st_ref[p] for p in range(pairs)],
                           strict_s, incl_s, lane_lo, bd)
    for p, sl in enumerate(sls):
        st_ref[p] = st_new[p]
        if not reverse:
            o_ref[0, :, sl] = y[p]
            continue
        r, _, kd, v, _, _ = ins[p]
        k_sum = kd + k_ref[0, :, sl] * (1.0 + (ag_o_all[:, sl] - 1.0) * vec(_V_KA, sl))
        bonus = _head_sum(r * k_sum * vec(_V_RK, sl), lane_lo) * v
        ysum = yf_ref[0, :, sl] + y[p]
        mean = _head_sum(ysum, lane_lo) * (1.0 / RW_HEAD)
        dev = ysum - mean
        var = _head_sum(dev * dev, lane_lo) * (1.0 / RW_HEAD)
        yn = dev * lax.rsqrt(var + RW_LNX_EPS) * vec(_V_LNW, sl) + vec(_V_LNB, sl)
        o_ref[0, :, sl] = ((yn + bonus) * gate_all[:, sl]).astype(o_ref.dtype)


def wkv_scan(proj, w2, a2, vecs, *, ctx_len, reverse, a2_other=None, y_fwd=None, g2=None):
    B, T, _ = proj.shape
    D = vecs.shape[1]
    C = WKV_CHUNK
    L = T - 2 * ctx_len
    assert ctx_len % C == 0 and L % C == 0 and 2 * C == LANES
    nchunk = (ctx_len + L) // C
    cc = ctx_len // C
    nl = L // C
    pairs = min(WKV_PAIRS, D // LANES)
    width = pairs * LANES
    assert D % width == 0
    nd = D // width
    if reverse:
        tchunk = lambda ci: cc + nchunk - 1 - ci
    else:
        tchunk = lambda ci: ci
    lchunk = lambda ci: jnp.clip(tchunk(ci) - cc, 0, nl - 1)
    pspec = lambda m: pl.BlockSpec((1, C, width), lambda bi, di, ci: (bi, tchunk(ci), m * nd + di))
    hid0 = 3 * D // RW_TN
    hspec = lambda m: pl.BlockSpec((1, C, RW_TN), lambda bi, di, ci: (bi, tchunk(ci), hid0 + m))
    lspec = pl.BlockSpec((1, C, width), lambda bi, di, ci: (bi, lchunk(ci), di))
    wspec = lambda rows: pl.BlockSpec((rows, width), lambda bi, di, ci: (0, di))
    args = [proj, proj, proj, proj, proj, w2, a2, vecs]
    in_specs = [pspec(0), pspec(1), pspec(2), hspec(1), hspec(2), wspec(RW_TN), wspec(RW_TN),
                wspec(vecs.shape[0])]
    if reverse:
        args += [a2_other, y_fwd, proj, g2]
        in_specs += [wspec(RW_TN), lspec, hspec(0), wspec(RW_TN)]
    return pl.pallas_call(
        functools.partial(_wkv_kernel, reverse=reverse, pairs=pairs),
        out_shape=jax.ShapeDtypeStruct((B, L, D), BF16 if reverse else F32),
        grid=(B, nd, nchunk),
        in_specs=in_specs,
        out_specs=lspec,
        scratch_shapes=[pltpu.VMEM((pairs, LANES, LANES), F32)],
        compiler_params=_params("parallel", "parallel", "arbitrary"),
        name="wkv_scan_bwd" if reverse else "wkv_scan_fwd",
    )(*args)


def _moe_ffn_kernel(be_ref, nused_ref, x_ref, w1_ref, w3_ref, w2_ref, y_in_ref, o_ref, w1b, w3b, w2b, *, block0):
    del y_in_ref
    i = pl.program_id(0)
    blk = block0 + i
    used = blk < nused_ref[0]

    @pl.when(used & ((i == 0) | (be_ref[blk] != be_ref[jnp.maximum(blk - 1, 0)])))
    def _():
        w1b[...] = w1_ref[0, 0].astype(BF16)
        w3b[...] = w3_ref[0, 0].astype(BF16)
        w2b[...] = w2_ref[0, 0].astype(BF16)

    @pl.when(used)
    def _():
        half = MOE_ROWS // 2
        sls = [slice(0, half), slice(half, MOE_ROWS)]
        x = [x_ref[sl, :] for sl in sls]
        h1 = [_mm(xh, w1b[...]) for xh in x]
        h3 = [_mm(xh, w3b[...]) for xh in x]
        hh = [((a * jax.nn.sigmoid(a)) * b).astype(BF16) for a, b in zip(h1, h3)]
        y = [_mm(h, w2b[...]) for h in hh]
        for sl, yh in zip(sls, y):
            o_ref[sl, :] = yh.astype(o_ref.dtype)

    @pl.when(jnp.logical_not(used))
    def _():
        o_ref[...] = jnp.zeros_like(o_ref)


def moe_ffn(x_rows, blk_expert, n_used, w1, w3, w2, layer, block0, y):
    rows, D = x_rows.shape
    F = w1.shape[3]
    wspec = lambda shape: pl.BlockSpec((1, 1) + shape, lambda i, be, nu: (layer, be[block0 + i], 0, 0))
    grid_spec = pltpu.PrefetchScalarGridSpec(
        num_scalar_prefetch=2,
        grid=(rows // MOE_ROWS,),
        in_specs=[pl.BlockSpec((MOE_ROWS, D), lambda i, be, nu: (i, 0)), wspec((D, F)), wspec((D, F)),
                  wspec((F, D)), pl.BlockSpec(memory_space=pl.ANY)],
        out_specs=pl.BlockSpec((MOE_ROWS, D), lambda i, be, nu: (block0 + i, 0)),
        scratch_shapes=[pltpu.VMEM((D, F), BF16), pltpu.VMEM((D, F), BF16), pltpu.VMEM((F, D), BF16)],
    )
    return pl.pallas_call(
        functools.partial(_moe_ffn_kernel, block0=block0),
        out_shape=jax.ShapeDtypeStruct(y.shape, y.dtype),
        grid_spec=grid_spec,
        input_output_aliases={6: 0},
        compiler_params=_params("arbitrary"),
        name="moe_ffn",
    )(blk_expert, n_used, x_rows, w1.astype(F32), w3.astype(F32), w2.astype(F32), y)


def hierarchical_moe(route, hb, w1, w3, w2, layer):
    n, D = hb.shape
    p_grp = jax.nn.softmax(route[:, :N_GROUPS], axis=-1)
    g_sel = jnp.argmax(p_grp, axis=-1)
    p_sel = jnp.take_along_axis(p_grp, g_sel[:, None], axis=-1)
    logits = route[:, N_GROUPS:N_GROUPS + N_EXPERTS].reshape(n, N_GROUPS, EXPERTS_PER_GROUP)
    logits_g = jnp.take_along_axis(logits, g_sel[:, None, None], axis=1)[:, 0]
    top_val, top_idx = lax.top_k(logits_g, TOP_K)
    gate = p_sel * jax.nn.softmax(top_val, axis=-1)
    expert = (g_sel[:, None] * EXPERTS_PER_GROUP + top_idx).astype(jnp.int32)
    flat_e = expert.reshape(-1)
    onehot = (flat_e[:, None] == jnp.arange(N_EXPERTS, dtype=jnp.int32)[None, :]).astype(jnp.int32)
    csum = jnp.cumsum(onehot, axis=0)
    rank = jnp.take_along_axis(csum, flat_e[:, None], axis=1)[:, 0] - 1
    counts = csum[-1]
    padded = (counts + MOE_ROWS - 1) // MOE_ROWS * MOE_ROWS
    pad_end = jnp.cumsum(padded)
    dest = (pad_end - padded)[flat_e] + rank
    nblk = (n * TOP_K + N_EXPERTS * (MOE_ROWS - 1) + MOE_ROWS - 1) // MOE_ROWS
    nblk = -(-nblk // MOE_SEGMENTS) * MOE_SEGMENTS
    slots = nblk * MOE_ROWS
    flat_tok = jnp.repeat(jnp.arange(n, dtype=jnp.int32), TOP_K)
    slot_tok = jnp.zeros((slots,), jnp.int32).at[dest].set(flat_tok)
    blk_start = jnp.arange(nblk, dtype=jnp.int32) * MOE_ROWS
    blk_expert = jnp.minimum(jnp.sum((pad_end[None, :] <= blk_start[:, None]).astype(jnp.int32), axis=1),
                             N_EXPERTS - 1).astype(jnp.int32)
    n_used = (pad_end[-1] // MOE_ROWS).astype(jnp.int32).reshape(1)
    seg_blk = nblk // MOE_SEGMENTS
    y = jnp.zeros((slots, D), BF16)
    for s in range(MOE_SEGMENTS):
        x_rows = hb[slot_tok[s * seg_blk * MOE_ROWS:(s + 1) * seg_blk * MOE_ROWS]]
        y = moe_ffn(x_rows, blk_expert, n_used, w1, w3, w2, layer, s * seg_blk, y)
    d2 = dest.reshape(n, TOP_K)
    return y[d2[:, 0]], y[d2[:, 1]], gate


def _moe_combine_kernel(ya_ref, yb_ref, gate_ref, res_ref, gt_ref, o_ref):
    gate = gate_ref[...]
    f = ya_ref[...].astype(F32) * gate[:, 0:1] + yb_ref[...].astype(F32) * gate[:, 1:2]
    o_ref[...] = res_ref[...] + gt_ref[0] * f


def moe_combine(ya, yb, gate, res, gt, row0, tm=512):
    M, D = res.shape
    tm = min(tm, M)
    nb = gt.shape[0]
    rpb = M // nb
    assert M % tm == 0 and row0 % tm == 0 and rpb % tm == 0
    r0 = row0 // tm
    row_spec = lambda w: pl.BlockSpec((tm, w), lambda i: (r0 + i, 0))
    return pl.pallas_call(
        _moe_combine_kernel,
        out_shape=jax.ShapeDtypeStruct((M, D), F32),
        grid=(M // tm,),
        in_specs=[row_spec(D), row_spec(D), row_spec(TOP_K),
                  pl.BlockSpec((tm, D), lambda i: (i, 0)),
                  pl.BlockSpec((1, 1, D), lambda i: ((i * tm) // rpb, 0, 0))],
        out_specs=pl.BlockSpec((tm, D), lambda i: (i, 0)),
        compiler_params=_params("parallel"),
        name="moe_combine",
    )(ya, yb, gate.astype(F32), res, gt.reshape(nb, 1, D).astype(F32))


def moe_router(x, norm_g, sc, sh, w_grp, w_exp):
    w_route = jnp.concatenate([w_grp, w_exp], axis=1).astype(F32)
    w_route = jnp.pad(w_route, ((0, 0), (0, LANES - w_route.shape[1])))
    return linear(x, w_route, norm=(norm_g, sc, sh), precise=True, emit_h=True)


def attention_layer(x, xc, norm_g, sc, sh, csc, csh, w_in, w_out, a_q_gain, a_k_gain, a_sink, b_q_gain, b_k_gain,
                    b_rpb, gt, cgt):
    B, L, D = x.shape
    C = xc.shape[1]
    gains = (a_q_gain, a_k_gain, b_q_gain, b_k_gain)
    qkv = qkv_projection(x.reshape(B * L, D), w_in, norm_g, sc, sh, gains, L).reshape(B, L, -1)
    qkv_c = qkv_projection(xc.reshape(B * C, D), w_in, norm_g, csc, csh, gains, None).reshape(B, C, -1)
    ya = window_attention(qkv, qkv_c, a_sink)
    yb = neighbourhood_attention(qkv, qkv_c, b_rpb)
    x_new = linear(ya.reshape(B * L, -1), w_out, x2=yb.reshape(B * L, -1),
                   gated=(x.reshape(B * L, D), gt)).reshape(B, L, D)
    mix_c = context_attention(qkv_c, a_sink)
    xc_new = linear(mix_c.reshape(B * C, -1), w_out, gated=(xc.reshape(B * C, D), cgt)).reshape(B, C, D)
    return x_new, xc_new


def rwkv_layer(x, xc, norm_g, sc, sh, csc, csh, gt, mu, wr, wk, wv, wo, w0, w1, w2, a0, a1, a2, g1, g2, k_k, k_a,
               r_k, lnx_w, lnx_b):
    B, L, D = x.shape
    C = xc.shape[1]
    T = L + 2 * C

    xcat = jnp.concatenate([xc, x, xc], axis=1).reshape(B * T, D)
    nl, na = w1.shape[2], a1.shape[2]
    assert 2 * nl <= RW_TN and 2 * na <= RW_TN and g1.shape[1] <= RW_TN
    proj = rwkv_projections(xcat, norm_g, sc, sh, csc, csh, mu,
                            (wr, wk, wv, g1, jnp.concatenate([w1[0], w1[1]], axis=1),
                             jnp.concatenate([a1[0], a1[1]], axis=1)), T, C)
    rows = lambda w, lo: jnp.pad(w, ((lo, RW_TN - lo - w.shape[0]), (0, 0))).astype(BF16)
    w2p = [rows(w2[0], 0), rows(w2[1], nl)]
    a2p = [rows(a2[0], 0), rows(a2[1], na)]
    vecs = lambda d: jnp.stack([w0[d], a0[d], k_k, k_a, a0[1 - d], r_k.reshape(D), lnx_w, lnx_b]).astype(F32)
    proj = proj.reshape(B, T, -1)
    y_fwd = wkv_scan(proj, w2p[0], a2p[0], vecs(0), ctx_len=C, reverse=False)
    z = wkv_scan(proj, w2p[1], a2p[1], vecs(1), ctx_len=C, reverse=True, a2_other=a2p[0], y_fwd=y_fwd,
                 g2=rows(g2, 0))
    return linear(z.reshape(B * L, D), wo, gated=(x.reshape(B * L, D), gt)).reshape(B, L, D)


def kernel(x, c, ctx, c_ctx, ada_w, ada_b, norm_mix_g, norm_ffn_g, attn_w_in, attn_w_out, a_q_gain, a_k_gain, a_sink, b_q_gain, b_k_gain, b_rpb, rw_mu, rw_wr, rw_wk, rw_wv, rw_wo, rw_w0, rw_w1, rw_w2, rw_a0, rw_a1, rw_a2, rw_g1, rw_g2, rw_k_k, rw_k_a, rw_r_k, rw_lnx_w, rw_lnx_b, moe_w_grp, moe_w_exp, moe_w1, moe_w3, moe_w2):
    B, L, D = x.shape
    C = ctx.shape[1]
    depth = ada_w.shape[0]
    assert depth == 2, "layer 0 is the attention mixer, layer 1 (last) the RWKV-7 mixer"
    pad = (-(B + 1)) % 8
    cond = jnp.concatenate([jax.nn.silu(c), jax.nn.silu(c_ctx)[None, :], jnp.zeros((pad, D), F32)], axis=0)
    xc = ctx
    for i in range(depth):
        last = i == depth - 1
        j = i // 2
        mod = linear(cond, ada_w, precise=True, layer=i) + ada_b[i]
        sh1, sc1, gt1, sh2, sc2, gt2 = jnp.split(mod[:B], 6, axis=-1)
        csh1, csc1, cgt1, csh2, csc2, cgt2 = jnp.split(mod[B:B + 1], 6, axis=-1)
        if i % 2 == 0:
            x, xc = attention_layer(x, xc, norm_mix_g[i], sc1, sh1, csc1, csh1, attn_w_in[j], attn_w_out[j],
                                    a_q_gain[j], a_k_gain[j], a_sink[j], b_q_gain[j], b_k_gain[j], b_rpb[j],
                                    gt1, cgt1)
        else:
            x = rwkv_layer(x, xc, norm_mix_g[i], sc1, sh1, csc1, csh1, gt1, rw_mu[j], rw_wr[j], rw_wk[j], rw_wv[j],
                           rw_wo[j], rw_w0[j], rw_w1[j], rw_w2[j], rw_a0[j], rw_a1[j], rw_a2[j], rw_g1[j], rw_g2[j],
                           rw_k_k[j], rw_k_a[j], rw_r_k[j], rw_lnx_w[j], rw_lnx_b[j])
        route, hb = moe_router(x.reshape(B * L, D), norm_ffn_g[i], sc2, sh2, moe_w_grp[i], moe_w_exp[i])
        if last:
            ya, yb, gate = hierarchical_moe(route, hb, moe_w1, moe_w3, moe_w2, i)
            x = moe_combine(ya, yb, gate, x.reshape(B * L, D), gt2, 0).reshape(B, L, D)
        else:
            route_c, hb_c = moe_router(xc.reshape(B * C, D), norm_ffn_g[i], csc2, csh2, moe_w_grp[i], moe_w_exp[i])
            ya, yb, gate = hierarchical_moe(jnp.concatenate([route, route_c], axis=0),
                                            jnp.concatenate([hb, hb_c], axis=0), moe_w1, moe_w3, moe_w2, i)
            x = moe_combine(ya, yb, gate, x.reshape(B * L, D), gt2, 0).reshape(B, L, D)
            xc = moe_combine(ya, yb, gate, xc.reshape(B * C, D), cgt2, B * L).reshape(B, C, D)
    return x
```

```python
import functools
import math

import numpy as np
import jax
import jax.numpy as jnp
from jax import lax
from jax.experimental import pallas as pl
from jax.experimental.pallas import tpu as pltpu

GRID_W = 64
HEAD_DIM = 128
A_HEADS = 8
A_KV_HEADS = 2
A_BLOCK = 128
B_HEADS = 8
NA_ROWS = 8
NA_COLS = 16
ROPE_THETA = 10000.0
A_Q_COLS = A_HEADS * HEAD_DIM
A_KV_COLS = A_KV_HEADS * HEAD_DIM
B_COLS = B_HEADS * HEAD_DIM
QKV_COLS = A_Q_COLS + 2 * A_KV_COLS + 3 * B_COLS
MIX_WIDTH = A_Q_COLS + B_COLS
RW_HEAD = 64
RW_LNX_EPS = 64e-5
N_GROUPS = 4
EXPERTS_PER_GROUP = 8
N_EXPERTS = N_GROUPS * EXPERTS_PER_GROUP
TOP_K = 2
NORM_EPS = 1e-6
NEG_INF = -1e30

LANES = 128
VMEM_LIMIT_BYTES = 48 * 1024 * 1024
MOE_ROWS = 256
MOE_SEGMENTS = 4
NB_ROWS = 8
WKV_CHUNK = 64
WKV_PAIRS = 16
QKV_TN = 4 * HEAD_DIM
ROW_TILE = 1024

QA_BLK = 0
KA_BLK = A_Q_COLS // HEAD_DIM
VA_BLK = KA_BLK + A_KV_HEADS
QB_BLK = VA_BLK + A_KV_HEADS
KB_BLK = QB_BLK + B_HEADS
VB_BLK = KB_BLK + B_HEADS

BF16 = jnp.bfloat16
F32 = jnp.float32


def _params(*sem):
    return pltpu.CompilerParams(dimension_semantics=sem, vmem_limit_bytes=VMEM_LIMIT_BYTES)


def _nt(a, b):
    return lax.dot_general(a, b, (((1,), (1,)), ((), ())), preferred_element_type=F32)


def _mm(a, b):
    return jnp.dot(a, b, preferred_element_type=F32)


def _row_tile(want, group_sizes):
    t = want
    while t > 8 and any(g % t for g in group_sizes):
        t //= 2
    assert all(g % t == 0 for g in group_sizes)
    return t


def _rms_modulate(x, g, sc, sh):
    y = x * lax.rsqrt(jnp.mean(x * x, axis=-1, keepdims=True) + NORM_EPS)
    return (y * g) * (1.0 + sc) + sh


def _linear_kernel(*refs, mode, gated, precise, emit_h, layered):
    it = iter(refs)
    x_ref = next(it)
    if mode == "norm":
        g_ref, sc_ref, sh_ref = next(it), next(it), next(it)
    elif mode == "cat2":
        x2_ref = next(it)
    w_ref = next(it)
    if precise == "split":
        wlo_ref = next(it)
    if gated:
        res_ref, gate_ref = next(it), next(it)
    o_ref = next(it)
    if emit_h:
        hb_ref = next(it)
    h_ref = next(it)
    if precise == "split":
        hlo_ref = next(it)

    @pl.when(pl.program_id(1) == 0)
    def _():
        if mode == "cat2":
            k1 = x_ref.shape[1]
            h_ref[:, 0:k1] = x_ref[...].astype(h_ref.dtype)
            h_ref[:, k1:] = x2_ref[...].astype(h_ref.dtype)
            return
        x = x_ref[...].astype(F32)
        if mode == "norm":
            x = _rms_modulate(x, g_ref[...], sc_ref[0], sh_ref[0])
        h_ref[...] = x.astype(h_ref.dtype)
        if precise == "split":
            hlo_ref[...] = (x - h_ref[...].astype(F32)).astype(hlo_ref.dtype)
        if emit_h:
            hb_ref[...] = x.astype(hb_ref.dtype)

    w = w_ref[0] if layered else w_ref[...]
    if precise == "highest":
        acc = jnp.dot(h_ref[...], w, preferred_element_type=F32, precision=lax.Precision.HIGHEST)
    elif precise == "split":
        acc = _mm(h_ref[...], w) + (_mm(hlo_ref[...], w) + _mm(h_ref[...], wlo_ref[...]))
    else:
        acc = _mm(h_ref[...], w)
    if gated:
        acc = res_ref[...] + gate_ref[0] * acc
    o_ref[...] = acc.astype(o_ref.dtype)


def linear(x, w, *, norm=None, x2=None, gated=None, out_dtype=F32, precise=False, emit_h=False, layer=None,
           tm=ROW_TILE, tn=512):
    M, K = x.shape
    N = w.shape[-1]
    groups = [M] + ([M // norm[1].shape[0]] if norm is not None else []) + (
        [M // gated[1].shape[0]] if gated is not None else [])
    tm = _row_tile(tm, groups)
    tn = tn if N % tn == 0 else N
    assert N % tn == 0
    mode = "norm" if norm is not None else "cat2" if x2 is not None else "plain"
    wdt = F32 if precise == "highest" else BF16
    args = [x]
    in_specs = [pl.BlockSpec((tm, K), lambda i, j: (i, 0))]
    vec = lambda t: t.reshape(1, -1).astype(F32)
    vec_spec = lambda k: pl.BlockSpec((1, k), lambda i, j: (0, 0))
    if mode == "norm":
        g, sc, sh = norm
        nb = sc.shape[0]
        rpb = M // nb
        assert rpb % tm == 0
        args += [vec(g), sc.reshape(nb, 1, K).astype(F32), sh.reshape(nb, 1, K).astype(F32)]
        in_specs += [vec_spec(K),
                     pl.BlockSpec((1, 1, K), lambda i, j: ((i * tm) // rpb, 0, 0)),
                     pl.BlockSpec((1, 1, K), lambda i, j: ((i * tm) // rpb, 0, 0))]
    elif mode == "cat2":
        args.append(x2)
        in_specs.append(pl.BlockSpec((tm, x2.shape[1]), lambda i, j: (i, 0)))
        K = K + x2.shape[1]
    assert w.shape[-2] == K
    args.append(w.astype(wdt))
    if layer is None:
        in_specs.append(pl.BlockSpec((K, tn), lambda i, j: (0, j)))
    else:
        in_specs.append(pl.BlockSpec((1, K, tn), lambda i, j: (layer, 0, j)))
    scratch = [pltpu.VMEM((tm, K), wdt)]
    if precise == "split":
        assert layer is None
        args.append((w.astype(F32) - args[-1].astype(F32)).astype(BF16))
        in_specs.append(in_specs[-1])
        scratch.append(pltpu.VMEM((tm, K), BF16))
    if gated is not None:
        res, gate = gated
        nbg = gate.shape[0]
        rpg = M // nbg
        assert rpg % tm == 0
        args += [res, gate.reshape(nbg, 1, N).astype(F32)]
        in_specs += [pl.BlockSpec((tm, tn), lambda i, j: (i, j)),
                     pl.BlockSpec((1, 1, tn), lambda i, j: ((i * tm) // rpg, 0, j))]
    out_shape = jax.ShapeDtypeStruct((M, N), out_dtype)
    out_specs = pl.BlockSpec((tm, tn), lambda i, j: (i, j))
    if emit_h:
        out_shape = (out_shape, jax.ShapeDtypeStruct((M, K), BF16))
        out_specs = (out_specs, pl.BlockSpec((tm, K), lambda i, j: (i, 0)))
    return pl.pallas_call(
        functools.partial(_linear_kernel, mode=mode, gated=gated is not None, precise=precise, emit_h=emit_h,
                          layered=layer is not None),
        out_shape=out_shape,
        grid=(M // tm, N // tn),
        in_specs=in_specs,
        out_specs=out_specs,
        scratch_shapes=scratch,
        compiler_params=_params("parallel", "arbitrary"),
        name="linear_" + mode,
    )(*args)


RW_TN = 512
RW_ROW_TILES = (768, 512, 256, 128, 64, 32, 16, 8)


def _rwkv_proj_kernel(x_ref, prev_ref, next_ref, g_ref, sc_ref, sh_ref, csc_ref, csh_ref, mu_ref, w_ref,
                      o_ref, hs_ref, xs_ref, h_ref, *, t_len, c_len, wide_tiles):
    j = pl.program_id(1)
    tm = x_ref.shape[0]

    @pl.when(j == 0)
    def _():
        t0 = (pl.program_id(0) * tm) % t_len

        def hmod(rows, t):
            is_ctx = (t < c_len) | (t >= t_len - c_len)
            return _rms_modulate(rows, g_ref[...], jnp.where(is_ctx, csc_ref[...], sc_ref[0]),
                                 jnp.where(is_ctx, csh_ref[...], sh_ref[0]))

        row = lax.broadcasted_iota(jnp.int32, (tm, 1), 0)
        row8 = lax.broadcasted_iota(jnp.int32, (8, 1), 0)
        t = t0 + row
        h = hmod(x_ref[...].astype(F32), t)
        h_before = hmod(prev_ref[...].astype(F32), t0 - 8 + row8)[7:8]
        h_after = hmod(next_ref[...].astype(F32), t0 + tm + row8)[0:1]
        up = jnp.where(row == 0, h_before, pltpu.roll(h, 1, 0))
        dn = jnp.where(row == tm - 1, h_after, pltpu.roll(h, tm - 1, 0))
        has_prev = (t != 0) & (t != c_len) & (t != t_len - c_len)
        has_next = (t != c_len - 1) & (t != t_len - c_len - 1) & (t != t_len - 1)
        hs_ref[...] = h
        xs_ref[...] = 0.5 * (jnp.where(has_prev, up, 0.0) + jnp.where(has_next, dn, 0.0)) - h

    group = jnp.where(j < 3 * wide_tiles, j // wide_tiles, j - 3 * wide_tiles + 3)

    @pl.when((j % wide_tiles == 0) | (j >= 3 * wide_tiles))
    def _():
        h_ref[...] = (hs_ref[...] + xs_ref[...] * mu_ref[pl.ds(group, 1), :]).astype(h_ref.dtype)

    o_ref[...] = _mm(h_ref[...], w_ref[...])


def rwkv_projections(xcat, norm_g, sc, sh, csc, csh, mu, weights, t_len, c_len):
    M, K = xcat.shape
    nb = sc.shape[0]
    tm = next(t for t in RW_ROW_TILES if t_len % t == 0)
    assert M == nb * t_len and K % RW_TN == 0
    wr, wk, wv, g1, w1c, a1c = weights
    padw = lambda w: jnp.pad(w, ((0, 0), (0, RW_TN - w.shape[1])))
    w_all = jnp.concatenate([wr, wk, wv, padw(g1), padw(w1c), padw(a1c)], axis=1).astype(BF16)
    mu_g = jnp.pad(mu[jnp.array([0, 2, 3, 5, 1, 4])].astype(F32), ((0, 2), (0, 0)))
    n_out = w_all.shape[1]
    per = tm // 8
    vec = lambda t: t.reshape(1, -1).astype(F32)
    vec_spec = pl.BlockSpec((1, K), lambda i, j: (0, 0))
    mod_spec = pl.BlockSpec((1, 1, K), lambda i, j: ((i * tm) // t_len, 0, 0))
    return pl.pallas_call(
        functools.partial(_rwkv_proj_kernel, t_len=t_len, c_len=c_len, wide_tiles=K // RW_TN),
        out_shape=jax.ShapeDtypeStruct((M, n_out), F32),
        grid=(M // tm, n_out // RW_TN),
        in_specs=[pl.BlockSpec((tm, K), lambda i, j: (i, 0)),
                  pl.BlockSpec((8, K), lambda i, j: (jnp.maximum(i * per - 1, 0), 0)),
                  pl.BlockSpec((8, K), lambda i, j: (jnp.minimum((i + 1) * per, M // 8 - 1), 0)),
                  vec_spec, mod_spec, mod_spec, vec_spec, vec_spec,
                  pl.BlockSpec((8, K), lambda i, j: (0, 0)),
                  pl.BlockSpec((K, RW_TN), lambda i, j: (0, j))],
        out_specs=pl.BlockSpec((tm, RW_TN), lambda i, j: (i, j)),
        scratch_shapes=[pltpu.VMEM((tm, K), F32), pltpu.VMEM((tm, K), F32), pltpu.VMEM((tm, K), BF16)],
        compiler_params=_params("parallel", "arbitrary"),
        name="rwkv_projections",
    )(xcat, xcat, xcat, vec(norm_g), sc.reshape(nb, 1, K).astype(F32), sh.reshape(nb, 1, K).astype(F32),
      vec(csc), vec(csh), mu_g, w_all)


def _qkv_kernel(x_ref, g_ref, sc_ref, sh_ref, w_ref, gain_ref, cos_ref, sin_ref, o_ref, h_ref, *, rope):
    j = pl.program_id(1)

    @pl.when(j == 0)
    def _():
        h_ref[...] = _rms_modulate(x_ref[...].astype(F32), g_ref[...], sc_ref[0], sh_ref[0]).astype(h_ref.dtype)

    acc = _mm(h_ref[...], w_ref[...])
    tm = acc.shape[0]

    def emit(kinds):
        for hh, kind in enumerate(kinds):
            sl = slice(hh * HEAD_DIM, (hh + 1) * HEAD_DIM)
            y = acc[:, sl]
            if kind != "plain":
                y = y * lax.rsqrt(jnp.mean(y * y, axis=-1, keepdims=True) + NORM_EPS) * gain_ref[:, sl]
                if kind == "rope" and rope:
                    lane = lax.broadcasted_iota(jnp.int32, (tm, HEAD_DIM), 1)
                    first = (lane % (HEAD_DIM // 2)) < HEAD_DIM // 4
                    partner = jnp.where(first, pltpu.roll(y, HEAD_DIM - HEAD_DIM // 4, 1),
                                        pltpu.roll(y, HEAD_DIM // 4, 1))
                    y = y * cos_ref[...] + partner * sin_ref[...]
            o_ref[:, sl] = y.astype(o_ref.dtype)

    per = QKV_TN // HEAD_DIM
    kinds = (["rope"] * A_HEADS + ["rope"] * A_KV_HEADS + ["plain"] * A_KV_HEADS + ["norm"] * B_HEADS
             + ["norm"] * B_HEADS + ["plain"] * B_HEADS)
    tiles = [tuple(kinds[t * per:(t + 1) * per]) for t in range(len(kinds) // per)]
    for kind_set in sorted(set(tiles)):
        cond = None
        for t, ks in enumerate(tiles):
            if ks == kind_set:
                cond = (j == t) if cond is None else (cond | (j == t))
        pl.when(cond)(functools.partial(emit, kind_set))


def qkv_projection(x, w_in, norm_g, sc, sh, gains, rope_len):
    M, K = x.shape
    N = w_in.shape[1]
    assert N == QKV_COLS and N % QKV_TN == 0
    nb = sc.shape[0]
    rpb = M // nb
    tm = _row_tile(ROW_TILE, [rpb] + ([rope_len] if rope_len is not None else []))
    a_q, a_k, b_q, b_k = (t.astype(F32) for t in gains)
    ones = jnp.ones((HEAD_DIM,), F32)
    gain = jnp.concatenate([jnp.tile(a_q, A_HEADS), jnp.tile(a_k, A_KV_HEADS), jnp.tile(ones, A_KV_HEADS),
                            jnp.tile(b_q, B_HEADS), jnp.tile(b_k, B_HEADS), jnp.tile(ones, B_HEADS)]).reshape(1, N)
    rope = rope_len is not None
    if rope:
        assert rope_len % tm == 0
        t = jnp.arange(rope_len)
        quarter = HEAD_DIM // 4
        inv_freq = ROPE_THETA ** (-jnp.arange(quarter, dtype=F32) / quarter)
        ar = (t // GRID_W).astype(F32)[:, None] * inv_freq[None, :]
        ac = (t % GRID_W).astype(F32)[:, None] * inv_freq[None, :]
        cos_t = jnp.concatenate([jnp.cos(ar), jnp.cos(ar), jnp.cos(ac), jnp.cos(ac)], axis=1)
        sin_t = jnp.concatenate([-jnp.sin(ar), jnp.sin(ar), -jnp.sin(ac), jnp.sin(ac)], axis=1)
        nt = rope_len // tm
    else:
        cos_t = jnp.ones((tm, HEAD_DIM), F32)
        sin_t = jnp.zeros((tm, HEAD_DIM), F32)
        nt = 1
    tab_spec = pl.BlockSpec((tm, HEAD_DIM), lambda i, j: (i % nt, 0))
    return pl.pallas_call(
        functools.partial(_qkv_kernel, rope=rope),
        out_shape=jax.ShapeDtypeStruct((M, N), BF16),
        grid=(M // tm, N // QKV_TN),
        in_specs=[pl.BlockSpec((tm, K), lambda i, j: (i, 0)),
                  pl.BlockSpec((1, K), lambda i, j: (0, 0)),
                  pl.BlockSpec((1, 1, K), lambda i, j: ((i * tm) // rpb, 0, 0)),
                  pl.BlockSpec((1, 1, K), lambda i, j: ((i * tm) // rpb, 0, 0)),
                  pl.BlockSpec((K, QKV_TN), lambda i, j: (0, j)),
                  pl.BlockSpec((1, QKV_TN), lambda i, j: (0, j)),
                  tab_spec, tab_spec],
        out_specs=pl.BlockSpec((tm, QKV_TN), lambda i, j: (i, j)),
        scratch_shapes=[pltpu.VMEM((tm, K), BF16)],
        compiler_params=_params("parallel", "arbitrary"),
        name="qkv_projection",
    )(x, norm_g.reshape(1, K).astype(F32), sc.reshape(nb, 1, K).astype(F32), sh.reshape(nb, 1, K).astype(F32),
      w_in.astype(BF16), gain, cos_t, sin_t)


def _win_attn_kernel(q_ref, kp_ref, kc_ref, kn_ref, vp_ref, vc_ref, vn_ref, kx_ref, vx_ref, sink_ref,
                     o_ref, *, nblk, groups, scale):
    n = pl.program_id(2)
    blk = A_BLOCK
    q = q_ref[0]
    qs = jnp.concatenate([q[:, g * HEAD_DIM:(g + 1) * HEAD_DIM] for g in range(groups)], axis=0)
    rows = groups * blk
    keys = jnp.concatenate([kp_ref[0], kc_ref[0], kn_ref[0], kx_ref[0]], axis=0)
    vals = jnp.concatenate([vp_ref[0], vc_ref[0], vn_ref[0], vx_ref[0]], axis=0)
    nkeys = keys.shape[0]
    s = _nt(qs, keys) * scale
    qi = lax.broadcasted_iota(jnp.int32, (rows, nkeys), 0) % blk
    kj = lax.broadcasted_iota(jnp.int32, (rows, nkeys), 1)
    ok = (((kj >= blk) | ((kj >= qi) & (n > 0)))
          & ((kj < 2 * blk) | (kj >= 3 * blk) | ((kj - 2 * blk <= qi) & (n < nblk - 1))))
    s = jnp.where(ok, s, NEG_INF)
    sink = sink_ref[0][:, :1]
    m = jnp.maximum(jnp.max(s, axis=-1, keepdims=True), sink)
    p = jnp.exp(s - m)
    den = jnp.sum(p, axis=-1, keepdims=True) + jnp.exp(sink - m)
    o = _mm(p.astype(BF16), vals) / den
    for g in range(groups):
        o_ref[0, :, g * HEAD_DIM:(g + 1) * HEAD_DIM] = o[g * blk:(g + 1) * blk].astype(o_ref.dtype)


def window_attention(qkv, qkv_c, sink):
    B, L, _ = qkv.shape
    C = qkv_c.shape[1]
    groups = A_HEADS // A_KV_HEADS
    nblk = L // A_BLOCK
    gw = groups * HEAD_DIM
    sink_t = jnp.broadcast_to(
        jnp.repeat(sink.astype(F32).reshape(A_KV_HEADS, groups), A_BLOCK, axis=1)[:, :, None],
        (A_KV_HEADS, groups * A_BLOCK, LANES))
    blkspec = lambda f: pl.BlockSpec((1, A_BLOCK, HEAD_DIM), f)
    prev = lambda c0: (lambda b, h, n: (b, jnp.maximum(n - 1, 0), c0 + h))
    cur = lambda c0: (lambda b, h, n: (b, n, c0 + h))
    nxt = lambda c0: (lambda b, h, n: (b, jnp.minimum(n + 1, nblk - 1), c0 + h))
    ctx = lambda c0: pl.BlockSpec((1, C, HEAD_DIM), lambda b, h, n: (b, 0, c0 + h))
    return pl.pallas_call(
        functools.partial(_win_attn_kernel, nblk=nblk, groups=groups, scale=HEAD_DIM ** -0.5),
        out_shape=jax.ShapeDtypeStruct((B, L, A_Q_COLS), BF16),
        grid=(B, A_KV_HEADS, nblk),
        in_specs=[pl.BlockSpec((1, A_BLOCK, gw), lambda b, h, n: (b, n, h)),
                  blkspec(prev(KA_BLK)), blkspec(cur(KA_BLK)), blkspec(nxt(KA_BLK)),
                  blkspec(prev(VA_BLK)), blkspec(cur(VA_BLK)), blkspec(nxt(VA_BLK)),
                  ctx(KA_BLK), ctx(VA_BLK),
                  pl.BlockSpec((1, groups * A_BLOCK, LANES), lambda b, h, n: (h, 0, 0))],
        out_specs=pl.BlockSpec((1, A_BLOCK, gw), lambda b, h, n: (b, n, h)),
        compiler_params=_params("parallel", "parallel", "arbitrary"),
        name="window_attention",
    )(qkv, qkv, qkv, qkv, qkv, qkv, qkv, qkv_c, qkv_c, sink_t)


def _nbr_attn_kernel(q_ref, kp_ref, kc_ref, kn_ref, vp_ref, vc_ref, vn_ref, kx_ref, vx_ref, bias_ref,
                     o_ref, kbuf, vbuf, *, grid_rows, scale):
    rb = pl.program_id(2)
    tok = NB_ROWS * GRID_W
    nk = NA_ROWS * GRID_W
    nctx = kx_ref.shape[1]
    kbuf[0:tok] = kp_ref[0]
    kbuf[tok:2 * tok] = kc_ref[0]
    kbuf[2 * tok:3 * tok] = kn_ref[0]
    vbuf[0:tok] = vp_ref[0]
    vbuf[tok:2 * tok] = vc_ref[0]
    vbuf[2 * tok:3 * tok] = vn_ref[0]
    kx = kx_ref[0]
    vx = vx_ref[0]
    sx_all = _nt(q_ref[0], kx) * scale
    R = range(NB_ROWS)
    rsl = [slice(rr * GRID_W, (rr + 1) * GRID_W) for rr in R]
    r = [rb * NB_ROWS + rr for rr in R]
    r0 = [jnp.clip(r[rr] - NA_ROWS // 2, 0, grid_rows - NA_ROWS) for rr in R]
    off = [pl.multiple_of((r0[rr] - (rb - 1) * NB_ROWS) * GRID_W, GRID_W) for rr in R]
    s = [jnp.concatenate([_nt(q_ref[0, rsl[rr], :], kbuf[pl.ds(off[rr], nk), :]) * scale
                          + bias_ref[0, r0[rr] - r[rr] + NA_ROWS - 1], sx_all[rsl[rr]]], axis=1)
         for rr in R]
    m = [jnp.max(s[rr], axis=-1, keepdims=True) for rr in R]
    p = [jnp.exp(s[rr] - m[rr]) for rr in R]
    den = [jnp.sum(p[rr], axis=-1, keepdims=True) for rr in R]
    pb = [p[rr].astype(BF16) for rr in R]
    o = [(_mm(pb[rr][:, 0:nk], vbuf[pl.ds(off[rr], nk), :]) + _mm(pb[rr][:, nk:nk + nctx], vx)) / den[rr]
         for rr in R]
    for rr in R:
        o_ref[0, rsl[rr], :] = o[rr].astype(o_ref.dtype)


def _nbr_bias_table(rpb):
    col = np.arange(GRID_W)
    c0 = np.clip(col - NA_COLS // 2, 0, GRID_W - NA_COLS)
    col_ok = (col[None, :] >= c0[:, None]) & (col[None, :] < c0[:, None] + NA_COLS)
    dc = np.clip(col[None, :] - col[:, None] + (NA_COLS - 1), 0, 2 * NA_COLS - 2)
    dr = np.arange(NA_ROWS)[:, None] + np.arange(NA_ROWS)[None, :]
    sel_r = np.eye(2 * NA_ROWS - 1, dtype=np.float32)[dr]
    sel_c = np.eye(2 * NA_COLS - 1, dtype=np.float32)[dc]
    t = jnp.einsum('djr,hrc->hdjc', sel_r, rpb.astype(F32), precision=lax.Precision.HIGHEST)
    t = jnp.einsum('hdjc,qkc->hdqjk', t, sel_c, precision=lax.Precision.HIGHEST)
    t = jnp.where(col_ok[None, None, :, None, :], t, NEG_INF)
    return t.reshape(rpb.shape[0], NA_ROWS, GRID_W, NA_ROWS * GRID_W)


def neighbourhood_attention(qkv, qkv_c, rpb):
    B, L, _ = qkv.shape
    C = qkv_c.shape[1]
    grid_rows = L // GRID_W
    tok = NB_ROWS * GRID_W
    assert grid_rows >= NA_ROWS and L % tok == 0 and NB_ROWS == NA_ROWS
    nrb = L // tok
    bias = _nbr_bias_table(rpb)
    spec = lambda f: pl.BlockSpec((1, tok, HEAD_DIM), f)
    prev = lambda c0: (lambda b, h, n: (b, jnp.maximum(n - 1, 0), c0 + h))
    cur = lambda c0: (lambda b, h, n: (b, n, c0 + h))
    nxt = lambda c0: (lambda b, h, n: (b, jnp.minimum(n + 1, nrb - 1), c0 + h))
    ctx = lambda c0: pl.BlockSpec((1, C, HEAD_DIM), lambda b, h, n: (b, 0, c0 + h))
    return pl.pallas_call(
        functools.partial(_nbr_attn_kernel, grid_rows=grid_rows, scale=HEAD_DIM ** -0.5),
        out_shape=jax.ShapeDtypeStruct((B, L, B_COLS), BF16),
        grid=(B, B_HEADS, nrb),
        in_specs=[spec(cur(QB_BLK)), spec(prev(KB_BLK)), spec(cur(KB_BLK)), spec(nxt(KB_BLK)),
                  spec(prev(VB_BLK)), spec(cur(VB_BLK)), spec(nxt(VB_BLK)),
                  ctx(KB_BLK), ctx(VB_BLK),
                  pl.BlockSpec((1, NA_ROWS, GRID_W, NA_ROWS * GRID_W), lambda b, h, n: (h, 0, 0, 0))],
        out_specs=spec(cur(0)),
        scratch_shapes=[pltpu.VMEM((3 * tok, HEAD_DIM), BF16), pltpu.VMEM((3 * tok, HEAD_DIM), BF16)],
        compiler_params=_params("parallel", "parallel", "arbitrary"),
        name="neighbourhood_attention",
    )(qkv, qkv, qkv, qkv, qkv, qkv, qkv, qkv_c, qkv_c, bias)


def _ctx_attn_kernel(q_ref, k_ref, v_ref, sink_ref, o_ref, *, scale):
    s = _nt(q_ref[0], k_ref[0]) * scale
    sink = sink_ref[0][:1, :1]
    m = jnp.maximum(jnp.max(s, axis=-1, keepdims=True), sink)
    p = jnp.exp(s - m)
    den = jnp.sum(p, axis=-1, keepdims=True) + jnp.exp(sink - m)
    o_ref[0] = (_mm(p.astype(BF16), v_ref[0]) / den).astype(o_ref.dtype)


def context_attention(qkv_c, a_sink):
    B, C, _ = qkv_c.shape
    grp = A_HEADS // A_KV_HEADS
    heads = A_HEADS + B_HEADS
    sink = jnp.concatenate([a_sink.astype(F32), jnp.full((B_HEADS,), NEG_INF, F32)])
    sink_t = jnp.broadcast_to(sink[:, None, None], (heads, 8, LANES))
    is_a = lambda h: h < A_HEADS
    qcol = lambda h: jnp.where(is_a(h), QA_BLK + h, QB_BLK + h - A_HEADS)
    kcol = lambda h: jnp.where(is_a(h), KA_BLK + h // grp, KB_BLK + h - A_HEADS)
    vcol = lambda h: jnp.where(is_a(h), VA_BLK + h // grp, VB_BLK + h - A_HEADS)
    return pl.pallas_call(
        functools.partial(_ctx_attn_kernel, scale=HEAD_DIM ** -0.5),
        out_shape=jax.ShapeDtypeStruct((B, C, MIX_WIDTH), BF16),
        grid=(B, heads),
        in_specs=[pl.BlockSpec((1, C, HEAD_DIM), lambda b, h: (b, 0, qcol(h))),
                  pl.BlockSpec((1, C, HEAD_DIM), lambda b, h: (b, 0, kcol(h))),
                  pl.BlockSpec((1, C, HEAD_DIM), lambda b, h: (b, 0, vcol(h))),
                  pl.BlockSpec((1, 8, LANES), lambda b, h: (h, 0, 0))],
        out_specs=pl.BlockSpec((1, C, HEAD_DIM), lambda b, h: (b, 0, h)),
        compiler_params=_params("parallel", "parallel"),
        name="context_attention",
    )(qkv_c, qkv_c, qkv_c, sink_t)


def _wkv_pairs(ins, cum, sts, strict_s, incl_s, lane_lo, bd):
    C = ins[0][0].shape[0]
    P = range(len(ins))
    steps = int(math.log2(C))
    cat = jnp.concatenate

    def bdw(x):
        return jnp.where(bd, cat([x, x], axis=0), 0.0).astype(BF16)

    tot = [jnp.sum(ins[p][1], axis=0, keepdims=True) for p in P]
    g_inv = [jnp.exp(-cum[p]) for p in P]
    lhs = [cat([ins[p][4] * jnp.exp(cum[p] - ins[p][1]), ins[p][0] * jnp.exp(cum[p])], axis=0).astype(BF16)
           for p in P]
    b_t = [ins[p][5] * g_inv[p] for p in P]
    k_t = [ins[p][2] * g_inv[p] for p in P]
    rhs = [cat([jnp.where(lane_lo, b_t[p], 0.0), jnp.where(lane_lo, 0.0, b_t[p]),
                jnp.where(lane_lo, k_t[p], 0.0), jnp.where(lane_lo, 0.0, k_t[p])], axis=0).astype(BF16) for p in P]
    prod = [_nt(lhs[p], rhs[p]) for p in P]
    sh = [_nt(lhs[p], sts[p].astype(BF16)) for p in P]
    ns = [jnp.where(strict_s, prod[p][0:C, 0:2 * C], 0.0) for p in P]
    n_k = [cat([jnp.where(strict_s, prod[p][0:C, 2 * C:4 * C], 0.0),
                jnp.where(incl_s, prod[p][C:2 * C, 2 * C:4 * C], 0.0)], axis=0).astype(BF16) for p in P]
    n_rb = [jnp.where(incl_s, prod[p][C:2 * C, 0:2 * C], 0.0).astype(BF16) for p in P]
    kv = [_mm(n_k[p], bdw(ins[p][3])) for p in P]
    x = [sh[p][0:C] + kv[p][0:C] for p in P]
    for i in range(steps):
        nsb = [ns[p].astype(BF16) for p in P]
        if i + 1 < steps:
            z = [_mm(nsb[p], cat([bdw(x[p]), bdw(ns[p])], axis=1)) for p in P]
            x = [x[p] + z[p][:, 0:LANES] for p in P]
            ns = [z[p][:, LANES:2 * LANES] for p in P]
        else:
            x = [x[p] + _mm(nsb[p], bdw(x[p])) for p in P]
    y = [sh[p][C:2 * C] + kv[p][C:2 * C] + _mm(n_rb[p], bdw(x[p])) for p in P]
    uv_t = [cat([x[p], ins[p][3]], axis=0).T.astype(BF16) for p in P]
    g_tail = [jnp.exp(tot[p] - cum[p]) for p in P]
    bk = [cat([ins[p][5] * g_tail[p], ins[p][2] * g_tail[p]], axis=0).astype(BF16) for p in P]
    st_new = [sts[p] * jnp.exp(tot[p]) + jnp.where(bd, _mm(uv_t[p], bk[p]), 0.0) for p in P]
    return y, st_new


def _cumsum_time(x, reverse):
    C = x.shape[0]
    ti = lax.broadcasted_iota(jnp.int32, (C, C), 0)
    si = lax.broadcasted_iota(jnp.int32, (C, C), 1)
    tri = jnp.where((si >= ti) if reverse else (si <= ti), 1.0, 0.0).astype(BF16)
    hi = x.astype(BF16)
    rem = x - hi.astype(F32)
    mid = rem.astype(BF16)
    lo = (rem - mid.astype(F32)).astype(BF16)
    return _mm(tri, hi) + _mm(tri, mid) + _mm(tri, lo)


def _head_sum(x, lane_lo):
    lo = jnp.sum(jnp.where(lane_lo, x, 0.0), axis=-1, keepdims=True)
    hi = jnp.sum(jnp.where(lane_lo, 0.0, x), axis=-1, keepdims=True)
    return jnp.where(lane_lo, lo, hi)


_V_W0, _V_A0, _V_KK, _V_KA, _V_A0_OTHER, _V_RK, _V_LNW, _V_LNB = range(8)


def _wkv_kernel(*refs, reverse, pairs):
    C = WKV_CHUNK
    if reverse:
        (r_ref, k_ref, v_ref, wl_ref, al_ref, w2_ref, a2_ref, vec_ref, a2o_ref, yf_ref, gh_ref, g2_ref,
         o_ref, st_ref) = refs
    else:
        r_ref, k_ref, v_ref, wl_ref, al_ref, w2_ref, a2_ref, vec_ref, o_ref, st_ref = refs

    @pl.when(pl.program_id(2) == 0)
    def _():
        st_ref[...] = jnp.zeros_like(st_ref)

    tt = lax.broadcasted_iota(jnp.int32, (C, 2 * C), 0)
    ss = lax.broadcasted_iota(jnp.int32, (C, 2 * C), 1) % C
    incl_s = (ss >= tt) if reverse else (ss <= tt)
    strict_s = (ss > tt) if reverse else (ss < tt)
    lane_lo = lax.broadcasted_iota(jnp.int32, (C, LANES), 1) < RW_HEAD
    bd = (lax.broadcasted_iota(jnp.int32, (LANES, LANES), 0) // RW_HEAD) == (
        lax.broadcasted_iota(jnp.int32, (LANES, LANES), 1) // RW_HEAD)
    sls = [slice(p * LANES, (p + 1) * LANES) for p in range(pairs)]
    vec = lambda row, sl: vec_ref[row:row + 1, sl]

    wl = jnp.tanh(wl_ref[0]).astype(BF16)
    al = al_ref[0].astype(BF16)
    w_raw = vec_ref[_V_W0:_V_W0 + 1, :] + _mm(wl, w2_ref[...])
    lw_all = -math.exp(-0.5) * jax.nn.sigmoid(w_raw)
    cum_all = _cumsum_time(lw_all, reverse)
    ag_all = jax.nn.sigmoid(vec_ref[_V_A0:_V_A0 + 1, :] + _mm(al, a2_ref[...]))
    if reverse:
        ag_o_all = jax.nn.sigmoid(vec_ref[_V_A0_OTHER:_V_A0_OTHER + 1, :] + _mm(al, a2o_ref[...]))
        gate_all = _mm(jax.nn.sigmoid(gh_ref[0]).astype(BF16), g2_ref[...])
    ins = []
    for sl in sls:
        lw, ag = lw_all[:, sl], ag_all[:, sl]
        k_raw = k_ref[0, :, sl]
        kkf = k_raw * vec(_V_KK, sl)
        kk = kkf * lax.rsqrt(_head_sum(kkf * kkf, lane_lo) + 1e-12)
        kd = k_raw * (1.0 + (ag - 1.0) * vec(_V_KA, sl))
        ins.append((r_ref[0, :, sl], lw, kd, v_ref[0, :, sl], -kk, kk * ag))
    y, st_new = _wkv_pairs(ins, [cum_all[:, sl] for sl in sls], [st_ref[p] for p in range(pairs)],
                           strict_s, incl_s, lane_lo, bd)
    for p, sl in enumerate(sls):
        st_ref[p] = st_new[p]
        if not reverse:
            o_ref[0, :, sl] = y[p]
            continue
        r, _, kd, v, _, _ = ins[p]
        k_sum = kd + k_ref[0, :, sl] * (1.0 + (ag_o_all[:, sl] - 1.0) * vec(_V_KA, sl))
        bonus = _head_sum(r * k_sum * vec(_V_RK, sl), lane_lo) * v
        ysum = yf_ref[0, :, sl] + y[p]
        mean = _head_sum(ysum, lane_lo) * (1.0 / RW_HEAD)
        dev = ysum - mean
        var = _head_sum(dev * dev, lane_lo) * (1.0 / RW_HEAD)
        yn = dev * lax.rsqrt(var + RW_LNX_EPS) * vec(_V_LNW, sl) + vec(_V_LNB, sl)
        o_ref[0, :, sl] = ((yn + bonus) * gate_all[:, sl]).astype(o_ref.dtype)


def wkv_scan(proj, w2, a2, vecs, *, ctx_len, reverse, a2_other=None, y_fwd=None, g2=None):
    B, T, _ = proj.shape
    D = vecs.shape[1]
    C = WKV_CHUNK
    L = T - 2 * ctx_len
    assert ctx_len % C == 0 and L % C == 0 and 2 * C == LANES
    nchunk = (ctx_len + L) // C
    cc = ctx_len // C
    nl = L // C
    pairs = min(WKV_PAIRS, D // LANES)
    width = pairs * LANES
    assert D % width == 0
    nd = D // width
    if reverse:
        tchunk = lambda ci: cc + nchunk - 1 - ci
    else:
        tchunk = lambda ci: ci
    lchunk = lambda ci: jnp.clip(tchunk(ci) - cc, 0, nl - 1)
    pspec = lambda m: pl.BlockSpec((1, C, width), lambda bi, di, ci: (bi, tchunk(ci), m * nd + di))
    hid0 = 3 * D // RW_TN
    hspec = lambda m: pl.BlockSpec((1, C, RW_TN), lambda bi, di, ci: (bi, tchunk(ci), hid0 + m))
    lspec = pl.BlockSpec((1, C, width), lambda bi, di, ci: (bi, lchunk(ci), di))
    wspec = lambda rows: pl.BlockSpec((rows, width), lambda bi, di, ci: (0, di))
    args = [proj, proj, proj, proj, proj, w2, a2, vecs]
    in_specs = [pspec(0), pspec(1), pspec(2), hspec(1), hspec(2), wspec(RW_TN), wspec(RW_TN),
                wspec(vecs.shape[0])]
    if reverse:
        args += [a2_other, y_fwd, proj, g2]
        in_specs += [wspec(RW_TN), lspec, hspec(0), wspec(RW_TN)]
    return pl.pallas_call(
        functools.partial(_wkv_kernel, reverse=reverse, pairs=pairs),
        out_shape=jax.ShapeDtypeStruct((B, L, D), BF16 if reverse else F32),
        grid=(B, nd, nchunk),
        in_specs=in_specs,
        out_specs=lspec,
        scratch_shapes=[pltpu.VMEM((pairs, LANES, LANES), F32)],
        compiler_params=_params("parallel", "parallel", "arbitrary"),
        name="wkv_scan_bwd" if reverse else "wkv_scan_fwd",
    )(*args)


def _moe_ffn_kernel(be_ref, nused_ref, x_ref, w1_ref, w3_ref, w2_ref, y_in_ref, o_ref, w1b, w3b, w2b, *, block0):
    del y_in_ref
    i = pl.program_id(0)
    blk = block0 + i
    used = blk < nused_ref[0]

    @pl.when(used & ((i == 0) | (be_ref[blk] != be_ref[jnp.maximum(blk - 1, 0)])))
    def _():
        w1b[...] = w1_ref[0, 0].astype(BF16)
        w3b[...] = w3_ref[0, 0].astype(BF16)
        w2b[...] = w2_ref[0, 0].astype(BF16)

    @pl.when(used)
    def _():
        half = MOE_ROWS // 2
        sls = [slice(0, half), slice(half, MOE_ROWS)]
        x = [x_ref[sl, :] for sl in sls]
        h1 = [_mm(xh, w1b[...]) for xh in x]
        h3 = [_mm(xh, w3b[...]) for xh in x]
        hh = [((a * jax.nn.sigmoid(a)) * b).astype(BF16) for a, b in zip(h1, h3)]
        y = [_mm(h, w2b[...]) for h in hh]
        for sl, yh in zip(sls, y):
            o_ref[sl, :] = yh.astype(o_ref.dtype)

    @pl.when(jnp.logical_not(used))
    def _():
        o_ref[...] = jnp.zeros_like(o_ref)


def moe_ffn(x_rows, blk_expert, n_used, w1, w3, w2, layer, block0, y):
    rows, D = x_rows.shape
    F = w1.shape[3]
    wspec = lambda shape: pl.BlockSpec((1, 1) + shape, lambda i, be, nu: (layer, be[block0 + i], 0, 0))
    grid_spec = pltpu.PrefetchScalarGridSpec(
        num_scalar_prefetch=2,
        grid=(rows // MOE_ROWS,),
        in_specs=[pl.BlockSpec((MOE_ROWS, D), lambda i, be, nu: (i, 0)), wspec((D, F)), wspec((D, F)),
                  wspec((F, D)), pl.BlockSpec(memory_space=pl.ANY)],
        out_specs=pl.BlockSpec((MOE_ROWS, D), lambda i, be, nu: (block0 + i, 0)),
        scratch_shapes=[pltpu.VMEM((D, F), BF16), pltpu.VMEM((D, F), BF16), pltpu.VMEM((F, D), BF16)],
    )
    return pl.pallas_call(
        functools.partial(_moe_ffn_kernel, block0=block0),
        out_shape=jax.ShapeDtypeStruct(y.shape, y.dtype),
        grid_spec=grid_spec,
        input_output_aliases={6: 0},
        compiler_params=_params("arbitrary"),
        name="moe_ffn",
    )(blk_expert, n_used, x_rows, w1.astype(F32), w3.astype(F32), w2.astype(F32), y)


def hierarchical_moe(route, hb, w1, w3, w2, layer):
    n, D = hb.shape
    p_grp = jax.nn.softmax(route[:, :N_GROUPS], axis=-1)
    g_sel = jnp.argmax(p_grp, axis=-1)
    p_sel = jnp.take_along_axis(p_grp, g_sel[:, None], axis=-1)
    logits = route[:, N_GROUPS:N_GROUPS + N_EXPERTS].reshape(n, N_GROUPS, EXPERTS_PER_GROUP)
    logits_g = jnp.take_along_axis(logits, g_sel[:, None, None], axis=1)[:, 0]
    top_val, top_idx = lax.top_k(logits_g, TOP_K)
    gate = p_sel * jax.nn.softmax(top_val, axis=-1)
    expert = (g_sel[:, None] * EXPERTS_PER_GROUP + top_idx).astype(jnp.int32)
    flat_e = expert.reshape(-1)
    onehot = (flat_e[:, None] == jnp.arange(N_EXPERTS, dtype=jnp.int32)[None, :]).astype(jnp.int32)
    csum = jnp.cumsum(onehot, axis=0)
    rank = jnp.take_along_axis(csum, flat_e[:, None], axis=1)[:, 0] - 1
    counts = csum[-1]
    padded = (counts + MOE_ROWS - 1) // MOE_ROWS * MOE_ROWS
    pad_end = jnp.cumsum(padded)
    dest = (pad_end - padded)[flat_e] + rank
    nblk = (n * TOP_K + N_EXPERTS * (MOE_ROWS - 1) + MOE_ROWS - 1) // MOE_ROWS
    nblk = -(-nblk // MOE_SEGMENTS) * MOE_SEGMENTS
    slots = nblk * MOE_ROWS
    flat_tok = jnp.repeat(jnp.arange(n, dtype=jnp.int32), TOP_K)
    slot_tok = jnp.zeros((slots,), jnp.int32).at[dest].set(flat_tok)
    blk_start = jnp.arange(nblk, dtype=jnp.int32) * MOE_ROWS
    blk_expert = jnp.minimum(jnp.sum((pad_end[None, :] <= blk_start[:, None]).astype(jnp.int32), axis=1),
                             N_EXPERTS - 1).astype(jnp.int32)
    n_used = (pad_end[-1] // MOE_ROWS).astype(jnp.int32).reshape(1)
    seg_blk = nblk // MOE_SEGMENTS
    y = jnp.zeros((slots, D), BF16)
    for s in range(MOE_SEGMENTS):
        x_rows = hb[slot_tok[s * seg_blk * MOE_ROWS:(s + 1) * seg_blk * MOE_ROWS]]
        y = moe_ffn(x_rows, blk_expert, n_used, w1, w3, w2, layer, s * seg_blk, y)
    d2 = dest.reshape(n, TOP_K)
    return y[d2[:, 0]], y[d2[:, 1]], gate


def _moe_combine_kernel(ya_ref, yb_ref, gate_ref, res_ref, gt_ref, o_ref):
    gate = gate_ref[...]
    f = ya_ref[...].astype(F32) * gate[:, 0:1] + yb_ref[...].astype(F32) * gate[:, 1:2]
    o_ref[...] = res_ref[...] + gt_ref[0] * f


def moe_combine(ya, yb, gate, res, gt, row0, tm=512):
    M, D = res.shape
    tm = min(tm, M)
    nb = gt.shape[0]
    rpb = M // nb
    assert M % tm == 0 and row0 % tm == 0 and rpb % tm == 0
    r0 = row0 // tm
    row_spec = lambda w: pl.BlockSpec((tm, w), lambda i: (r0 + i, 0))
    return pl.pallas_call(
        _moe_combine_kernel,
        out_shape=jax.ShapeDtypeStruct((M, D), F32),
        grid=(M // tm,),
        in_specs=[row_spec(D), row_spec(D), row_spec(TOP_K),
                  pl.BlockSpec((tm, D), lambda i: (i, 0)),
                  pl.BlockSpec((1, 1, D), lambda i: ((i * tm) // rpb, 0, 0))],
        out_specs=pl.BlockSpec((tm, D), lambda i: (i, 0)),
        compiler_params=_params("parallel"),
        name="moe_combine",
    )(ya, yb, gate.astype(F32), res, gt.reshape(nb, 1, D).astype(F32))


def moe_router(x, norm_g, sc, sh, w_grp, w_exp):
    w_route = jnp.concatenate([w_grp, w_exp], axis=1).astype(F32)
    w_route = jnp.pad(w_route, ((0, 0), (0, LANES - w_route.shape[1])))
    return linear(x, w_route, norm=(norm_g, sc, sh), precise="split", emit_h=True)


def attention_layer(x, xc, norm_g, sc, sh, csc, csh, w_in, w_out, a_q_gain, a_k_gain, a_sink, b_q_gain, b_k_gain,
                    b_rpb, gt, cgt):
    B, L, D = x.shape
    C = xc.shape[1]
    gains = (a_q_gain, a_k_gain, b_q_gain, b_k_gain)
    qkv = qkv_projection(x.reshape(B * L, D), w_in, norm_g, sc, sh, gains, L).reshape(B, L, -1)
    qkv_c = qkv_projection(xc.reshape(B * C, D), w_in, norm_g, csc, csh, gains, None).reshape(B, C, -1)
    ya = window_attention(qkv, qkv_c, a_sink)
    yb = neighbourhood_attention(qkv, qkv_c, b_rpb)
    x_new = linear(ya.reshape(B * L, -1), w_out, x2=yb.reshape(B * L, -1),
                   gated=(x.reshape(B * L, D), gt)).reshape(B, L, D)
    mix_c = context_attention(qkv_c, a_sink)
    xc_new = linear(mix_c.reshape(B * C, -1), w_out, gated=(xc.reshape(B * C, D), cgt)).reshape(B, C, D)
    return x_new, xc_new


def rwkv_layer(x, xc, norm_g, sc, sh, csc, csh, gt, mu, wr, wk, wv, wo, w0, w1, w2, a0, a1, a2, g1, g2, k_k, k_a,
               r_k, lnx_w, lnx_b):
    B, L, D = x.shape
    C = xc.shape[1]
    T = L + 2 * C

    xcat = jnp.concatenate([xc, x, xc], axis=1).reshape(B * T, D)
    nl, na = w1.shape[2], a1.shape[2]
    assert 2 * nl <= RW_TN and 2 * na <= RW_TN and g1.shape[1] <= RW_TN
    proj = rwkv_projections(xcat, norm_g, sc, sh, csc, csh, mu,
                            (wr, wk, wv, g1, jnp.concatenate([w1[0], w1[1]], axis=1),
                             jnp.concatenate([a1[0], a1[1]], axis=1)), T, C)
    rows = lambda w, lo: jnp.pad(w, ((lo, RW_TN - lo - w.shape[0]), (0, 0))).astype(BF16)
    w2p = [rows(w2[0], 0), rows(w2[1], nl)]
    a2p = [rows(a2[0], 0), rows(a2[1], na)]
    vecs = lambda d: jnp.stack([w0[d], a0[d], k_k, k_a, a0[1 - d], r_k.reshape(D), lnx_w, lnx_b]).astype(F32)
    proj = proj.reshape(B, T, -1)
    y_fwd = wkv_scan(proj, w2p[0], a2p[0], vecs(0), ctx_len=C, reverse=False)
    z = wkv_scan(proj, w2p[1], a2p[1], vecs(1), ctx_len=C, reverse=True, a2_other=a2p[0], y_fwd=y_fwd,
                 g2=rows(g2, 0))
    return linear(z.reshape(B * L, D), wo, gated=(x.reshape(B * L, D), gt)).reshape(B, L, D)


def kernel(x, c, ctx, c_ctx, ada_w, ada_b, norm_mix_g, norm_ffn_g, attn_w_in, attn_w_out, a_q_gain, a_k_gain, a_sink, b_q_gain, b_k_gain, b_rpb, rw_mu, rw_wr, rw_wk, rw_wv, rw_wo, rw_w0, rw_w1, rw_w2, rw_a0, rw_a1, rw_a2, rw_g1, rw_g2, rw_k_k, rw_k_a, rw_r_k, rw_lnx_w, rw_lnx_b, moe_w_grp, moe_w_exp, moe_w1, moe_w3, moe_w2):
    B, L, D = x.shape
    C = ctx.shape[1]
    depth = ada_w.shape[0]
    assert depth == 2, "layer 0 is the attention mixer, layer 1 (last) the RWKV-7 mixer"
    pad = (-(B + 1)) % 8
    cond = jnp.concatenate([jax.nn.silu(c), jax.nn.silu(c_ctx)[None, :], jnp.zeros((pad, D), F32)], axis=0)
    xc = ctx
    for i in range(depth):
        last = i == depth - 1
        j = i // 2
        mod = linear(cond, ada_w, precise="highest", layer=i) + ada_b[i]
        sh1, sc1, gt1, sh2, sc2, gt2 = jnp.split(mod[:B], 6, axis=-1)
        csh1, csc1, cgt1, csh2, csc2, cgt2 = jnp.split(mod[B:B + 1], 6, axis=-1)
        if i % 2 == 0:
            x, xc = attention_layer(x, xc, norm_mix_g[i], sc1, sh1, csc1, csh1, attn_w_in[j], attn_w_out[j],
                                    a_q_gain[j], a_k_gain[j], a_sink[j], b_q_gain[j], b_k_gain[j], b_rpb[j],
                                    gt1, cgt1)
        else:
            x = rwkv_layer(x, xc, norm_mix_g[i], sc1, sh1, csc1, csh1, gt1, rw_mu[j], rw_wr[j], rw_wk[j], rw_wv[j],
                           rw_wo[j], rw_w0[j], rw_w1[j], rw_w2[j], rw_a0[j], rw_a1[j], rw_a2[j], rw_g1[j], rw_g2[j],
                           rw_k_k[j], rw_k_a[j], rw_r_k[j], rw_lnx_w[j], rw_lnx_b[j])
        route, hb = moe_router(x.reshape(B * L, D), norm_ffn_g[i], sc2, sh2, moe_w_grp[i], moe_w_exp[i])
        if last:
            ya, yb, gate = hierarchical_moe(route, hb, moe_w1, moe_w3, moe_w2, i)
            x = moe_combine(ya, yb, gate, x.reshape(B * L, D), gt2, 0).reshape(B, L, D)
        else:
            route_c, hb_c = moe_router(xc.reshape(B * C, D), norm_ffn_g[i], csc2, csh2, moe_w_grp[i], moe_w_exp[i])
            ya, yb, gate = hierarchical_moe(jnp.concatenate([route, route_c], axis=0),
                                            jnp.concatenate([hb, hb_c], axis=0), moe_w1, moe_w3, moe_w2, i)
            x = moe_combine(ya, yb, gate, x.reshape(B * L, D), gt2, 0).reshape(B, L, D)
            xc = moe_combine(ya, yb, gate, xc.reshape(B * C, D), cgt2, B * L).reshape(B, C, D)
    return x
```
